```python
import math
import numpy as np
import jax
import jax.numpy as jnp
from jax import lax

D_MODEL = 2048
BATCH = 2
SEQ = 8192
DEPTH = 4

GRID_W = 64
N_MIXERS = 3
NORM_EPS = 1e-6
NEG_INF = -1e30

DA_HEADS = 8
DA_HEAD_DIM = 128
DA_WIDTH = DA_HEADS * 2 * DA_HEAD_DIM
DA_Q_BLOCK = 128

GLA_HEADS = 4
GLA_KEY_DIM = D_MODEL // 2
GLA_VAL_DIM = D_MODEL
GLA_DK = GLA_KEY_DIM // GLA_HEADS
GLA_DV = GLA_VAL_DIM // GLA_HEADS
GLA_GATE_RANK = 16
GLA_TAU = 16.0
GLA_CHUNK = 64

NA_HEADS = 64
NA_HEAD_DIM = 32
NA_WIDTH = NA_HEADS * NA_HEAD_DIM
NA_ROWS = 8
NA_COLS = 16

MOE_GROUPS = 4
MOE_EXPERTS_PER_GROUP = 8
MOE_EXPERTS = MOE_GROUPS * MOE_EXPERTS_PER_GROUP
MOE_TOP_K = 2
MOE_D_FF = 512
MOE_BLOCK = 128

kernel_name = "hybrid_diffattn_gla_natten_hmoe_encoder"


def rms_norm(x, g):
    xf = x.astype(jnp.float32)
    y = xf * lax.rsqrt(jnp.mean(xf * xf, axis=-1, keepdims=True) + NORM_EPS)
    return (y * g.astype(jnp.float32)).astype(x.dtype)


def lambda_init(layer_idx):
    return 0.8 - 0.6 * math.exp(-0.3 * layer_idx)


def alibi_slopes(n_heads):
    return jnp.asarray(2.0 ** (-8.0 * np.arange(1, n_heads + 1) / n_heads), dtype=jnp.float32)


def diff_attention(h, w_in, w_out, lam_q1, lam_k1, lam_q2, lam_k2, subln_g, lam_init):
    bsz, t, _ = h.shape
    f32 = jnp.float32
    q, k, v = jnp.split(h @ w_in, 3, axis=-1)
    q = q.reshape(bsz, t, DA_HEADS, 2, DA_HEAD_DIM) * DA_HEAD_DIM ** -0.5
    k = k.reshape(bsz, t, DA_HEADS, 2, DA_HEAD_DIM)
    v = v.reshape(bsz, t, DA_HEADS, 2 * DA_HEAD_DIM)
    lam = (jnp.exp(jnp.sum(lam_q1.astype(f32) * lam_k1.astype(f32)))
           - jnp.exp(jnp.sum(lam_q2.astype(f32) * lam_k2.astype(f32))) + lam_init)
    slopes = alibi_slopes(DA_HEADS)
    n_blk = t // DA_Q_BLOCK
    q_blocks = q.reshape(bsz, n_blk, DA_Q_BLOCK, DA_HEADS, 2, DA_HEAD_DIM).transpose(1, 0, 3, 4, 2, 5)
    k_t = k.transpose(0, 2, 3, 1, 4)
    v_t = v.transpose(0, 2, 1, 3)
    key_pos = jnp.arange(t)

    def block(args):
        q_blk, start = args
        s = jnp.einsum('bhiqd,bhikd->bhiqk', q_blk, k_t).astype(f32)
        q_pos = start + jnp.arange(DA_Q_BLOCK)
        dist = jnp.abs(q_pos[:, None] - key_pos[None, :]).astype(f32)
        p = jax.nn.softmax(s - slopes[None, :, None, None, None] * dist, axis=-1)
        a = p[:, :, 0] - lam * p[:, :, 1]
        return jnp.einsum('bhqk,bhkv->bhqv', a.astype(v_t.dtype), v_t)

    o = lax.map(block, (q_blocks, jnp.arange(n_blk, dtype=jnp.int32) * DA_Q_BLOCK))
    o = o.transpose(1, 0, 3, 2, 4).reshape(bsz, t, DA_HEADS, 2 * DA_HEAD_DIM)
    o = rms_norm(o, subln_g) * (1.0 - lam_init)
    return o.reshape(bsz, t, DA_WIDTH) @ w_out


def _gla_chunked(q, k, v, log_a):
    bsz, nh, t, dk = q.shape
    dv = v.shape[-1]
    cs = GLA_CHUNK
    n = t // cs
    q = q.reshape(bsz, nh, n, cs, dk)
    k = k.reshape(bsz, nh, n, cs, dk)
    v = v.reshape(bsz, nh, n, cs, dv)
    b = jnp.cumsum(log_a.reshape(bsz, nh, n, cs, dk), axis=3)
    b_end = b[:, :, :, -1:, :]
    q_in = q * jnp.exp(b)
    k_in = k * jnp.exp(-b)
    k_out = k * jnp.exp(b_end - b)
    lower = jnp.tril(jnp.ones((cs, cs), dtype=bool))
    att = jnp.where(lower, jnp.einsum('bhncd,bhnsd->bhncs', q_in, k_in), 0.0)
    o_intra = jnp.einsum('bhncs,bhnsv->bhncv', att, v)

    def step(state, xs):
        q_c, k_c, v_c, dec_c = xs
        o_c = jnp.einsum('bhcd,bhdv->bhcv', q_c, state)
        state = dec_c[..., None] * state + jnp.einsum('bhcd,bhcv->bhdv', k_c, v_c)
        return state, o_c

    xs = (jnp.moveaxis(q_in, 2, 0), jnp.moveaxis(k_out, 2, 0), jnp.moveaxis(v, 2, 0),
          jnp.moveaxis(jnp.exp(b_end[:, :, :, 0, :]), 2, 0))
    state0 = jnp.zeros((bsz, nh, dk, dv), jnp.float32)
    _, o_inter = lax.scan(step, state0, xs)
    return (o_intra + jnp.moveaxis(o_inter, 0, 2)).reshape(bsz, nh, t, dv)


def gla_mixer(h, w_in, gate_w2_f, gate_b_f, gate_w2_b, gate_b_b, norm_g, w_out):
    bsz, t, _ = h.shape
    f32 = jnp.float32
    cuts = [GLA_KEY_DIM, 2 * GLA_KEY_DIM, 2 * GLA_KEY_DIM + GLA_VAL_DIM, 2 * GLA_KEY_DIM + 2 * GLA_VAL_DIM]
    q, k, v, r, g_lr = jnp.split(h @ w_in, cuts, axis=-1)
    g_f, g_b = jnp.split(g_lr, 2, axis=-1)
    log_a_f = jax.nn.log_sigmoid((g_f @ gate_w2_f + gate_b_f).astype(f32)) / GLA_TAU
    log_a_b = jax.nn.log_sigmoid((g_b @ gate_w2_b + gate_b_b).astype(f32)) / GLA_TAU

    def heads(z, dh):
        return z.astype(f32).reshape(bsz, t, GLA_HEADS, dh).transpose(0, 2, 1, 3)

    qh = heads(q, GLA_DK) * GLA_DK ** -0.5
    kh = heads(k, GLA_DK)
    vh = heads(v, GLA_DV)
    o_fwd = _gla_chunked(qh, kh, vh, heads(log_a_f, GLA_DK))
    flip = lambda z: jnp.flip(z, axis=2)
    o_bwd = flip(_gla_chunked(flip(qh), flip(kh), flip(vh), flip(heads(log_a_b, GLA_DK))))
    o = (o_fwd + o_bwd).transpose(0, 2, 1, 3)
    o = rms_norm(o, norm_g) * jax.nn.silu(r.astype(f32)).reshape(bsz, t, GLA_HEADS, GLA_DV)
    return o.reshape(bsz, t, GLA_VAL_DIM).astype(h.dtype) @ w_out


def neighborhood_attention(h, w_in, rpb, w_out):
    bsz, t, _ = h.shape
    rows = t // GRID_W
    kr = min(NA_ROWS, rows)
    q, k, v = jnp.split(h @ w_in, 3, axis=-1)
    shape5 = (bsz, rows, GRID_W, NA_HEADS, NA_HEAD_DIM)
    q = q.reshape(shape5) * NA_HEAD_DIM ** -0.5
    k = k.reshape(shape5)
    v = v.reshape(shape5)
    col = np.arange(GRID_W)
    col_start = np.clip(col - NA_COLS // 2, 0, GRID_W - NA_COLS)
    in_window = (col[None, :] >= col_start[:, None]) & (col[None, :] < col_start[:, None] + NA_COLS)
    col_off = np.clip(col[None, :] - col[:, None] + NA_COLS - 1, 0, 2 * NA_COLS - 2)

    def row_block(r):
        r0 = jnp.clip(r - kr // 2, 0, rows - kr)
        k_band = lax.dynamic_slice_in_dim(k, r0, kr, axis=1)
        v_band = lax.dynamic_slice_in_dim(v, r0, kr, axis=1)
        q_row = lax.dynamic_index_in_dim(q, r, axis=1, keepdims=False)
        s = jnp.einsum('bqhd,bikhd->bhqik', q_row, k_band).astype(jnp.float32)
        row_off = r0 + jnp.arange(kr) - r + NA_ROWS - 1
        bias = rpb[:, row_off][:, :, col_off].transpose(0, 2, 1, 3)
        s = jnp.where(in_window[None, None, :, None, :], s + bias[None].astype(jnp.float32), NEG_INF)
        p = jax.nn.softmax(s.reshape(bsz, NA_HEADS, GRID_W, kr * GRID_W), axis=-1).reshape(s.shape)
        return jnp.einsum('bhqik,bikhd->bqhd', p.astype(v_band.dtype), v_band)

    o = lax.map(row_block, jnp.arange(rows, dtype=jnp.int32))
    o = o.transpose(1, 0, 2, 3, 4).reshape(bsz, t, NA_WIDTH)
    return o @ w_out


def _experts_sorted(hf, expert_id, weight, w_gu, w_d):
    n_tok, d = hf.shape
    n_exp = w_gu.shape[0]
    flat_e = expert_id.reshape(-1)
    n_asg = flat_e.shape[0]
    order = jnp.argsort(flat_e)
    sorted_e = flat_e[order]
    tok = order // MOE_TOP_K
    counts = jnp.bincount(flat_e, length=n_exp)
    start = jnp.cumsum(counts) - counts
    padded = (counts + MOE_BLOCK - 1) // MOE_BLOCK * MOE_BLOCK
    pad_end = jnp.cumsum(padded)
    pad_start = pad_end - padded
    slot = pad_start[sorted_e] + jnp.arange(n_asg) - start[sorted_e]
    n_blocks = -(-n_asg // MOE_BLOCK) + n_exp
    buf = jnp.zeros((n_blocks * MOE_BLOCK, d), hf.dtype).at[slot].set(hf[tok])
    block_e = jnp.minimum(jnp.searchsorted(pad_end, jnp.arange(n_blocks) * MOE_BLOCK, side='right'), n_exp - 1)

    def block(args):
        xb, e = args
        g, u = jnp.split(xb @ w_gu[e], 2, axis=-1)
        return (jax.nn.silu(g) * u) @ w_d[e]

    out = lax.map(block, (buf.reshape(n_blocks, MOE_BLOCK, d), block_e)).reshape(-1, d)
    contrib = out[slot] * weight.reshape(-1)[order][:, None].astype(out.dtype)
    return jnp.zeros((n_tok, d), out.dtype).at[tok].add(contrib)


def hierarchical_moe(h, rg_w, rg_b, re_w, re_b, w_gu, w_d):
    bsz, t, d = h.shape
    f32 = jnp.float32
    hf = h.reshape(-1, d)
    n_tok = hf.shape[0]
    g_prob = jax.nn.softmax((hf @ rg_w).astype(f32) + rg_b.astype(f32), axis=-1)
    g_p, g_idx = lax.top_k(g_prob, 1)
    e_logits = ((hf @ re_w).astype(f32) + re_b.astype(f32)).reshape(n_tok, MOE_GROUPS, MOE_EXPERTS_PER_GROUP)
    e_logits = e_logits[jnp.arange(n_tok), g_idx[:, 0]]
    e_p, e_idx = lax.top_k(jax.nn.softmax(e_logits, axis=-1), MOE_TOP_K)
    e_p = e_p / jnp.sum(e_p, axis=-1, keepdims=True)
    weight = g_p * e_p
    expert_id = g_idx * MOE_EXPERTS_PER_GROUP + e_idx
    return _experts_sorted(hf, expert_id, weight, w_gu, w_d).reshape(bsz, t, d)


def setup_inputs(seed: int = 0) -> dict:
    key = jax.random.key(seed)
    keys = jax.random.split(key, 40)
    D = D_MODEL
    n_da = len(range(0, DEPTH, N_MIXERS))
    n_gla = len(range(1, DEPTH, N_MIXERS))
    n_na = len(range(2, DEPTH, N_MIXERS))

    def nrm(i, shape, scale):
        return jax.random.normal(keys[i], shape, jnp.float32) * scale

    def gain(i, shape):
        return 1.0 + nrm(i, shape, 0.05)

    gla_in_cols = 2 * GLA_KEY_DIM + 2 * GLA_VAL_DIM + 2 * GLA_GATE_RANK
    return {
        "x": nrm(0, (BATCH, SEQ, D), 1.0),
        "c": nrm(1, (BATCH, D), 1.0),
        "ada_w": nrm(2, (DEPTH, D, 6 * D), 0.5 * D ** -0.5),
        "ada_b": nrm(3, (DEPTH, 6 * D), 0.02),
        "norm1_g": gain(4, (DEPTH, D)),
        "norm2_g": gain(5, (DEPTH, D)),
        "router_g_w": nrm(6, (DEPTH, D, MOE_GROUPS), D ** -0.5),
        "router_g_b": nrm(7, (DEPTH, MOE_GROUPS), 0.01),
        "router_e_w": nrm(8, (DEPTH, D, MOE_EXPERTS), D ** -0.5),
        "router_e_b": nrm(9, (DEPTH, MOE_EXPERTS), 0.01),
        "moe_w_gu": nrm(10, (DEPTH, MOE_EXPERTS, D, 2 * MOE_D_FF), D ** -0.5),
        "moe_w_d": nrm(11, (DEPTH, MOE_EXPERTS, MOE_D_FF, D), MOE_D_FF ** -0.5),
        "da_w_in": nrm(12, (n_da, D, 3 * DA_WIDTH), D ** -0.5),
        "da_w_out": nrm(13, (n_da, DA_WIDTH, D), DA_WIDTH ** -0.5),
        "da_lam_q1": nrm(14, (n_da, DA_HEAD_DIM), 0.1),
        "da_lam_k1": nrm(15, (n_da, DA_HEAD_DIM), 0.1),
        "da_lam_q2": nrm(16, (n_da, DA_HEAD_DIM), 0.1),
        "da_lam_k2": nrm(17, (n_da, DA_HEAD_DIM), 0.1),
        "da_subln_g": gain(18, (n_da, 2 * DA_HEAD_DIM)),
        "gla_w_in": nrm(19, (n_gla, D, gla_in_cols), D ** -0.5),
        "gla_gate_w2_f": nrm(20, (n_gla, GLA_GATE_RANK, GLA_KEY_DIM), GLA_GATE_RANK ** -0.5),
        "gla_gate_b_f": nrm(21, (n_gla, GLA_KEY_DIM), 0.1),
        "gla_gate_w2_b": nrm(22, (n_gla, GLA_GATE_RANK, GLA_KEY_DIM), GLA_GATE_RANK ** -0.5),
        "gla_gate_b_b": nrm(23, (n_gla, GLA_KEY_DIM), 0.1),
        "gla_norm_g": gain(24, (n_gla, GLA_DV)),
        "gla_w_out": nrm(25, (n_gla, GLA_VAL_DIM, D), GLA_VAL_DIM ** -0.5),
        "na_w_in": nrm(26, (n_na, D, 3 * NA_WIDTH), D ** -0.5),
        "na_rpb": nrm(27, (n_na, NA_HEADS, 2 * NA_ROWS - 1, 2 * NA_COLS - 1), 0.1),
        "na_w_out": nrm(28, (n_na, NA_WIDTH, D), NA_WIDTH ** -0.5),
        "final_g": gain(29, (D,)),
    }


def reference(x, c, ada_w, ada_b, norm1_g, norm2_g, router_g_w, router_g_b, router_e_w, router_e_b,
              moe_w_gu, moe_w_d, da_w_in, da_w_out, da_lam_q1, da_lam_k1, da_lam_q2, da_lam_k2, da_subln_g,
              gla_w_in, gla_gate_w2_f, gla_gate_b_f, gla_gate_w2_b, gla_gate_b_b, gla_norm_g, gla_w_out,
              na_w_in, na_rpb, na_w_out, final_g):
    cond = jax.nn.silu(c)
    for i in range(DEPTH):
        mod = (cond @ ada_w[i] + ada_b[i])[:, None, :]
        sh1, sc1, g1, sh2, sc2, g2 = jnp.split(mod, 6, axis=-1)
        hm = rms_norm(x, norm1_g[i]) * (1.0 + sc1) + sh1
        kind = i % N_MIXERS
        j = i // N_MIXERS
        if kind == 0:
            y = diff_attention(hm, da_w_in[j], da_w_out[j], da_lam_q1[j], da_lam_k1[j],
                               da_lam_q2[j], da_lam_k2[j], da_subln_g[j], lambda_init(i))
        elif kind == 1:
            y = gla_mixer(hm, gla_w_in[j], gla_gate_w2_f[j], gla_gate_b_f[j], gla_gate_w2_b[j],
                          gla_gate_b_b[j], gla_norm_g[j], gla_w_out[j])
        else:
            y = neighborhood_attention(hm, na_w_in[j], na_rpb[j], na_w_out[j])
        x = x + g1 * y
        hf = rms_norm(x, norm2_g[i]) * (1.0 + sc2) + sh2
        x = x + g2 * hierarchical_moe(hf, router_g_w[i], router_g_b[i], router_e_w[i], router_e_b[i],
                                      moe_w_gu[i], moe_w_d[i])
    return rms_norm(x, final_g)
```

```python
import functools
import math

import numpy as np
import jax
import jax.numpy as jnp
from jax import lax
from jax.experimental import pallas as pl
from jax.experimental.pallas import tpu as pltpu

F32 = jnp.float32
BF16 = jnp.bfloat16

D_MODEL = 2048
DEPTH = 4
N_MIXERS = 3
NORM_EPS = 1e-6
NEG_INF = -1e30
LOG2E = 1.4426950408889634

DA_HEADS = 8
DA_HEAD_DIM = 128
DA_VAL_DIM = 2 * DA_HEAD_DIM

GLA_HEADS = 4
GLA_DK = 256
GLA_DV = 512
GLA_GATE_RANK = 16
GLA_TAU = 16.0
GLA_CHUNK = 64

GRID_W = 64
NA_HEADS = 64
NA_HEAD_DIM = 32
NA_ROWS = 8
NA_COLS = 16
NA_GROUP = 4
NA_QROWS = 8

MOE_GROUPS = 4
MOE_EPG = 8
MOE_EXPERTS = 32
MOE_TOP_K = 2
MOE_D_FF = 512
MOE_BLK = 256

LANE = 128
VMEM_LIMIT = 56 * 1024 * 1024


def _cparams(sem):
    return pltpu.CompilerParams(dimension_semantics=sem, vmem_limit_bytes=VMEM_LIMIT)


def _split2(v):
    hi = v.astype(BF16)
    lo = (v - hi.astype(F32)).astype(BF16)
    return hi, lo


def _split3(v):
    hi = v.astype(BF16)
    r = v - hi.astype(F32)
    mid = r.astype(BF16)
    lo = (r - mid.astype(F32)).astype(BF16)
    return hi, mid, lo


def _bf16_parts(v, n):
    parts = []
    r = v.astype(F32)
    for _ in range(n):
        top = lax.bitcast_convert_type(
            lax.bitcast_convert_type(r, jnp.uint32) & jnp.uint32(0xFFFF0000), F32)
        parts.append(top.astype(BF16))
        r = r - top
    return parts


def _nt_dot(a, b):
    return lax.dot_general(a, b, (((1,), (1,)), ((), ())), preferred_element_type=F32)


def _tn_dot(a, b):
    return lax.dot_general(a, b, (((0,), (0,)), ((), ())), preferred_element_type=F32)


def _rms(x):
    return x * lax.rsqrt(jnp.mean(x * x, axis=-1, keepdims=True) + NORM_EPS)


def _ada_kernel(c_ref, w_ref, b_ref, o_ref):
    c = c_ref[...]
    cond = (c * jax.nn.sigmoid(c)).astype(BF16)
    o_ref[...] = jnp.dot(cond, w_ref[...].astype(BF16), preferred_element_type=F32) + b_ref[...]


def _ada_mod(c, ada_w, ada_b):
    depth, d, n = ada_w.shape
    bsz = c.shape[0]
    rows = 8
    c8 = jnp.zeros((rows, d), F32).at[:bsz].set(c)
    tn = 1024
    out = pl.pallas_call(
        _ada_kernel,
        grid=(depth, n // tn),
        in_specs=[
            pl.BlockSpec((rows, d), lambda l, j: (0, 0)),
            pl.BlockSpec((None, d, tn), lambda l, j: (l, 0, j)),
            pl.BlockSpec((None, 1, tn), lambda l, j: (l, 0, j)),
        ],
        out_specs=pl.BlockSpec((None, rows, tn), lambda l, j: (l, 0, j)),
        out_shape=jax.ShapeDtypeStruct((depth, rows, n), F32),
        compiler_params=_cparams(("parallel", "parallel")),
        name="ada_mod",
    )(c8, ada_w, ada_b.reshape(depth, 1, n))
    return out[:, :bsz]


def _proj_kernel(x_ref, g_ref, sc_ref, sh_ref, w_ref, cs_ref, o_ref, h_scr):
    @pl.when(pl.program_id(2) == 0)
    def _():
        h = _rms(x_ref[...]) * g_ref[...] * (1.0 + sc_ref[...]) + sh_ref[...]
        h_scr[...] = h.astype(BF16)

    acc = jnp.dot(h_scr[...], w_ref[...], preferred_element_type=F32)
    o_ref[...] = (acc * cs_ref[...]).astype(o_ref.dtype)


def _norm_proj(x, g, sc, sh, w, col_scale, *, tm, tn, out_dtype):
    bsz, t, d = x.shape
    n = w.shape[1]
    tm = min(tm, t)
    return pl.pallas_call(
        _proj_kernel,
        grid=(bsz, t // tm, n // tn),
        in_specs=[
            pl.BlockSpec((None, tm, d), lambda b, i, j: (b, i, 0)),
            pl.BlockSpec((1, d), lambda b, i, j: (0, 0)),
            pl.BlockSpec((None, 1, d), lambda b, i, j: (b, 0, 0)),
            pl.BlockSpec((None, 1, d), lambda b, i, j: (b, 0, 0)),
            pl.BlockSpec((d, tn), lambda b, i, j: (0, j)),
            pl.BlockSpec((1, tn), lambda b, i, j: (0, j)),
        ],
        out_specs=pl.BlockSpec((None, tm, tn), lambda b, i, j: (b, i, j)),
        out_shape=jax.ShapeDtypeStruct((bsz, t, n), out_dtype),
        scratch_shapes=[pltpu.VMEM((tm, d), BF16)],
        compiler_params=_cparams(("parallel", "parallel", "arbitrary")),
        name="norm_proj",
    )(x, g.reshape(1, d), sc, sh, w, col_scale.reshape(1, n))


def _da_kernel(c_ref, q_ref, k_ref, v_ref, qf_ref, kf_ref, u_ref, lam_ref, g_ref, o_ref,
               qa_scr, m_scr, l_scr, acc_scr, *, lam_init):
    h = pl.program_id(1)
    qi = pl.program_id(2)
    kb = pl.program_id(3)
    hd = DA_HEAD_DIM

    @pl.when(kb == 0)
    def _():
        m_scr[...] = jnp.full(m_scr.shape, -jnp.inf, F32)
        l_scr[...] = jnp.zeros(l_scr.shape, F32)
        acc_scr[...] = jnp.zeros(acc_scr.shape, F32)
        q = q_ref[...]
        qf = qf_ref[...]
        for m in range(2):
            qm = q[:, m * hd:(m + 1) * hd]
            qa_scr[m] = jnp.concatenate([qm, qf], axis=1)

    def step(diag):
        sgn = jnp.where(kb > qi, -1.0, 1.0).astype(BF16)
        k = k_ref[...]
        kf = kf_ref[...] * sgn
        v = v_ref[...]
        for m in range(2):
            ka = jnp.concatenate([k[:, m * hd:(m + 1) * hd], kf], axis=1)
            s = _nt_dot(qa_scr[m], ka)
            if diag:
                s = s - (2.0 * c_ref[h]) * u_ref[...]
            m_old = m_scr[m]
            m_new = jnp.maximum(m_old, jnp.max(s, axis=1, keepdims=True))
            alpha = jnp.exp2(m_old - m_new)
            p = jnp.exp2(s - m_new)
            l_scr[m] = alpha * l_scr[m] + jnp.sum(p, axis=1, keepdims=True)
            acc_scr[m] = alpha * acc_scr[m] + jnp.dot(p.astype(BF16), v, preferred_element_type=F32)
            m_scr[m] = m_new

    @pl.when(kb != qi)
    def _():
        step(False)

    @pl.when(kb == qi)
    def _():
        step(True)

    @pl.when(kb == pl.num_programs(3) - 1)
    def _():
        lv = lam_ref[...]
        lam = (jnp.exp(jnp.sum(lv[0:1] * lv[1:2], axis=1, keepdims=True))
               - jnp.exp(jnp.sum(lv[2:3] * lv[3:4], axis=1, keepdims=True)) + lam_init)
        o = acc_scr[0] / l_scr[0] - lam * (acc_scr[1] / l_scr[1])
        o = _rms(o) * g_ref[...] * (1.0 - lam_init)
        o_ref[...] = o.astype(o_ref.dtype)


def _alibi_features(t, n_heads):
    slopes = 2.0 ** (-8.0 * np.arange(1, n_heads + 1) / n_heads)
    c = jnp.asarray(slopes * LOG2E, F32)
    cp = c[:, None] * jnp.arange(t, dtype=F32)[None, :]
    p3 = jnp.stack(_bf16_parts(cp, 3), axis=-1)
    ones = jnp.ones((n_heads, t, 3), BF16)
    pad = jnp.zeros((n_heads, t, LANE - 6), BF16)
    qf = jnp.concatenate([-p3, ones, pad], axis=-1)
    kf = jnp.concatenate([ones, p3, pad], axis=-1)
    return c, qf, kf


def _diff_attention(qkv, lam_vecs, subln_g, lam_init, *, tile):
    bsz, t, _ = qkv.shape
    nh = DA_HEADS
    tq = tk = min(tile, t)
    c, qf, kf = _alibi_features(t, nh)
    ii = jnp.arange(tq, dtype=F32)
    u = jnp.maximum(ii[None, :] - ii[:, None], 0.0)
    grid_spec = pltpu.PrefetchScalarGridSpec(
        num_scalar_prefetch=1,
        grid=(bsz, nh, t // tq, t // tk),
        in_specs=[
            pl.BlockSpec((None, tq, DA_VAL_DIM), lambda b, h, i, j, c: (b, i, h)),
            pl.BlockSpec((None, tk, DA_VAL_DIM), lambda b, h, i, j, c: (b, j, nh + h)),
            pl.BlockSpec((None, tk, DA_VAL_DIM), lambda b, h, i, j, c: (b, j, 2 * nh + h)),
            pl.BlockSpec((None, tq, LANE), lambda b, h, i, j, c: (h, i, 0)),
            pl.BlockSpec((None, tk, LANE), lambda b, h, i, j, c: (h, j, 0)),
            pl.BlockSpec((tq, tk), lambda b, h, i, j, c: (0, 0)),
            pl.BlockSpec((4, DA_HEAD_DIM), lambda b, h, i, j, c: (0, 0)),
            pl.BlockSpec((1, DA_VAL_DIM), lambda b, h, i, j, c: (0, 0)),
        ],
        out_specs=pl.BlockSpec((None, tq, DA_VAL_DIM), lambda b, h, i, j, c: (b, i, h)),
        scratch_shapes=[
            pltpu.VMEM((2, tq, 2 * DA_HEAD_DIM), BF16),
            pltpu.VMEM((2, tq, 1), F32),
            pltpu.VMEM((2, tq, 1), F32),
            pltpu.VMEM((2, tq, DA_VAL_DIM), F32),
        ],
    )
    return pl.pallas_call(
        functools.partial(_da_kernel, lam_init=lam_init),
        grid_spec=grid_spec,
        out_shape=jax.ShapeDtypeStruct((bsz, t, nh * DA_VAL_DIM), BF16),
        compiler_params=_cparams(("parallel", "parallel", "parallel", "arbitrary")),
        name="diff_attention",
    )(c, qkv, qkv, qkv, qf, kf, u, lam_vecs, subln_g.reshape(1, DA_VAL_DIM))


def _gla_kernel(q_ref, k_ref, v_ref, gl_ref, w2h_ref, w2l_ref, gb_ref, o_ref, st_scr, *, reverse, nchunk):
    cs = GLA_CHUNK

    @pl.when(pl.program_id(2) == 0)
    def _():
        st_scr[...] = jnp.zeros(st_scr.shape, F32)

    ri = lax.broadcasted_iota(jnp.int32, (cs, cs), 0)
    ci = lax.broadcasted_iota(jnp.int32, (cs, cs), 1)
    keep = (ci >= ri) if reverse else (ci <= ri)
    tri = jnp.where(keep, 1.0, 0.0).astype(BF16)
    w2h = w2h_ref[...]
    w2l = w2l_ref[...]
    order = range(nchunk - 1, -1, -1) if reverse else range(nchunk)
    for c in order:
        rows = slice(c * cs, (c + 1) * cs)
        gh, glo = _split2(gl_ref[rows, :])
        x = (jnp.dot(gh, w2h, preferred_element_type=F32) + jnp.dot(glo, w2h, preferred_element_type=F32)
             + jnp.dot(gh, w2l, preferred_element_type=F32) + gb_ref[...])
        la = (jnp.minimum(x, 0.0) - jnp.log1p(jnp.exp(-jnp.abs(x)))) * (1.0 / GLA_TAU)
        a1, a2, a3 = _split3(la)
        b = (jnp.dot(tri, a1, preferred_element_type=F32) + jnp.dot(tri, a2, preferred_element_type=F32)
             + jnp.dot(tri, a3, preferred_element_type=F32))
        tot = b[0:1] if reverse else b[cs - 1:cs]
        q = q_ref[rows, :].astype(F32)
        k = k_ref[rows, :].astype(F32)
        v = v_ref[rows, :]
        q_in = (q * jnp.exp(b)).astype(BF16)
        k_in = (k * jnp.exp(-b)).astype(BF16)
        k_out = (k * jnp.exp(tot - b)).astype(BF16)
        att = jnp.where(keep, _nt_dot(q_in, k_in), 0.0)
        st = st_scr[...]
        o = jnp.dot(att.astype(BF16), v, preferred_element_type=F32) + _nt_dot(q_in, st.astype(BF16))
        o_ref[rows, :] = o
        st_scr[...] = st * jnp.exp(tot) + _tn_dot(v, k_out)


def _gla_scan(qkvr, g_lr, w2h, w2l, gb, *, reverse, blk):
    bsz, t, _ = qkvr.shape
    nh = GLA_HEADS
    blk = min(blk, t)
    nblk = t // blk
    pos = (lambda i: nblk - 1 - i) if reverse else (lambda i: i)
    kq = (nh * GLA_DK) // GLA_DK
    kv = (2 * nh * GLA_DK) // GLA_DV
    return pl.pallas_call(
        functools.partial(_gla_kernel, reverse=reverse, nchunk=blk // GLA_CHUNK),
        grid=(bsz, nh, nblk),
        in_specs=[
            pl.BlockSpec((None, blk, GLA_DK), lambda b, h, i: (b, pos(i), h)),
            pl.BlockSpec((None, blk, GLA_DK), lambda b, h, i: (b, pos(i), kq + h)),
            pl.BlockSpec((None, blk, GLA_DV), lambda b, h, i: (b, pos(i), kv + h)),
            pl.BlockSpec((None, blk, LANE), lambda b, h, i: (b, pos(i), 0)),
            pl.BlockSpec((LANE, GLA_DK), lambda b, h, i: (0, h)),
            pl.BlockSpec((LANE, GLA_DK), lambda b, h, i: (0, h)),
            pl.BlockSpec((1, GLA_DK), lambda b, h, i: (0, h)),
        ],
        out_specs=pl.BlockSpec((None, blk, GLA_DV), lambda b, h, i: (b, pos(i), h)),
        out_shape=jax.ShapeDtypeStruct((bsz, t, nh * GLA_DV), F32),
        scratch_shapes=[pltpu.VMEM((GLA_DV, GLA_DK), F32)],
        compiler_params=_cparams(("parallel", "parallel", "arbitrary")),
        name="gla_scan_bwd" if reverse else "gla_scan_fwd",
    )(qkvr, qkvr, qkvr, g_lr, w2h, w2l, gb)


def _gla_fin_kernel(of_ref, ob_ref, r_ref, g_ref, o_ref):
    o = _rms(of_ref[...] + ob_ref[...]) * g_ref[...]
    r = r_ref[...].astype(F32)
    o_ref[...] = (o * (r * jax.nn.sigmoid(r))).astype(o_ref.dtype)


def _gla_finish(o_f, o_b, qkvr, norm_g, *, tm):
    bsz, t, _ = o_f.shape
    nh = GLA_HEADS
    tm = min(tm, t)
    kr = (2 * nh * GLA_DK + nh * GLA_DV) // GLA_DV
    return pl.pallas_call(
        _gla_fin_kernel,
        grid=(bsz, t // tm, nh),
        in_specs=[
            pl.BlockSpec((None, tm, GLA_DV), lambda b, i, h: (b, i, h)),
            pl.BlockSpec((None, tm, GLA_DV), lambda b, i, h: (b, i, h)),
            pl.BlockSpec((None, tm, GLA_DV), lambda b, i, h: (b, i, kr + h)),
            pl.BlockSpec((1, GLA_DV), lambda b, i, h: (0, 0)),
        ],
        out_specs=pl.BlockSpec((None, tm, GLA_DV), lambda b, i, h: (b, i, h)),
        out_shape=jax.ShapeDtypeStruct((bsz, t, nh * GLA_DV), BF16),
        compiler_params=_cparams(("parallel", "parallel", "parallel")),
        name="gla_finish",
    )(o_f, o_b, qkvr, norm_g.reshape(1, GLA_DV))


def _na_kernel(q_ref, k0, k1, k2, k3, v0, v1, v2, v3, tbl_ref, o_ref, kcat, vcat, *, n_rows):
    i = pl.program_id(2)
    w = GRID_W
    half = 4 * w
    for j, (kr, vr) in enumerate(((k0, v0), (k1, v1), (k2, v2), (k3, v3))):
        kcat[j * half:(j + 1) * half, :] = kr[...]
        vcat[j * half:(j + 1) * half, :] = vr[...]
    lane_head = lax.broadcasted_iota(jnp.int32, (w, LANE), 1) // NA_HEAD_DIM
    base = i * NA_QROWS
    for rr in range(NA_QROWS):
        r = base + rr
        r0 = jnp.clip(r - NA_ROWS // 2, 0, n_rows - NA_ROWS)
        start = pl.multiple_of((4 + r0 - base) * w, w)
        delta = r - r0
        qr = q_ref[rr * w:(rr + 1) * w, :]
        qs = jnp.concatenate([jnp.where(lane_head == hh, qr, jnp.zeros_like(qr)) for hh in range(NA_GROUP)], axis=0)
        kb = kcat[pl.ds(start, NA_ROWS * w), :]
        vb = vcat[pl.ds(start, NA_ROWS * w), :]
        s = _nt_dot(qs, kb)
        bias = jnp.concatenate(
            [tbl_ref[2 * j - delta + NA_ROWS - 1].reshape(NA_GROUP * w, 2 * w) for j in range(NA_ROWS // 2)], axis=1)
        s = s + bias
        p = jnp.exp(s - jnp.max(s, axis=1, keepdims=True))
        l = jnp.sum(p, axis=1, keepdims=True)
        o4 = jnp.dot(p.astype(BF16), vb, preferred_element_type=F32) / l
        o = jnp.zeros((w, LANE), F32)
        for hh in range(NA_GROUP):
            o = jnp.where(lane_head == hh, o4[hh * w:(hh + 1) * w, :], o)
        o_ref[rr * w:(rr + 1) * w, :] = o.astype(o_ref.dtype)


def _na_bias_table(rpb):
    col = np.arange(GRID_W)
    col_start = np.clip(col - NA_COLS // 2, 0, GRID_W - NA_COLS)
    in_window = (col[None, :] >= col_start[:, None]) & (col[None, :] < col_start[:, None] + NA_COLS)
    col_off = np.clip(col[None, :] - col[:, None] + NA_COLS - 1, 0, 2 * NA_COLS - 2)
    cb = jnp.where(in_window[None, None], rpb.astype(F32)[:, :, col_off], NEG_INF)
    pair = jnp.concatenate([cb[:, :-1], cb[:, 1:]], axis=-1)
    n_pair = 2 * NA_ROWS - 2
    pair = pair.reshape(NA_HEADS // NA_GROUP, NA_GROUP, n_pair, GRID_W, 2 * GRID_W)
    return pair.transpose(0, 2, 1, 3, 4)


def _neighborhood_attention(qkv, rpb):
    bsz, t, _ = qkv.shape
    n_rows = t // GRID_W
    ng = NA_HEADS // NA_GROUP
    tq = NA_QROWS * GRID_W
    half = 4 * GRID_W
    nhalf = t // half
    tbl = _na_bias_table(rpb)

    def kv_spec(j, col0):
        return pl.BlockSpec((None, half, LANE),
                            lambda g, b, i: (b, jnp.clip(2 * i - 1 + j, 0, nhalf - 1), col0 + g))

    return pl.pallas_call(
        functools.partial(_na_kernel, n_rows=n_rows),
        grid=(ng, bsz, t // tq),
        in_specs=[pl.BlockSpec((None, tq, LANE), lambda g, b, i: (b, i, g))]
        + [kv_spec(j, ng) for j in range(4)] + [kv_spec(j, 2 * ng) for j in range(4)]
        + [pl.BlockSpec((None, 2 * NA_ROWS - 2, NA_GROUP, GRID_W, 2 * GRID_W), lambda g, b, i: (g, 0, 0, 0, 0))],
        out_specs=pl.BlockSpec((None, tq, LANE), lambda g, b, i: (b, i, g)),
        out_shape=jax.ShapeDtypeStruct((bsz, t, NA_HEADS * NA_HEAD_DIM), BF16),
        scratch_shapes=[pltpu.VMEM((4 * half, LANE), BF16), pltpu.VMEM((4 * half, LANE), BF16)],
        compiler_params=_cparams(("parallel", "parallel", "parallel")),
        name="neighborhood_attention",
    )(qkv, *([qkv] * 8), tbl)


def _post_kernel(o_ref, w_ref, x_ref, g1_ref, n2_ref, sc_ref, sh_ref, wrh_ref, wrl_ref, xn_ref, hf_ref, lg_ref):
    y = jnp.dot(o_ref[...], w_ref[...], preferred_element_type=F32)
    xn = x_ref[...] + g1_ref[...] * y
    xn_ref[...] = xn
    hf = _rms(xn) * n2_ref[...] * (1.0 + sc_ref[...]) + sh_ref[...]
    hf_ref[...] = hf
    hh, hl = _split2(hf)
    wrh = wrh_ref[...]
    lg_ref[...] = (jnp.dot(hh, wrh, preferred_element_type=F32) + jnp.dot(hl, wrh, preferred_element_type=F32)
                   + jnp.dot(hh, wrl_ref[...], preferred_element_type=F32))


def _post_mixer(o, w_out, x, g1, n2g, sc2, sh2, wr_hi, wr_lo, *, tm):
    bsz, t, d = x.shape
    kdim = o.shape[-1]
    tm = min(tm, t)
    vec = pl.BlockSpec((None, 1, d), lambda b, i: (b, 0, 0))
    tile = pl.BlockSpec((None, tm, d), lambda b, i: (b, i, 0))
    return pl.pallas_call(
        _post_kernel,
        grid=(bsz, t // tm),
        in_specs=[
            pl.BlockSpec((None, tm, kdim), lambda b, i: (b, i, 0)),
            pl.BlockSpec((kdim, d), lambda b, i: (0, 0)),
            tile, vec,
            pl.BlockSpec((1, d), lambda b, i: (0, 0)),
            vec, vec,
            pl.BlockSpec((d, LANE), lambda b, i: (0, 0)),
            pl.BlockSpec((d, LANE), lambda b, i: (0, 0)),
        ],
        out_specs=[tile, tile, pl.BlockSpec((None, tm, LANE), lambda b, i: (b, i, 0))],
        out_shape=[jax.ShapeDtypeStruct((bsz, t, d), F32), jax.ShapeDtypeStruct((bsz, t, d), F32),
                   jax.ShapeDtypeStruct((bsz, t, LANE), F32)],
        compiler_params=_cparams(("parallel", "parallel")),
        name="post_mixer",
    )(o, w_out, x, g1, n2g.reshape(1, d), sc2, sh2, wr_hi, wr_lo)


ROUTER_ROWS = 8 + MOE_EXPERTS


def _router_kernel(lt_ref, b_ref, id_ref, w_ref):
    lt = lt_ref[...] + b_ref[...]
    tn = lt.shape[1]
    lg = lt[0:MOE_GROUPS]
    e = jnp.exp(lg - jnp.max(lg, axis=0, keepdims=True))
    gp = e / jnp.sum(e, axis=0, keepdims=True)
    g_p = jnp.max(gp, axis=0, keepdims=True)
    rg = lax.broadcasted_iota(jnp.int32, (MOE_GROUPS, tn), 0)
    g_idx = jnp.min(jnp.where(gp == g_p, rg, MOE_GROUPS), axis=0, keepdims=True)
    el = jnp.zeros((MOE_EPG, tn), F32)
    for g in range(MOE_GROUPS):
        el = jnp.where(g_idx == g, lt[8 + g * MOE_EPG:8 + (g + 1) * MOE_EPG], el)
    ee = jnp.exp(el - jnp.max(el, axis=0, keepdims=True))
    ep = ee / jnp.sum(ee, axis=0, keepdims=True)
    re = lax.broadcasted_iota(jnp.int32, (MOE_EPG, tn), 0)
    p1 = jnp.max(ep, axis=0, keepdims=True)
    i1 = jnp.min(jnp.where(ep == p1, re, MOE_EPG), axis=0, keepdims=True)
    ep2 = jnp.where(re == i1, -1.0, ep)
    p2 = jnp.max(ep2, axis=0, keepdims=True)
    i2 = jnp.min(jnp.where(ep2 == p2, re, MOE_EPG), axis=0, keepdims=True)
    den = p1 + p2
    id_ref[0:1, :] = g_idx * MOE_EPG + i1
    id_ref[1:2, :] = g_idx * MOE_EPG + i2
    w_ref[0:1, :] = g_p * (p1 / den)
    w_ref[1:2, :] = g_p * (p2 / den)


def _router(logits_t, bias_col, *, tn):
    n = logits_t.shape[1]
    tn = min(tn, n)
    return pl.pallas_call(
        _router_kernel,
        grid=(n // tn,),
        in_specs=[pl.BlockSpec((ROUTER_ROWS, tn), lambda i: (0, i)),
                  pl.BlockSpec((ROUTER_ROWS, 1), lambda i: (0, 0))],
        out_specs=[pl.BlockSpec((MOE_TOP_K, tn), lambda i: (0, i)), pl.BlockSpec((MOE_TOP_K, tn), lambda i: (0, i))],
        out_shape=[jax.ShapeDtypeStruct((MOE_TOP_K, n), jnp.int32), jax.ShapeDtypeStruct((MOE_TOP_K, n), F32)],
        compiler_params=_cparams(("parallel",)),
        name="router_topk",
    )(logits_t, bias_col)


def _moe_kernel(be_ref, nu_ref, nv_ref, src_cur, src_nxt, dst_cur, hf_hbm, wgu_ref, wd_ref, y_hbm,
                xbuf, obuf, gsem, ssem):
    i = pl.program_id(0)
    nu = nu_ref[0]
    slot = i % 2
    blk = xbuf.shape[1]
    ff = wd_ref.shape[0]

    def gather(src_ref, s):
        def body(r, carry):
            tok = src_ref[0, r]
            pltpu.make_async_copy(hf_hbm.at[pl.ds(tok, 1)], xbuf.at[s, pl.ds(r, 1)], gsem.at[s]).start()
            return carry
        lax.fori_loop(0, blk, body, 0, unroll=8)

    def wait_gather(s):
        pltpu.make_async_copy(hf_hbm.at[pl.ds(0, blk)], xbuf.at[s], gsem.at[s]).wait()

    def wait_scatter(s, n_rows):
        def body(r, carry):
            pltpu.make_async_copy(obuf.at[s, pl.ds(0, 1)], y_hbm.at[pl.ds(0, 1)], ssem.at[s]).wait()
            return carry
        lax.fori_loop(0, n_rows, body, 0)

    @pl.when(jnp.logical_and(i == 0, nu > 0))
    def _():
        gather(src_cur, 0)

    @pl.when(i + 1 < nu)
    def _():
        gather(src_nxt, 1 - slot)

    @pl.when(i < nu)
    def _():
        wait_gather(slot)

        @pl.when(i >= 2)
        def _():
            wait_scatter(slot, nv_ref[jnp.maximum(i - 2, 0)])

        x = xbuf[slot].astype(BF16)
        hgu = jnp.dot(x, wgu_ref[...], preferred_element_type=F32)
        g = hgu[:, :ff]
        u = hgu[:, ff:]
        act = (g * jax.nn.sigmoid(g) * u).astype(BF16)
        obuf[slot] = jnp.dot(act, wd_ref[...], preferred_element_type=F32)

        def body(r, carry):
            row = dst_cur[0, r]
            pltpu.make_async_copy(obuf.at[slot, pl.ds(r, 1)], y_hbm.at[pl.ds(row, 1)], ssem.at[slot]).start()
            return carry
        lax.fori_loop(0, nv_ref[i], body, 0)

    @pl.when(i == pl.num_programs(0) - 1)
    def _():
        @pl.when(nu >= 1)
        def _():
            wait_scatter((nu - 1) % 2, nv_ref[jnp.maximum(nu - 1, 0)])

        @pl.when(nu >= 2)
        def _():
            wait_scatter(nu % 2, nv_ref[jnp.maximum(nu - 2, 0)])


def _moe_plan(expert_id, n_blocks, blk):
    flat_e = expert_id.reshape(-1)
    n_asg = flat_e.shape[0]
    onehot = (flat_e[:, None] == jnp.arange(MOE_EXPERTS, dtype=jnp.int32)[None, :]).astype(jnp.int32)
    csum = jnp.cumsum(onehot, axis=0)
    rank = jnp.sum(csum * onehot, axis=1) - 1
    counts = csum[-1]
    padded = (counts + blk - 1) // blk * blk
    pad_end = jnp.cumsum(padded)
    pad_start = pad_end - padded
    slot = pad_start[flat_e] + rank
    n_slots = n_blocks * blk
    asg = jnp.full((n_slots,), -1, jnp.int32).at[slot].set(jnp.arange(n_asg, dtype=jnp.int32))
    valid = asg >= 0
    src_tok = jnp.where(valid, asg // MOE_TOP_K, 0)
    dst_row = jnp.where(valid, asg, 0)
    n_valid = jnp.sum(valid.reshape(n_blocks, blk).astype(jnp.int32), axis=1)
    block_e = jnp.minimum(
        jnp.searchsorted(pad_end, jnp.arange(n_blocks, dtype=jnp.int32) * blk, side='right'), MOE_EXPERTS - 1)
    n_used = (pad_end[-1] // blk).astype(jnp.int32).reshape(1)
    return (block_e.astype(jnp.int32), n_used, n_valid, src_tok.reshape(n_blocks, 1, blk),
            dst_row.reshape(n_blocks, 1, blk))


def _moe_experts(hf, expert_id, w_gu, w_d, *, blk):
    n_tok, d = hf.shape
    n_asg = n_tok * MOE_TOP_K
    blk = min(blk, n_asg)
    n_blocks = -(-n_asg // blk) + MOE_EXPERTS
    block_e, n_used, n_valid, src_tok, dst_row = _moe_plan(expert_id, n_blocks, blk)
    ff = w_d.shape[1]
    smem = functools.partial(pl.BlockSpec, memory_space=pltpu.SMEM)
    grid_spec = pltpu.PrefetchScalarGridSpec(
        num_scalar_prefetch=3,
        grid=(n_blocks,),
        in_specs=[
            smem((None, 1, blk), lambda i, be, nu, nv: (i, 0, 0)),
            smem((None, 1, blk), lambda i, be, nu, nv: (jnp.minimum(i + 1, n_blocks - 1), 0, 0)),
            smem((None, 1, blk), lambda i, be, nu, nv: (i, 0, 0)),
            pl.BlockSpec(memory_space=pl.ANY),
            pl.BlockSpec((None, d, 2 * ff), lambda i, be, nu, nv: (be[i], 0, 0)),
            pl.BlockSpec((None, ff, d), lambda i, be, nu, nv: (be[i], 0, 0)),
        ],
        out_specs=pl.BlockSpec(memory_space=pl.ANY),
        scratch_shapes=[
            pltpu.VMEM((2, blk, d), F32),
            pltpu.VMEM((2, blk, d), F32),
            pltpu.SemaphoreType.DMA((2,)),
            pltpu.SemaphoreType.DMA((2,)),
        ],
    )
    return pl.pallas_call(
        _moe_kernel,
        grid_spec=grid_spec,
        out_shape=jax.ShapeDtypeStruct((n_asg, d), F32),
        compiler_params=_cparams(("arbitrary",)),
        name="moe_experts",
    )(block_e, n_used, n_valid, src_tok, src_tok, dst_row, hf, w_gu, w_d)


def _combine_kernel(x_ref, y_ref, w_ref, g2_ref, fg_ref, o_ref, *, final):
    d = x_ref.shape[-1]
    y = y_ref[...]
    w = w_ref[...]
    moe = w[:, 0:1] * y[:, :d] + w[:, 1:2] * y[:, d:]
    xn = x_ref[...] + g2_ref[...] * moe
    if final:
        xn = _rms(xn) * fg_ref[...]
    o_ref[...] = xn


def _combine(x, ybuf, wts, g2, final_g, *, final, tm):
    bsz, t, d = x.shape
    tm = min(tm, t)
    y2 = ybuf.reshape(-1, MOE_TOP_K * d)
    nt = t // tm
    return pl.pallas_call(
        functools.partial(_combine_kernel, final=final),
        grid=(bsz, nt),
        in_specs=[
            pl.BlockSpec((None, tm, d), lambda b, i: (b, i, 0)),
            pl.BlockSpec((tm, MOE_TOP_K * d), lambda b, i: (b * nt + i, 0)),
            pl.BlockSpec((tm, MOE_TOP_K), lambda b, i: (b * nt + i, 0)),
            pl.BlockSpec((None, 1, d), lambda b, i: (b, 0, 0)),
            pl.BlockSpec((1, d), lambda b, i: (0, 0)),
        ],
        out_specs=pl.BlockSpec((None, tm, d), lambda b, i: (b, i, 0)),
        out_shape=jax.ShapeDtypeStruct((bsz, t, d), F32),
        compiler_params=_cparams(("parallel", "parallel")),
        name="moe_combine_final" if final else "moe_combine",
    )(x, y2, wts, g2, final_g.reshape(1, d))


def _router_weights(rg_w, rg_b, re_w, re_b):
    d = rg_w.shape[0]
    wr = jnp.zeros((d, LANE), F32).at[:, 0:MOE_GROUPS].set(rg_w).at[:, 8:8 + MOE_EXPERTS].set(re_w)
    bias = jnp.zeros((ROUTER_ROWS, 1), F32).at[0:MOE_GROUPS, 0].set(rg_b).at[8:, 0].set(re_b)
    return tuple(_bf16_parts(wr, 2)) + (bias,)


def _col_scale(n, n_scaled, scale):
    return jnp.concatenate([jnp.full((n_scaled,), scale, F32), jnp.ones((n - n_scaled,), F32)])


def kernel(x, c, ada_w, ada_b, norm1_g, norm2_g, router_g_w, router_g_b, router_e_w, router_e_b, moe_w_gu, moe_w_d, da_w_in, da_w_out, da_lam_q1, da_lam_k1, da_lam_q2, da_lam_k2, da_subln_g, gla_w_in, gla_gate_w2_f, gla_gate_b_f, gla_gate_w2_b, gla_gate_b_b, gla_norm_g, gla_w_out, na_w_in, na_rpb, na_w_out, final_g):
    bsz, t, d = x.shape
    n_tok = bsz * t
    mod = _ada_mod(c, ada_w, ada_b)
    wts = ybuf = g2 = None
    for i in range(DEPTH):
        sh1, sc1, g1, sh2, sc2, g2_i = [mod[i, :, None, m * d:(m + 1) * d] for m in range(6)]
        if i > 0:
            x = _combine(x, ybuf, wts, g2, final_g, final=False, tm=512)
        kind, j = i % N_MIXERS, i // N_MIXERS
        if kind == 0:
            w_in = da_w_in[j].astype(BF16)
            cs = _col_scale(w_in.shape[1], DA_HEADS * DA_VAL_DIM, DA_HEAD_DIM ** -0.5 * LOG2E)
            qkv = _norm_proj(x, norm1_g[i], sc1, sh1, w_in, cs, tm=1024, tn=512, out_dtype=BF16)
            lam_vecs = jnp.stack([da_lam_q1[j], da_lam_k1[j], da_lam_q2[j], da_lam_k2[j]]).astype(F32)
            lam_init = 0.8 - 0.6 * math.exp(-0.3 * i)
            o = _diff_attention(qkv, lam_vecs, da_subln_g[j], lam_init, tile=1024)
            w_out = da_w_out[j]
        elif kind == 1:
            n_main = 2 * GLA_HEADS * GLA_DK + 2 * GLA_HEADS * GLA_DV
            w_main = gla_w_in[j][:, :n_main].astype(BF16)
            w_gate = jnp.zeros((d, LANE), F32).at[:, :2 * GLA_GATE_RANK].set(gla_w_in[j][:, n_main:]).astype(BF16)
            cs = _col_scale(n_main, GLA_HEADS * GLA_DK, GLA_DK ** -0.5)
            qkvr = _norm_proj(x, norm1_g[i], sc1, sh1, w_main, cs, tm=1024, tn=512, out_dtype=BF16)
            g_lr = _norm_proj(x, norm1_g[i], sc1, sh1, w_gate, jnp.ones((LANE,), F32), tm=1024, tn=LANE,
                              out_dtype=F32)
            outs = []
            for rev, w2, gb in ((False, gla_gate_w2_f[j], gla_gate_b_f[j]), (True, gla_gate_w2_b[j], gla_gate_b_b[j])):
                r0 = GLA_GATE_RANK if rev else 0
                w2p = jnp.zeros((LANE, w2.shape[1]), F32).at[r0:r0 + GLA_GATE_RANK].set(w2)
                w2h, w2l = _bf16_parts(w2p, 2)
                outs.append(_gla_scan(qkvr, g_lr, w2h, w2l, gb.reshape(1, -1).astype(F32), reverse=rev, blk=512))
            o = _gla_finish(outs[0], outs[1], qkvr, gla_norm_g[j], tm=512)
            w_out = gla_w_out[j]
        else:
            w_in = na_w_in[j].astype(BF16)
            cs = _col_scale(w_in.shape[1], NA_HEADS * NA_HEAD_DIM, NA_HEAD_DIM ** -0.5)
            qkv = _norm_proj(x, norm1_g[i], sc1, sh1, w_in, cs, tm=1024, tn=512, out_dtype=BF16)
            o = _neighborhood_attention(qkv, na_rpb[j])
            w_out = na_w_out[j]
        wr_hi, wr_lo, r_bias = _router_weights(router_g_w[i], router_g_b[i], router_e_w[i], router_e_b[i])
        x, hf, logits = _post_mixer(o, w_out.astype(BF16), x, g1, norm2_g[i], sc2, sh2, wr_hi, wr_lo, tm=256)
        logits_t = logits.reshape(n_tok, LANE).T[:ROUTER_ROWS]
        ids, wt = _router(logits_t, r_bias, tn=2048)
        ybuf = _moe_experts(hf.reshape(n_tok, d), ids.T, moe_w_gu[i].astype(BF16), moe_w_d[i].astype(BF16),
                            blk=MOE_BLK)
        wts = wt.T
        g2 = g2_i
    return _combine(x, ybuf, wts, g2, final_g, final=True, tm=512)
```

```python
import functools
import math

import numpy as np
import jax
import jax.numpy as jnp
from jax import lax
from jax.experimental import pallas as pl
from jax.experimental.pallas import tpu as pltpu

F32 = jnp.float32
BF16 = jnp.bfloat16

D_MODEL = 2048
DEPTH = 4
N_MIXERS = 3
NORM_EPS = 1e-6
NEG_INF = -1e30
LOG2E = 1.4426950408889634

DA_HEADS = 8
DA_HEAD_DIM = 128
DA_VAL_DIM = 2 * DA_HEAD_DIM
DA_ROW_BLK = 16

GLA_HEADS = 4
GLA_DK = 256
GLA_DV = 512
GLA_GATE_RANK = 16
GLA_TAU = 16.0
GLA_CHUNK = 64

GRID_W = 64
NA_HEADS = 64
NA_HEAD_DIM = 32
NA_ROWS = 8
NA_COLS = 16
NA_GROUP = 4
NA_QROWS = 8

MOE_GROUPS = 4
MOE_EPG = 8
MOE_EXPERTS = 32
MOE_TOP_K = 2
MOE_D_FF = 512
MOE_BLK = 256

LANE = 128
VMEM_LIMIT = 56 * 1024 * 1024


def _cparams(sem):
    return pltpu.CompilerParams(dimension_semantics=sem, vmem_limit_bytes=VMEM_LIMIT)


def _split2(v):
    hi = v.astype(BF16)
    lo = (v - hi.astype(F32)).astype(BF16)
    return hi, lo


def _split3(v):
    hi = v.astype(BF16)
    r = v - hi.astype(F32)
    mid = r.astype(BF16)
    lo = (r - mid.astype(F32)).astype(BF16)
    return hi, mid, lo


def _bf16_parts(v, n):
    parts = []
    r = v.astype(F32)
    for _ in range(n):
        top = lax.bitcast_convert_type(
            lax.bitcast_convert_type(r, jnp.uint32) & jnp.uint32(0xFFFF0000), F32)
        parts.append(top.astype(BF16))
        r = r - top
    return parts


def _nt_dot(a, b):
    return lax.dot_general(a, b, (((1,), (1,)), ((), ())), preferred_element_type=F32)


def _tn_dot(a, b):
    return lax.dot_general(a, b, (((0,), (0,)), ((), ())), preferred_element_type=F32)


def _rms(x):
    return x * lax.rsqrt(jnp.mean(x * x, axis=-1, keepdims=True) + NORM_EPS)


def _ada_kernel(c_ref, w_ref, b_ref, o_ref):
    c = c_ref[...]
    cond = (c * jax.nn.sigmoid(c)).astype(BF16)
    o_ref[...] = jnp.dot(cond, w_ref[...].astype(BF16), preferred_element_type=F32) + b_ref[...]


def _ada_mod(c, ada_w, ada_b):
    depth, d, n = ada_w.shape
    bsz = c.shape[0]
    rows = 8
    c8 = jnp.zeros((rows, d), F32).at[:bsz].set(c)
    tn = 1024
    out = pl.pallas_call(
        _ada_kernel,
        grid=(depth, n // tn),
        in_specs=[
            pl.BlockSpec((rows, d), lambda l, j: (0, 0)),
            pl.BlockSpec((None, d, tn), lambda l, j: (l, 0, j)),
            pl.BlockSpec((None, 1, tn), lambda l, j: (l, 0, j)),
        ],
        out_specs=pl.BlockSpec((None, rows, tn), lambda l, j: (l, 0, j)),
        out_shape=jax.ShapeDtypeStruct((depth, rows, n), F32),
        compiler_params=_cparams(("parallel", "parallel")),
        name="ada_mod",
    )(c8, ada_w, ada_b.reshape(depth, 1, n))
    return out[:, :bsz]


def _proj_kernel(x_ref, g_ref, sc_ref, sh_ref, w_ref, cs_ref, o_ref, h_scr):
    @pl.when(pl.program_id(2) == 0)
    def _():
        h = _rms(x_ref[...]) * g_ref[...] * (1.0 + sc_ref[...]) + sh_ref[...]
        h_scr[...] = h.astype(BF16)

    acc = jnp.dot(h_scr[...], w_ref[...], preferred_element_type=F32)
    o_ref[...] = (acc * cs_ref[...]).astype(o_ref.dtype)


def _norm_proj(x, g, sc, sh, w, col_scale, *, tm, tn, out_dtype):
    bsz, t, d = x.shape
    n = w.shape[1]
    tm = min(tm, t)
    return pl.pallas_call(
        _proj_kernel,
        grid=(bsz, t // tm, n // tn),
        in_specs=[
            pl.BlockSpec((None, tm, d), lambda b, i, j: (b, i, 0)),
            pl.BlockSpec((1, d), lambda b, i, j: (0, 0)),
            pl.BlockSpec((None, 1, d), lambda b, i, j: (b, 0, 0)),
            pl.BlockSpec((None, 1, d), lambda b, i, j: (b, 0, 0)),
            pl.BlockSpec((d, tn), lambda b, i, j: (0, j)),
            pl.BlockSpec((1, tn), lambda b, i, j: (0, j)),
        ],
        out_specs=pl.BlockSpec((None, tm, tn), lambda b, i, j: (b, i, j)),
        out_shape=jax.ShapeDtypeStruct((bsz, t, n), out_dtype),
        scratch_shapes=[pltpu.VMEM((tm, d), BF16)],
        compiler_params=_cparams(("parallel", "parallel", "arbitrary")),
        name="norm_proj",
    )(x, g.reshape(1, d), sc, sh, w, col_scale.reshape(1, n))


def _da_kernel(c_ref, q_ref, k_ref, v_ref, qf_ref, kf_ref, u_ref, lam_ref, g_ref, o_ref,
               qa_scr, m_scr, l_scr, acc_scr, s_scr, p_scr, *, lam_init):
    h = pl.program_id(1)
    qi = pl.program_id(2)
    kb = pl.program_id(3)
    hd = DA_HEAD_DIM
    n_split, hr = s_scr.shape[0] // 2, s_scr.shape[1]

    @pl.when(kb == 0)
    def _():
        m_scr[...] = jnp.full(m_scr.shape, -jnp.inf, F32)
        l_scr[...] = jnp.zeros(l_scr.shape, F32)
        acc_scr[...] = jnp.zeros(acc_scr.shape, F32)
        q = q_ref[...]
        qf = qf_ref[...]
        for m in range(2):
            qm = q[:, m * hd:(m + 1) * hd]
            qa_scr[m] = jnp.concatenate([qm, qf], axis=1)

    def step(diag):
        sgn = jnp.where(kb > qi, -1.0, 1.0).astype(BF16)
        k = k_ref[...]
        kf = kf_ref[...] * sgn
        v = v_ref[...]
        c2 = 2.0 * c_ref[h]
        for m in range(2):
            ka = jnp.concatenate([k[:, m * hd:(m + 1) * hd], kf], axis=1)
            for sp in range(n_split):
                s_scr[m * n_split + sp] = _nt_dot(qa_scr[m, sp * hr:(sp + 1) * hr, :], ka)
        nlt = s_scr.shape[2] // LANE
        blocks = [slice(rb * DA_ROW_BLK, (rb + 1) * DA_ROW_BLK) for rb in range(hr // DA_ROW_BLK)]
        for m in range(2):
            for sp in range(n_split):
                ci = m * n_split + sp
                rs = slice(sp * hr, (sp + 1) * hr)
                pmax = []
                for loc in blocks:
                    s = s_scr[ci, loc, :]
                    if diag:
                        s = s - c2 * u_ref[sp * hr + loc.start:sp * hr + loc.stop, :]
                        s_scr[ci, loc, :] = s
                    pm = s[:, 0:LANE]
                    for lt in range(1, nlt):
                        pm = jnp.maximum(pm, s[:, lt * LANE:(lt + 1) * LANE])
                    pmax.append(pm)
                m_old = m_scr[m, rs, :]
                m_new = jnp.maximum(m_old, jnp.max(jnp.concatenate(pmax, axis=0), axis=1, keepdims=True))
                alpha = jnp.exp2(m_old - m_new)
                m_scr[m, rs, :] = m_new
                psum = []
                for loc in blocks:
                    s = s_scr[ci, loc, :]
                    mb = m_new[loc]
                    acc_l = None
                    for lt in range(nlt):
                        p = jnp.exp2(s[:, lt * LANE:(lt + 1) * LANE] - mb)
                        acc_l = p if acc_l is None else acc_l + p
                        p_scr[ci, loc, lt * LANE:(lt + 1) * LANE] = p.astype(BF16)
                    psum.append(acc_l)
                row_sum = jnp.sum(jnp.concatenate(psum, axis=0), axis=1, keepdims=True)
                l_scr[m, rs, :] = alpha * l_scr[m, rs, :] + row_sum
                pv = jnp.dot(p_scr[ci], v, preferred_element_type=F32)
                acc_scr[m, rs, :] = jnp.concatenate([alpha, alpha], axis=1) * acc_scr[m, rs, :] + pv

    @pl.when(kb != qi)
    def _():
        step(False)

    @pl.when(kb == qi)
    def _():
        step(True)

    @pl.when(kb == pl.num_programs(3) - 1)
    def _():
        lv = lam_ref[...]
        lam = (jnp.exp(jnp.sum(lv[0:1] * lv[1:2], axis=1, keepdims=True))
               - jnp.exp(jnp.sum(lv[2:3] * lv[3:4], axis=1, keepdims=True)) + lam_init)
        inv = [1.0 / l_scr[m] for m in range(2)]
        o = (acc_scr[0] * jnp.concatenate([inv[0], inv[0]], axis=1)
             - lam * (acc_scr[1] * jnp.concatenate([inv[1], inv[1]], axis=1)))
        o = _rms(o) * g_ref[...] * (1.0 - lam_init)
        o_ref[...] = o.astype(o_ref.dtype)


def _alibi_features(t, n_heads):
    slopes = 2.0 ** (-8.0 * np.arange(1, n_heads + 1) / n_heads)
    c = jnp.asarray(slopes * LOG2E, F32)
    cp = c[:, None] * jnp.arange(t, dtype=F32)[None, :]
    p3 = jnp.stack(_bf16_parts(cp, 3), axis=-1)
    ones = jnp.ones((n_heads, t, 3), BF16)
    pad = jnp.zeros((n_heads, t, LANE - 6), BF16)
    qf = jnp.concatenate([-p3, ones, pad], axis=-1)
    kf = jnp.concatenate([ones, p3, pad], axis=-1)
    return c, qf, kf


def _diff_attention(qkv, lam_vecs, subln_g, lam_init, *, tile):
    bsz, t, _ = qkv.shape
    nh = DA_HEADS
    tq = tk = min(tile, t)
    n_split = 2 if tq >= 4 * DA_ROW_BLK else 1
    c, qf, kf = _alibi_features(t, nh)
    ii = jnp.arange(tq, dtype=F32)
    u = jnp.maximum(ii[None, :] - ii[:, None], 0.0)
    grid_spec = pltpu.PrefetchScalarGridSpec(
        num_scalar_prefetch=1,
        grid=(bsz, nh, t // tq, t // tk),
        in_specs=[
            pl.BlockSpec((None, tq, DA_VAL_DIM), lambda b, h, i, j, c: (b, i, h)),
            pl.BlockSpec((None, tk, DA_VAL_DIM), lambda b, h, i, j, c: (b, j, nh + h)),
            pl.BlockSpec((None, tk, DA_VAL_DIM), lambda b, h, i, j, c: (b, j, 2 * nh + h)),
            pl.BlockSpec((None, tq, LANE), lambda b, h, i, j, c: (h, i, 0)),
            pl.BlockSpec((None, tk, LANE), lambda b, h, i, j, c: (h, j, 0)),
            pl.BlockSpec((tq, tk), lambda b, h, i, j, c: (0, 0)),
            pl.BlockSpec((4, DA_HEAD_DIM), lambda b, h, i, j, c: (0, 0)),
            pl.BlockSpec((1, DA_VAL_DIM), lambda b, h, i, j, c: (0, 0)),
        ],
        out_specs=pl.BlockSpec((None, tq, DA_VAL_DIM), lambda b, h, i, j, c: (b, i, h)),
        scratch_shapes=[
            pltpu.VMEM((2, tq, 2 * DA_HEAD_DIM), BF16),
            pltpu.VMEM((2, tq, LANE), F32),
            pltpu.VMEM((2, tq, LANE), F32),
            pltpu.VMEM((2, tq, DA_VAL_DIM), F32),
            pltpu.VMEM((2 * n_split, tq // n_split, tk), F32),
            pltpu.VMEM((2 * n_split, tq // n_split, tk), BF16),
        ],
    )
    return pl.pallas_call(
        functools.partial(_da_kernel, lam_init=lam_init),
        grid_spec=grid_spec,
        out_shape=jax.ShapeDtypeStruct((bsz, t, nh * DA_VAL_DIM), BF16),
        compiler_params=_cparams(("parallel", "parallel", "parallel", "arbitrary")),
        name="diff_attention",
    )(c, qkv, qkv, qkv, qf, kf, u, lam_vecs, subln_g.reshape(1, DA_VAL_DIM))


def _gla_kernel(q_ref, k_ref, v_ref, gl_ref, w2h_ref, w2l_ref, gb_ref, o_ref, st_scr, *, reverse, nchunk):
    cs = GLA_CHUNK

    @pl.when(pl.program_id(2) == 0)
    def _():
        st_scr[...] = jnp.zeros(st_scr.shape, F32)

    ri = lax.broadcasted_iota(jnp.int32, (cs, cs), 0)
    ci = lax.broadcasted_iota(jnp.int32, (cs, cs), 1)
    keep = (ci >= ri) if reverse else (ci <= ri)
    tri = jnp.where(keep, 1.0, 0.0).astype(BF16)
    w2h = w2h_ref[...]
    w2l = w2l_ref[...]
    order = range(nchunk - 1, -1, -1) if reverse else range(nchunk)
    for c in order:
        rows = slice(c * cs, (c + 1) * cs)
        gh, glo = _split2(gl_ref[rows, :])
        x = (jnp.dot(gh, w2h, preferred_element_type=F32) + jnp.dot(glo, w2h, preferred_element_type=F32)
             + jnp.dot(gh, w2l, preferred_element_type=F32) + gb_ref[...])
        la = (jnp.minimum(x, 0.0) - jnp.log1p(jnp.exp(-jnp.abs(x)))) * (1.0 / GLA_TAU)
        a1, a2, a3 = _split3(la)
        b = (jnp.dot(tri, a1, preferred_element_type=F32) + jnp.dot(tri, a2, preferred_element_type=F32)
             + jnp.dot(tri, a3, preferred_element_type=F32))
        tot = b[0:1] if reverse else b[cs - 1:cs]
        q = q_ref[rows, :].astype(F32)
        k = k_ref[rows, :].astype(F32)
        v = v_ref[rows, :]
        q_in = (q * jnp.exp(b)).astype(BF16)
        k_in = (k * jnp.exp(-b)).astype(BF16)
        k_out = (k * jnp.exp(tot - b)).astype(BF16)
        att = jnp.where(keep, _nt_dot(q_in, k_in), 0.0)
        st = st_scr[...]
        o = jnp.dot(att.astype(BF16), v, preferred_element_type=F32) + _nt_dot(q_in, st.astype(BF16))
        o_ref[rows, :] = o
        st_scr[...] = st * jnp.exp(tot) + _tn_dot(v, k_out)


def _gla_scan(qkvr, g_lr, w2h, w2l, gb, *, reverse, blk):
    bsz, t, _ = qkvr.shape
    nh = GLA_HEADS
    blk = min(blk, t)
    nblk = t // blk
    pos = (lambda i: nblk - 1 - i) if reverse else (lambda i: i)
    kq = (nh * GLA_DK) // GLA_DK
    kv = (2 * nh * GLA_DK) // GLA_DV
    return pl.pallas_call(
        functools.partial(_gla_kernel, reverse=reverse, nchunk=blk // GLA_CHUNK),
        grid=(bsz, nh, nblk),
        in_specs=[
            pl.BlockSpec((None, blk, GLA_DK), lambda b, h, i: (b, pos(i), h)),
            pl.BlockSpec((None, blk, GLA_DK), lambda b, h, i: (b, pos(i), kq + h)),
            pl.BlockSpec((None, blk, GLA_DV), lambda b, h, i: (b, pos(i), kv + h)),
            pl.BlockSpec((None, blk, LANE), lambda b, h, i: (b, pos(i), 0)),
            pl.BlockSpec((LANE, GLA_DK), lambda b, h, i: (0, h)),
            pl.BlockSpec((LANE, GLA_DK), lambda b, h, i: (0, h)),
            pl.BlockSpec((1, GLA_DK), lambda b, h, i: (0, h)),
        ],
        out_specs=pl.BlockSpec((None, blk, GLA_DV), lambda b, h, i: (b, pos(i), h)),
        out_shape=jax.ShapeDtypeStruct((bsz, t, nh * GLA_DV), F32),
        scratch_shapes=[pltpu.VMEM((GLA_DV, GLA_DK), F32)],
        compiler_params=_cparams(("parallel", "parallel", "arbitrary")),
        name="gla_scan_bwd" if reverse else "gla_scan_fwd",
    )(qkvr, qkvr, qkvr, g_lr, w2h, w2l, gb)


def _gla_fin_kernel(of_ref, ob_ref, r_ref, g_ref, o_ref):
    o = _rms(of_ref[...] + ob_ref[...]) * g_ref[...]
    r = r_ref[...].astype(F32)
    o_ref[...] = (o * (r * jax.nn.sigmoid(r))).astype(o_ref.dtype)


def _gla_finish(o_f, o_b, qkvr, norm_g, *, tm):
    bsz, t, _ = o_f.shape
    nh = GLA_HEADS
    tm = min(tm, t)
    kr = (2 * nh * GLA_DK + nh * GLA_DV) // GLA_DV
    return pl.pallas_call(
        _gla_fin_kernel,
        grid=(bsz, t // tm, nh),
        in_specs=[
            pl.BlockSpec((None, tm, GLA_DV), lambda b, i, h: (b, i, h)),
            pl.BlockSpec((None, tm, GLA_DV), lambda b, i, h: (b, i, h)),
            pl.BlockSpec((None, tm, GLA_DV), lambda b, i, h: (b, i, kr + h)),
            pl.BlockSpec((1, GLA_DV), lambda b, i, h: (0, 0)),
        ],
        out_specs=pl.BlockSpec((None, tm, GLA_DV), lambda b, i, h: (b, i, h)),
        out_shape=jax.ShapeDtypeStruct((bsz, t, nh * GLA_DV), BF16),
        compiler_params=_cparams(("parallel", "parallel", "parallel")),
        name="gla_finish",
    )(o_f, o_b, qkvr, norm_g.reshape(1, GLA_DV))


def _na_kernel(q_ref, k0, k1, k2, k3, v0, v1, v2, v3, tbl_ref, o_ref, kcat, vcat, *, n_rows):
    i = pl.program_id(2)
    w = GRID_W
    half = 4 * w
    for j, (kr, vr) in enumerate(((k0, v0), (k1, v1), (k2, v2), (k3, v3))):
        kcat[j * half:(j + 1) * half, :] = kr[...]
        vcat[j * half:(j + 1) * half, :] = vr[...]
    lane_head = lax.broadcasted_iota(jnp.int32, (w, LANE), 1) // NA_HEAD_DIM
    base = i * NA_QROWS
    for rr in range(NA_QROWS):
        r = base + rr
        r0 = jnp.clip(r - NA_ROWS // 2, 0, n_rows - NA_ROWS)
        start = pl.multiple_of((4 + r0 - base) * w, w)
        delta = r - r0
        qr = q_ref[rr * w:(rr + 1) * w, :]
        qs = jnp.concatenate([jnp.where(lane_head == hh, qr, jnp.zeros_like(qr)) for hh in range(NA_GROUP)], axis=0)
        kb = kcat[pl.ds(start, NA_ROWS * w), :]
        vb = vcat[pl.ds(start, NA_ROWS * w), :]
        s = _nt_dot(qs, kb)
        bias = jnp.concatenate(
            [tbl_ref[2 * j - delta + NA_ROWS - 1].reshape(NA_GROUP * w, 2 * w) for j in range(NA_ROWS // 2)], axis=1)
        s = s + bias
        p = jnp.exp(s - jnp.max(s, axis=1, keepdims=True))
        l = jnp.sum(p, axis=1, keepdims=True)
        o4 = jnp.dot(p.astype(BF16), vb, preferred_element_type=F32) / l
        o = jnp.zeros((w, LANE), F32)
        for hh in range(NA_GROUP):
            o = jnp.where(lane_head == hh, o4[hh * w:(hh + 1) * w, :], o)
        o_ref[rr * w:(rr + 1) * w, :] = o.astype(o_ref.dtype)


def _na_bias_table(rpb):
    col = np.arange(GRID_W)
    col_start = np.clip(col - NA_COLS // 2, 0, GRID_W - NA_COLS)
    in_window = (col[None, :] >= col_start[:, None]) & (col[None, :] < col_start[:, None] + NA_COLS)
    col_off = np.clip(col[None, :] - col[:, None] + NA_COLS - 1, 0, 2 * NA_COLS - 2)
    cb = jnp.where(in_window[None, None], rpb.astype(F32)[:, :, col_off], NEG_INF)
    pair = jnp.concatenate([cb[:, :-1], cb[:, 1:]], axis=-1)
    n_pair = 2 * NA_ROWS - 2
    pair = pair.reshape(NA_HEADS // NA_GROUP, NA_GROUP, n_pair, GRID_W, 2 * GRID_W)
    return pair.transpose(0, 2, 1, 3, 4)


def _neighborhood_attention(qkv, rpb):
    bsz, t, _ = qkv.shape
    n_rows = t // GRID_W
    ng = NA_HEADS // NA_GROUP
    tq = NA_QROWS * GRID_W
    half = 4 * GRID_W
    nhalf = t // half
    tbl = _na_bias_table(rpb)

    def kv_spec(j, col0):
        return pl.BlockSpec((None, half, LANE),
                            lambda g, b, i: (b, jnp.clip(2 * i - 1 + j, 0, nhalf - 1), col0 + g))

    return pl.pallas_call(
        functools.partial(_na_kernel, n_rows=n_rows),
        grid=(ng, bsz, t // tq),
        in_specs=[pl.BlockSpec((None, tq, LANE), lambda g, b, i: (b, i, g))]
        + [kv_spec(j, ng) for j in range(4)] + [kv_spec(j, 2 * ng) for j in range(4)]
        + [pl.BlockSpec((None, 2 * NA_ROWS - 2, NA_GROUP, GRID_W, 2 * GRID_W), lambda g, b, i: (g, 0, 0, 0, 0))],
        out_specs=pl.BlockSpec((None, tq, LANE), lambda g, b, i: (b, i, g)),
        out_shape=jax.ShapeDtypeStruct((bsz, t, NA_HEADS * NA_HEAD_DIM), BF16),
        scratch_shapes=[pltpu.VMEM((4 * half, LANE), BF16), pltpu.VMEM((4 * half, LANE), BF16)],
        compiler_params=_cparams(("parallel", "parallel", "parallel")),
        name="neighborhood_attention",
    )(qkv, *([qkv] * 8), tbl)


def _post_kernel(o_ref, w_ref, x_ref, g1_ref, n2_ref, sc_ref, sh_ref, wrh_ref, wrl_ref, xn_ref, hf_ref, lg_ref):
    y = jnp.dot(o_ref[...], w_ref[...], preferred_element_type=F32)
    xn = x_ref[...] + g1_ref[...] * y
    xn_ref[...] = xn
    hf = _rms(xn) * n2_ref[...] * (1.0 + sc_ref[...]) + sh_ref[...]
    hf_ref[...] = hf
    hh, hl = _split2(hf)
    wrh = wrh_ref[...]
    lg_ref[...] = (jnp.dot(hh, wrh, preferred_element_type=F32) + jnp.dot(hl, wrh, preferred_element_type=F32)
                   + jnp.dot(hh, wrl_ref[...], preferred_element_type=F32))


def _post_mixer(o, w_out, x, g1, n2g, sc2, sh2, wr_hi, wr_lo, *, tm):
    bsz, t, d = x.shape
    kdim = o.shape[-1]
    tm = min(tm, t)
    vec = pl.BlockSpec((None, 1, d), lambda b, i: (b, 0, 0))
    tile = pl.BlockSpec((None, tm, d), lambda b, i: (b, i, 0))
    return pl.pallas_call(
        _post_kernel,
        grid=(bsz, t // tm),
        in_specs=[
            pl.BlockSpec((None, tm, kdim), lambda b, i: (b, i, 0)),
            pl.BlockSpec((kdim, d), lambda b, i: (0, 0)),
            tile, vec,
            pl.BlockSpec((1, d), lambda b, i: (0, 0)),
            vec, vec,
            pl.BlockSpec((d, LANE), lambda b, i: (0, 0)),
            pl.BlockSpec((d, LANE), lambda b, i: (0, 0)),
        ],
        out_specs=[tile, tile, pl.BlockSpec((None, tm, LANE), lambda b, i: (b, i, 0))],
        out_shape=[jax.ShapeDtypeStruct((bsz, t, d), F32), jax.ShapeDtypeStruct((bsz, t, d), F32),
                   jax.ShapeDtypeStruct((bsz, t, LANE), F32)],
        compiler_params=_cparams(("parallel", "parallel")),
        name="post_mixer",
    )(o, w_out, x, g1, n2g.reshape(1, d), sc2, sh2, wr_hi, wr_lo)


ROUTER_ROWS = 8 + MOE_EXPERTS


def _router_kernel(lt_ref, b_ref, id_ref, w_ref):
    lt = lt_ref[...] + b_ref[...]
    tn = lt.shape[1]
    lg = lt[0:MOE_GROUPS]
    e = jnp.exp(lg - jnp.max(lg, axis=0, keepdims=True))
    gp = e / jnp.sum(e, axis=0, keepdims=True)
    g_p = jnp.max(gp, axis=0, keepdims=True)
    rg = lax.broadcasted_iota(jnp.int32, (MOE_GROUPS, tn), 0)
    g_idx = jnp.min(jnp.where(gp == g_p, rg, MOE_GROUPS), axis=0, keepdims=True)
    el = jnp.zeros((MOE_EPG, tn), F32)
    for g in range(MOE_GROUPS):
        el = jnp.where(g_idx == g, lt[8 + g * MOE_EPG:8 + (g + 1) * MOE_EPG], el)
    ee = jnp.exp(el - jnp.max(el, axis=0, keepdims=True))
    ep = ee / jnp.sum(ee, axis=0, keepdims=True)
    re = lax.broadcasted_iota(jnp.int32, (MOE_EPG, tn), 0)
    p1 = jnp.max(ep, axis=0, keepdims=True)
    i1 = jnp.min(jnp.where(ep == p1, re, MOE_EPG), axis=0, keepdims=True)
    ep2 = jnp.where(re == i1, -1.0, ep)
    p2 = jnp.max(ep2, axis=0, keepdims=True)
    i2 = jnp.min(jnp.where(ep2 == p2, re, MOE_EPG), axis=0, keepdims=True)
    den = p1 + p2
    id_ref[0:1, :] = g_idx * MOE_EPG + i1
    id_ref[1:2, :] = g_idx * MOE_EPG + i2
    w_ref[0:1, :] = g_p * (p1 / den)
    w_ref[1:2, :] = g_p * (p2 / den)


def _router(logits_t, bias_col, *, tn):
    n = logits_t.shape[1]
    tn = min(tn, n)
    return pl.pallas_call(
        _router_kernel,
        grid=(n // tn,),
        in_specs=[pl.BlockSpec((ROUTER_ROWS, tn), lambda i: (0, i)),
                  pl.BlockSpec((ROUTER_ROWS, 1), lambda i: (0, 0))],
        out_specs=[pl.BlockSpec((MOE_TOP_K, tn), lambda i: (0, i)), pl.BlockSpec((MOE_TOP_K, tn), lambda i: (0, i))],
        out_shape=[jax.ShapeDtypeStruct((MOE_TOP_K, n), jnp.int32), jax.ShapeDtypeStruct((MOE_TOP_K, n), F32)],
        compiler_params=_cparams(("parallel",)),
        name="router_topk",
    )(logits_t, bias_col)


MOE_DMA_UNROLL = 8


def _moe_kernel(pb_ref, pe_ref, r0_ref, r1_ref, fl_ref, np_ref, src_cur, src_nxt, dst_cur, hf_hbm, wgu_ref, wd_ref,
                y_hbm, wgu_bf, wd_bf, xbuf, obuf, gsem, ssem, *, n_blocks):
    s = pl.program_id(0)
    blk = xbuf.shape[1]
    ff = wd_ref.shape[0]
    b = pb_ref[s]
    slot = b % 2
    flags = fl_ref[s]
    active = s < np_ref[0]
    first = jnp.logical_and(active, (flags & 1) != 0)
    last = jnp.logical_and(active, (flags & 2) != 0)
    new_w = jnp.logical_and(active, (flags & 4) != 0)

    def row_dmas(make):
        def body(g, carry):
            for j in range(MOE_DMA_UNROLL):
                make(g * MOE_DMA_UNROLL + j).start(priority=j % 2)
            return carry
        lax.fori_loop(0, blk // MOE_DMA_UNROLL, body, 0)

    def gather(src_ref, sl):
        row_dmas(lambda r: pltpu.make_async_copy(
            hf_hbm.at[pl.ds(src_ref[0, r], 1)], xbuf.at[sl, pl.ds(r, 1)], gsem.at[sl]))

    def wait_gather(sl):
        pltpu.make_async_copy(hf_hbm.at[pl.ds(0, blk)], xbuf.at[sl], gsem.at[sl]).wait()

    def wait_scatter(sl):
        pltpu.make_async_copy(obuf.at[sl], y_hbm.at[pl.ds(0, blk)], ssem.at[sl]).wait()

    @pl.when(s == 0)
    def _():
        obuf[...] = jnp.zeros(obuf.shape, F32)
        gather(src_cur, 0)

    @pl.when(first)
    def _():
        @pl.when(b + 1 < n_blocks)
        def _():
            gather(src_nxt, 1 - slot)

        wait_gather(slot)

        @pl.when(b >= 2)
        def _():
            wait_scatter(slot)

    @pl.when(new_w)
    def _():
        wgu_bf[...] = wgu_ref[...].astype(BF16)
        wd_bf[...] = wd_ref[...].astype(BF16)

    @pl.when(active)
    def _():
        x = xbuf[slot].astype(BF16)
        hgu = jnp.dot(x, wgu_bf[...], preferred_element_type=F32)
        g = hgu[:, :ff]
        u = hgu[:, ff:]
        act = (g * jax.nn.sigmoid(g) * u).astype(BF16)
        y = jnp.dot(act, wd_bf[...], preferred_element_type=F32)
        ri = lax.broadcasted_iota(jnp.int32, (blk, 1), 0)
        mine = jnp.logical_and(ri >= r0_ref[s], ri < r1_ref[s])
        obuf[slot] = jnp.where(mine, y, obuf[slot])

    @pl.when(last)
    def _():
        row_dmas(lambda r: pltpu.make_async_copy(
            obuf.at[slot, pl.ds(r, 1)], y_hbm.at[pl.ds(dst_cur[0, r], 1)], ssem.at[slot]))

    @pl.when(s == pl.num_programs(0) - 1)
    def _():
        wait_scatter(0)
        wait_scatter(1)


def _moe_plan(expert_id, blk):
    flat_e = expert_id.reshape(-1)
    n_asg = flat_e.shape[0]
    n_blocks = n_asg // blk
    n_steps = n_blocks + MOE_EXPERTS - 1
    onehot = (flat_e[:, None] == jnp.arange(MOE_EXPERTS, dtype=jnp.int32)[None, :]).astype(jnp.int32)
    csum = jnp.cumsum(onehot, axis=0)
    rank = jnp.sum(csum * onehot, axis=1) - 1
    counts = csum[-1]
    ends = jnp.cumsum(counts)
    starts = ends - counts
    slot = starts[flat_e] + rank
    asg = jnp.zeros((n_asg,), jnp.int32).at[slot].set(jnp.arange(n_asg, dtype=jnp.int32))
    src_tok = asg // MOE_TOP_K
    dst_row = (asg % MOE_TOP_K) * (n_asg // MOE_TOP_K) + asg // MOE_TOP_K
    bidx = jnp.arange(n_blocks, dtype=jnp.int32)
    e_lo = jnp.searchsorted(ends, bidx * blk, side='right').astype(jnp.int32)
    e_hi = jnp.searchsorted(ends, (bidx + 1) * blk - 1, side='right').astype(jnp.int32)
    n_pair_b = e_hi - e_lo + 1
    pair_end = jnp.cumsum(n_pair_b)
    pair_start = pair_end - n_pair_b
    n_pairs = pair_end[-1]
    sidx = jnp.arange(n_steps, dtype=jnp.int32)
    pb = jnp.minimum(jnp.searchsorted(pair_end, sidx, side='right').astype(jnp.int32), n_blocks - 1)
    pe = jnp.clip(e_lo[pb] + sidx - pair_start[pb], 0, MOE_EXPERTS - 1)
    pe = jnp.where(sidx < n_pairs, pe, pe[jnp.maximum(n_pairs - 1, 0)])
    r0 = jnp.clip(starts[pe] - pb * blk, 0, blk)
    r1 = jnp.clip(ends[pe] - pb * blk, 0, blk)
    prev_e = jnp.concatenate([jnp.full((1,), -1, jnp.int32), pe[:-1]])
    flags = ((sidx == pair_start[pb]).astype(jnp.int32) + 2 * (sidx == pair_end[pb] - 1).astype(jnp.int32)
             + 4 * (pe != prev_e).astype(jnp.int32))
    flags = jnp.where(sidx < n_pairs, flags, 0)
    i32 = lambda a: a.astype(jnp.int32)
    return (i32(pb), i32(pe), i32(r0), i32(r1), i32(flags), i32(n_pairs).reshape(1),
            src_tok.reshape(n_blocks, 1, blk), dst_row.reshape(n_blocks, 1, blk))


def _moe_experts(hf, expert_id, w_gu, w_d, *, blk):
    n_tok, d = hf.shape
    n_asg = n_tok * MOE_TOP_K
    blk = min(blk, n_asg // 2)
    n_blocks = n_asg // blk
    pb, pe, r0, r1, flags, n_pairs, src_tok, dst_row = _moe_plan(expert_id, blk)
    ff = w_d.shape[1]
    smem = functools.partial(pl.BlockSpec, memory_space=pltpu.SMEM)
    grid_spec = pltpu.PrefetchScalarGridSpec(
        num_scalar_prefetch=6,
        grid=(pb.shape[0],),
        in_specs=[
            smem((None, 1, blk), lambda s, pb, *_: (pb[s], 0, 0)),
            smem((None, 1, blk), lambda s, pb, *_: (jnp.minimum(pb[s] + 1, n_blocks - 1), 0, 0)),
            smem((None, 1, blk), lambda s, pb, *_: (pb[s], 0, 0)),
            pl.BlockSpec(memory_space=pl.ANY),
            pl.BlockSpec((None, d, 2 * ff), lambda s, pb, pe, *_: (pe[s], 0, 0)),
            pl.BlockSpec((None, ff, d), lambda s, pb, pe, *_: (pe[s], 0, 0)),
        ],
        out_specs=pl.BlockSpec(memory_space=pl.ANY),
        scratch_shapes=[
            pltpu.VMEM((d, 2 * ff), BF16),
            pltpu.VMEM((ff, d), BF16),
            pltpu.VMEM((2, blk, d), F32),
            pltpu.VMEM((2, blk, d), F32),
            pltpu.SemaphoreType.DMA((2,)),
            pltpu.SemaphoreType.DMA((2,)),
        ],
    )
    return pl.pallas_call(
        functools.partial(_moe_kernel, n_blocks=n_blocks),
        grid_spec=grid_spec,
        out_shape=jax.ShapeDtypeStruct((n_asg, d), F32),
        compiler_params=_cparams(("arbitrary",)),
        name="moe_experts",
    )(pb, pe, r0, r1, flags, n_pairs, src_tok, src_tok, dst_row, hf, w_gu, w_d)


def _combine_kernel(x_ref, y0_ref, y1_ref, w_ref, g2_ref, fg_ref, o_ref, *, final):
    w = w_ref[...]
    moe = w[:, 0:1] * y0_ref[...] + w[:, 1:2] * y1_ref[...]
    xn = x_ref[...] + g2_ref[...] * moe
    if final:
        xn = _rms(xn) * fg_ref[...]
    o_ref[...] = xn


def _combine(x, ybuf, wts, g2, final_g, *, final, tm):
    bsz, t, d = x.shape
    tm = min(tm, t)
    y2 = ybuf.reshape(MOE_TOP_K, bsz * t, d)
    nt = t // tm
    return pl.pallas_call(
        functools.partial(_combine_kernel, final=final),
        grid=(bsz, nt),
        in_specs=[
            pl.BlockSpec((None, tm, d), lambda b, i: (b, i, 0)),
            pl.BlockSpec((None, tm, d), lambda b, i: (0, b * nt + i, 0)),
            pl.BlockSpec((None, tm, d), lambda b, i: (1, b * nt + i, 0)),
            pl.BlockSpec((tm, MOE_TOP_K), lambda b, i: (b * nt + i, 0)),
            pl.BlockSpec((None, 1, d), lambda b, i: (b, 0, 0)),
            pl.BlockSpec((1, d), lambda b, i: (0, 0)),
        ],
        out_specs=pl.BlockSpec((None, tm, d), lambda b, i: (b, i, 0)),
        out_shape=jax.ShapeDtypeStruct((bsz, t, d), F32),
        compiler_params=_cparams(("parallel", "parallel")),
        name="moe_combine_final" if final else "moe_combine",
    )(x, y2, y2, wts, g2, final_g.reshape(1, d))


def _router_weights(rg_w, rg_b, re_w, re_b):
    d = rg_w.shape[0]
    wr = jnp.zeros((d, LANE), F32).at[:, 0:MOE_GROUPS].set(rg_w).at[:, 8:8 + MOE_EXPERTS].set(re_w)
    bias = jnp.zeros((ROUTER_ROWS, 1), F32).at[0:MOE_GROUPS, 0].set(rg_b).at[8:, 0].set(re_b)
    return tuple(_bf16_parts(wr, 2)) + (bias,)


def _col_scale(n, n_scaled, scale):
    return jnp.concatenate([jnp.full((n_scaled,), scale, F32), jnp.ones((n - n_scaled,), F32)])


def kernel(x, c, ada_w, ada_b, norm1_g, norm2_g, router_g_w, router_g_b, router_e_w, router_e_b, moe_w_gu, moe_w_d, da_w_in, da_w_out, da_lam_q1, da_lam_k1, da_lam_q2, da_lam_k2, da_subln_g, gla_w_in, gla_gate_w2_f, gla_gate_b_f, gla_gate_w2_b, gla_gate_b_b, gla_norm_g, gla_w_out, na_w_in, na_rpb, na_w_out, final_g):
    bsz, t, d = x.shape
    n_tok = bsz * t
    mod = _ada_mod(c, ada_w, ada_b)
    wts = ybuf = g2 = None
    for i in range(DEPTH):
        sh1, sc1, g1, sh2, sc2, g2_i = [mod[i, :, None, m * d:(m + 1) * d] for m in range(6)]
        if i > 0:
            x = _combine(x, ybuf, wts, g2, final_g, final=False, tm=512)
        kind, j = i % N_MIXERS, i // N_MIXERS
        if kind == 0:
            w_in = da_w_in[j].astype(BF16)
            cs = _col_scale(w_in.shape[1], DA_HEADS * DA_VAL_DIM, DA_HEAD_DIM ** -0.5 * LOG2E)
            qkv = _norm_proj(x, norm1_g[i], sc1, sh1, w_in, cs, tm=1024, tn=512, out_dtype=BF16)
            lam_vecs = jnp.stack([da_lam_q1[j], da_lam_k1[j], da_lam_q2[j], da_lam_k2[j]]).astype(F32)
            lam_init = 0.8 - 0.6 * math.exp(-0.3 * i)
            o = _diff_attention(qkv, lam_vecs, da_subln_g[j], lam_init, tile=1024)
            w_out = da_w_out[j]
        elif kind == 1:
            n_main = 2 * GLA_HEADS * GLA_DK + 2 * GLA_HEADS * GLA_DV
            w_main = gla_w_in[j][:, :n_main].astype(BF16)
            w_gate = jnp.zeros((d, LANE), F32).at[:, :2 * GLA_GATE_RANK].set(gla_w_in[j][:, n_main:]).astype(BF16)
            cs = _col_scale(n_main, GLA_HEADS * GLA_DK, GLA_DK ** -0.5)
            qkvr = _norm_proj(x, norm1_g[i], sc1, sh1, w_main, cs, tm=1024, tn=512, out_dtype=BF16)
            g_lr = _norm_proj(x, norm1_g[i], sc1, sh1, w_gate, jnp.ones((LANE,), F32), tm=1024, tn=LANE,
                              out_dtype=F32)
            outs = []
            for rev, w2, gb in ((False, gla_gate_w2_f[j], gla_gate_b_f[j]), (True, gla_gate_w2_b[j], gla_gate_b_b[j])):
                r0 = GLA_GATE_RANK if rev else 0
                w2p = jnp.zeros((LANE, w2.shape[1]), F32).at[r0:r0 + GLA_GATE_RANK].set(w2)
                w2h, w2l = _bf16_parts(w2p, 2)
                outs.append(_gla_scan(qkvr, g_lr, w2h, w2l, gb.reshape(1, -1).astype(F32), reverse=rev, blk=512))
            o = _gla_finish(outs[0], outs[1], qkvr, gla_norm_g[j], tm=512)
            w_out = gla_w_out[j]
        else:
            w_in = na_w_in[j].astype(BF16)
            cs = _col_scale(w_in.shape[1], NA_HEADS * NA_HEAD_DIM, NA_HEAD_DIM ** -0.5)
            qkv = _norm_proj(x, norm1_g[i], sc1, sh1, w_in, cs, tm=1024, tn=512, out_dtype=BF16)
            o = _neighborhood_attention(qkv, na_rpb[j])
            w_out = na_w_out[j]
        wr_hi, wr_lo, r_bias = _router_weights(router_g_w[i], router_g_b[i], router_e_w[i], router_e_b[i])
        x, hf, logits = _post_mixer(o, w_out.astype(BF16), x, g1, norm2_g[i], sc2, sh2, wr_hi, wr_lo, tm=256)
        logits_t = logits.reshape(n_tok, LANE).T[:ROUTER_ROWS]
        ids, wt = _router(logits_t, r_bias, tn=2048)
        ybuf = _moe_experts(hf.reshape(n_tok, d), ids.T, moe_w_gu[i], moe_w_d[i], blk=MOE_BLK)
        wts = wt.T
        g2 = g2_i
    return _combine(x, ybuf, wts, g2, final_g, final=True, tm=512)
```

```python
import functools
import math

import numpy as np
import jax
import jax.numpy as jnp
from jax import lax
from jax.experimental import pallas as pl
from jax.experimental.pallas import tpu as pltpu

F32 = jnp.float32
BF16 = jnp.bfloat16

D_MODEL = 2048
DEPTH = 4
N_MIXERS = 3
NORM_EPS = 1e-6
NEG_INF = -1e30
LOG2E = 1.4426950408889634

DA_HEADS = 8
DA_HEAD_DIM = 128
DA_VAL_DIM = 2 * DA_HEAD_DIM
DA_ROW_BLK = 16

GLA_HEADS = 4
GLA_DK = 256
GLA_DV = 512
GLA_GATE_RANK = 16
GLA_TAU = 16.0
GLA_CHUNK = 64

GRID_W = 64
NA_HEADS = 64
NA_HEAD_DIM = 32
NA_ROWS = 8
NA_COLS = 16
NA_GROUP = 4
NA_QROWS = 32

MOE_GROUPS = 4
MOE_EPG = 8
MOE_EXPERTS = 32
MOE_TOP_K = 2
MOE_D_FF = 512
MOE_BLK = 256

LANE = 128
VMEM_LIMIT = 56 * 1024 * 1024


def _cparams(sem):
    return pltpu.CompilerParams(dimension_semantics=sem, vmem_limit_bytes=VMEM_LIMIT)


def _split2(v):
    hi = v.astype(BF16)
    lo = (v - hi.astype(F32)).astype(BF16)
    return hi, lo


def _split3(v):
    hi = v.astype(BF16)
    r = v - hi.astype(F32)
    mid = r.astype(BF16)
    lo = (r - mid.astype(F32)).astype(BF16)
    return hi, mid, lo


def _bf16_parts(v, n):
    parts = []
    r = v.astype(F32)
    for _ in range(n):
        top = lax.bitcast_convert_type(
            lax.bitcast_convert_type(r, jnp.uint32) & jnp.uint32(0xFFFF0000), F32)
        parts.append(top.astype(BF16))
        r = r - top
    return parts


def _nt_dot(a, b):
    return lax.dot_general(a, b, (((1,), (1,)), ((), ())), preferred_element_type=F32)


def _tn_dot(a, b):
    return lax.dot_general(a, b, (((0,), (0,)), ((), ())), preferred_element_type=F32)


def _rms(x):
    return x * lax.rsqrt(jnp.mean(x * x, axis=-1, keepdims=True) + NORM_EPS)


def _ada_kernel(c_ref, w_ref, b_ref, o_ref):
    c = c_ref[...]
    cond = (c * jax.nn.sigmoid(c)).astype(BF16)
    o_ref[...] = jnp.dot(cond, w_ref[...].astype(BF16), preferred_element_type=F32) + b_ref[...]


def _ada_mod(c, ada_w, ada_b):
    depth, d, n = ada_w.shape
    bsz = c.shape[0]
    rows = 8
    c8 = jnp.zeros((rows, d), F32).at[:bsz].set(c)
    tn = 1024
    out = pl.pallas_call(
        _ada_kernel,
        grid=(depth, n // tn),
        in_specs=[
            pl.BlockSpec((rows, d), lambda l, j: (0, 0)),
            pl.BlockSpec((None, d, tn), lambda l, j: (l, 0, j)),
            pl.BlockSpec((None, 1, tn), lambda l, j: (l, 0, j)),
        ],
        out_specs=pl.BlockSpec((None, rows, tn), lambda l, j: (l, 0, j)),
        out_shape=jax.ShapeDtypeStruct((depth, rows, n), F32),
        compiler_params=_cparams(("parallel", "parallel")),
        name="ada_mod",
    )(c8, ada_w, ada_b.reshape(depth, 1, n))
    return out[:, :bsz]


def _proj_kernel(x_ref, g_ref, sc_ref, sh_ref, w_ref, cs_ref, o_ref, h_scr):
    @pl.when(pl.program_id(2) == 0)
    def _():
        h = _rms(x_ref[...]) * g_ref[...] * (1.0 + sc_ref[...]) + sh_ref[...]
        h_scr[...] = h.astype(BF16)

    acc = jnp.dot(h_scr[...], w_ref[...], preferred_element_type=F32)
    o_ref[...] = (acc * cs_ref[...]).astype(o_ref.dtype)


def _norm_proj(x, g, sc, sh, w, col_scale, *, tm, tn, out_dtype):
    bsz, t, d = x.shape
    n = w.shape[1]
    tm = min(tm, t)
    return pl.pallas_call(
        _proj_kernel,
        grid=(bsz, t // tm, n // tn),
        in_specs=[
            pl.BlockSpec((None, tm, d), lambda b, i, j: (b, i, 0)),
            pl.BlockSpec((1, d), lambda b, i, j: (0, 0)),
            pl.BlockSpec((None, 1, d), lambda b, i, j: (b, 0, 0)),
            pl.BlockSpec((None, 1, d), lambda b, i, j: (b, 0, 0)),
            pl.BlockSpec((d, tn), lambda b, i, j: (0, j)),
            pl.BlockSpec((1, tn), lambda b, i, j: (0, j)),
        ],
        out_specs=pl.BlockSpec((None, tm, tn), lambda b, i, j: (b, i, j)),
        out_shape=jax.ShapeDtypeStruct((bsz, t, n), out_dtype),
        scratch_shapes=[pltpu.VMEM((tm, d), BF16)],
        compiler_params=_cparams(("parallel", "parallel", "arbitrary")),
        name="norm_proj",
    )(x, g.reshape(1, d), sc, sh, w, col_scale.reshape(1, n))


def _da_kernel(c_ref, q_ref, k_ref, v_ref, qf_ref, kf_ref, u_ref, lam_ref, g_ref, o_ref,
               qa_scr, m_scr, l_scr, acc_scr, s_scr, p_scr, *, lam_init):
    h = pl.program_id(1)
    qi = pl.program_id(2)
    kb = pl.program_id(3)
    hd = DA_HEAD_DIM
    n_split, hr, tk = s_scr.shape[0] // 2, s_scr.shape[1], s_scr.shape[2]
    n_sub = k_ref.shape[0] // tk

    @pl.when(kb == 0)
    def _():
        m_scr[...] = jnp.full(m_scr.shape, -jnp.inf, F32)
        l_scr[...] = jnp.zeros(l_scr.shape, F32)
        acc_scr[...] = jnp.zeros(acc_scr.shape, F32)
        q = q_ref[...]
        qf = qf_ref[...]
        for m in range(2):
            qm = q[:, m * hd:(m + 1) * hd]
            qa_scr[m] = jnp.concatenate([qm, qf], axis=1)

    def step(diag, sub):
        kt = kb * n_sub + sub
        sgn = jnp.where(kt > qi, -1.0, 1.0).astype(BF16)
        ks = slice(sub * tk, (sub + 1) * tk)
        k = k_ref[ks, :]
        kf = kf_ref[ks, :] * sgn
        v = v_ref[ks, :]
        c2 = 2.0 * c_ref[h]
        for m in range(2):
            ka = jnp.concatenate([k[:, m * hd:(m + 1) * hd], kf], axis=1)
            for sp in range(n_split):
                s_scr[m * n_split + sp] = _nt_dot(qa_scr[m, sp * hr:(sp + 1) * hr, :], ka)
        nlt = tk // LANE
        blocks = [slice(rb * DA_ROW_BLK, (rb + 1) * DA_ROW_BLK) for rb in range(hr // DA_ROW_BLK)]
        for m in range(2):
            for sp in range(n_split):
                ci = m * n_split + sp
                rs = slice(sp * hr, (sp + 1) * hr)
                pmax = []
                for loc in blocks:
                    s = s_scr[ci, loc, :]
                    if diag:
                        s = s - c2 * u_ref[sp * hr + loc.start:sp * hr + loc.stop, :]
                        s_scr[ci, loc, :] = s
                    pm = s[:, 0:LANE]
                    for lt in range(1, nlt):
                        pm = jnp.maximum(pm, s[:, lt * LANE:(lt + 1) * LANE])
                    pmax.append(pm)
                m_old = m_scr[m, rs, :]
                m_new = jnp.maximum(m_old, jnp.max(jnp.concatenate(pmax, axis=0), axis=1, keepdims=True))
                alpha = jnp.exp2(m_old - m_new)
                m_scr[m, rs, :] = m_new
                psum = []
                for loc in blocks:
                    s = s_scr[ci, loc, :]
                    mb = m_new[loc]
                    acc_l = None
                    for lt in range(nlt):
                        p = jnp.exp2(s[:, lt * LANE:(lt + 1) * LANE] - mb)
                        acc_l = p if acc_l is None else acc_l + p
                        p_scr[ci, loc, lt * LANE:(lt + 1) * LANE] = p.astype(BF16)
                    psum.append(acc_l)
                row_sum = jnp.sum(jnp.concatenate(psum, axis=0), axis=1, keepdims=True)
                l_scr[m, rs, :] = alpha * l_scr[m, rs, :] + row_sum
                pv = jnp.dot(p_scr[ci], v, preferred_element_type=F32)
                acc_scr[m, rs, :] = jnp.concatenate([alpha, alpha], axis=1) * acc_scr[m, rs, :] + pv

    for sub in range(n_sub):
        @pl.when(kb * n_sub + sub != qi)
        def _():
            step(False, sub)

        @pl.when(kb * n_sub + sub == qi)
        def _():
            step(True, sub)

    @pl.when(kb == pl.num_programs(3) - 1)
    def _():
        lv = lam_ref[...]
        lam = (jnp.exp(jnp.sum(lv[0:1] * lv[1:2], axis=1, keepdims=True))
               - jnp.exp(jnp.sum(lv[2:3] * lv[3:4], axis=1, keepdims=True)) + lam_init)
        inv = [1.0 / l_scr[m] for m in range(2)]
        o = (acc_scr[0] * jnp.concatenate([inv[0], inv[0]], axis=1)
             - lam * (acc_scr[1] * jnp.concatenate([inv[1], inv[1]], axis=1)))
        o = _rms(o) * g_ref[...] * (1.0 - lam_init)
        o_ref[...] = o.astype(o_ref.dtype)


def _alibi_features(t, n_heads):
    slopes = 2.0 ** (-8.0 * np.arange(1, n_heads + 1) / n_heads)
    c = jnp.asarray(slopes * LOG2E, F32)
    cp = c[:, None] * jnp.arange(t, dtype=F32)[None, :]
    p3 = jnp.stack(_bf16_parts(cp, 3), axis=-1)
    ones = jnp.ones((n_heads, t, 3), BF16)
    pad = jnp.zeros((n_heads, t, LANE - 6), BF16)
    qf = jnp.concatenate([-p3, ones, pad], axis=-1)
    kf = jnp.concatenate([ones, p3, pad], axis=-1)
    return c, qf, kf


def _diff_attention(qkv, lam_vecs, subln_g, lam_init, *, tile, key_tiles):
    bsz, t, _ = qkv.shape
    nh = DA_HEADS
    tq = tk = min(tile, t)
    n_split = 2 if tq >= 4 * DA_ROW_BLK else 1
    n_sub = min(key_tiles, t // tk)
    tks = tk * n_sub
    c, qf, kf = _alibi_features(t, nh)
    ii = jnp.arange(tq, dtype=F32)
    u = jnp.maximum(ii[None, :] - ii[:, None], 0.0)
    grid_spec = pltpu.PrefetchScalarGridSpec(
        num_scalar_prefetch=1,
        grid=(bsz, nh, t // tq, t // tks),
        in_specs=[
            pl.BlockSpec((None, tq, DA_VAL_DIM), lambda b, h, i, j, c: (b, i, h)),
            pl.BlockSpec((None, tks, DA_VAL_DIM), lambda b, h, i, j, c: (b, j, nh + h)),
            pl.BlockSpec((None, tks, DA_VAL_DIM), lambda b, h, i, j, c: (b, j, 2 * nh + h)),
            pl.BlockSpec((None, tq, LANE), lambda b, h, i, j, c: (h, i, 0)),
            pl.BlockSpec((None, tks, LANE), lambda b, h, i, j, c: (h, j, 0)),
            pl.BlockSpec((tq, tk), lambda b, h, i, j, c: (0, 0)),
            pl.BlockSpec((4, DA_HEAD_DIM), lambda b, h, i, j, c: (0, 0)),
            pl.BlockSpec((1, DA_VAL_DIM), lambda b, h, i, j, c: (0, 0)),
        ],
        out_specs=pl.BlockSpec((None, tq, DA_VAL_DIM), lambda b, h, i, j, c: (b, i, h)),
        scratch_shapes=[
            pltpu.VMEM((2, tq, 2 * DA_HEAD_DIM), BF16),
            pltpu.VMEM((2, tq, LANE), F32),
            pltpu.VMEM((2, tq, LANE), F32),
            pltpu.VMEM((2, tq, DA_VAL_DIM), F32),
            pltpu.VMEM((2 * n_split, tq // n_split, tk), F32),
            pltpu.VMEM((2 * n_split, tq // n_split, tk), BF16),
        ],
    )
    return pl.pallas_call(
        functools.partial(_da_kernel, lam_init=lam_init),
        grid_spec=grid_spec,
        out_shape=jax.ShapeDtypeStruct((bsz, t, nh * DA_VAL_DIM), BF16),
        compiler_params=_cparams(("parallel", "parallel", "parallel", "arbitrary")),
        name="diff_attention",
    )(c, qkv, qkv, qkv, qf, kf, u, lam_vecs, subln_g.reshape(1, DA_VAL_DIM))


def _gla_kernel(q_ref, k_ref, v_ref, gl_ref, w2h_ref, w2l_ref, gb_ref, o_ref, st_scr, *, reverse, nchunk):
    cs = GLA_CHUNK

    @pl.when(pl.program_id(2) == 0)
    def _():
        st_scr[...] = jnp.zeros(st_scr.shape, F32)

    ri = lax.broadcasted_iota(jnp.int32, (cs, cs), 0)
    ci = lax.broadcasted_iota(jnp.int32, (cs, cs), 1)
    keep = (ci >= ri) if reverse else (ci <= ri)
    tri = jnp.where(keep, 1.0, 0.0).astype(BF16)
    w2h = w2h_ref[...]
    w2l = w2l_ref[...]
    order = range(nchunk - 1, -1, -1) if reverse else range(nchunk)
    for c in order:
        rows = slice(c * cs, (c + 1) * cs)
        gh, glo = _split2(gl_ref[rows, :])
        x = (jnp.dot(gh, w2h, preferred_element_type=F32) + jnp.dot(glo, w2h, preferred_element_type=F32)
             + jnp.dot(gh, w2l, preferred_element_type=F32) + gb_ref[...])
        la = (jnp.minimum(x, 0.0) - jnp.log1p(jnp.exp(-jnp.abs(x)))) * (1.0 / GLA_TAU)
        a1, a2, a3 = _split3(la)
        b = (jnp.dot(tri, a1, preferred_element_type=F32) + jnp.dot(tri, a2, preferred_element_type=F32)
             + jnp.dot(tri, a3, preferred_element_type=F32))
        tot = b[0:1] if reverse else b[cs - 1:cs]
        q = q_ref[rows, :].astype(F32)
        k = k_ref[rows, :].astype(F32)
        v = v_ref[rows, :]
        q_in = (q * jnp.exp(b)).astype(BF16)
        k_in = (k * jnp.exp(-b)).astype(BF16)
        k_out = (k * jnp.exp(tot - b)).astype(BF16)
        att = jnp.where(keep, _nt_dot(q_in, k_in), 0.0)
        st = st_scr[...]
        o = jnp.dot(att.astype(BF16), v, preferred_element_type=F32) + _nt_dot(q_in, st.astype(BF16))
        o_ref[rows, :] = o
        st_scr[...] = st * jnp.exp(tot) + _tn_dot(v, k_out)


def _gla_scan(qkvr, g_lr, w2h, w2l, gb, *, reverse, blk):
    bsz, t, _ = qkvr.shape
    nh = GLA_HEADS
    blk = min(blk, t)
    nblk = t // blk
    pos = (lambda i: nblk - 1 - i) if reverse else (lambda i: i)
    kq = (nh * GLA_DK) // GLA_DK
    kv = (2 * nh * GLA_DK) // GLA_DV
    return pl.pallas_call(
        functools.partial(_gla_kernel, reverse=reverse, nchunk=blk // GLA_CHUNK),
        grid=(bsz, nh, nblk),
        in_specs=[
            pl.BlockSpec((None, blk, GLA_DK), lambda b, h, i: (b, pos(i), h)),
            pl.BlockSpec((None, blk, GLA_DK), lambda b, h, i: (b, pos(i), kq + h)),
            pl.BlockSpec((None, blk, GLA_DV), lambda b, h, i: (b, pos(i), kv + h)),
            pl.BlockSpec((None, blk, LANE), lambda b, h, i: (b, pos(i), 0)),
            pl.BlockSpec((LANE, GLA_DK), lambda b, h, i: (0, h)),
            pl.BlockSpec((LANE, GLA_DK), lambda b, h, i: (0, h)),
            pl.BlockSpec((1, GLA_DK), lambda b, h, i: (0, h)),
        ],
        out_specs=pl.BlockSpec((None, blk, GLA_DV), lambda b, h, i: (b, pos(i), h)),
        out_shape=jax.ShapeDtypeStruct((bsz, t, nh * GLA_DV), F32),
        scratch_shapes=[pltpu.VMEM((GLA_DV, GLA_DK), F32)],
        compiler_params=_cparams(("parallel", "parallel", "arbitrary")),
        name="gla_scan_bwd" if reverse else "gla_scan_fwd",
    )(qkvr, qkvr, qkvr, g_lr, w2h, w2l, gb)


def _gla_fin_kernel(of_ref, ob_ref, r_ref, g_ref, o_ref):
    o = _rms(of_ref[...] + ob_ref[...]) * g_ref[...]
    r = r_ref[...].astype(F32)
    o_ref[...] = (o * (r * jax.nn.sigmoid(r))).astype(o_ref.dtype)


def _gla_finish(o_f, o_b, qkvr, norm_g, *, tm):
    bsz, t, _ = o_f.shape
    nh = GLA_HEADS
    tm = min(tm, t)
    kr = (2 * nh * GLA_DK + nh * GLA_DV) // GLA_DV
    return pl.pallas_call(
        _gla_fin_kernel,
        grid=(bsz, t // tm, nh),
        in_specs=[
            pl.BlockSpec((None, tm, GLA_DV), lambda b, i, h: (b, i, h)),
            pl.BlockSpec((None, tm, GLA_DV), lambda b, i, h: (b, i, h)),
            pl.BlockSpec((None, tm, GLA_DV), lambda b, i, h: (b, i, kr + h)),
            pl.BlockSpec((1, GLA_DV), lambda b, i, h: (0, 0)),
        ],
        out_specs=pl.BlockSpec((None, tm, GLA_DV), lambda b, i, h: (b, i, h)),
        out_shape=jax.ShapeDtypeStruct((bsz, t, nh * GLA_DV), BF16),
        compiler_params=_cparams(("parallel", "parallel", "parallel")),
        name="gla_finish",
    )(o_f, o_b, qkvr, norm_g.reshape(1, GLA_DV))


def _na_kernel(q_ref, kp, km, kn, vp, vm, vn, tbl_ref, o_ref, kcat, vcat, *, n_rows):
    i = pl.program_id(2)
    w = GRID_W
    halo = (NA_ROWS // 2) * w
    main = NA_QROWS * w
    for ref_p, ref_m, ref_n, cat in ((kp, km, kn, kcat), (vp, vm, vn, vcat)):
        cat[0:halo, :] = ref_p[...]
        cat[halo:halo + main, :] = ref_m[...]
        cat[halo + main:2 * halo + main, :] = ref_n[...]
    lane_head = lax.broadcasted_iota(jnp.int32, (w, LANE), 1) // NA_HEAD_DIM
    base = i * NA_QROWS
    for rr in range(NA_QROWS):
        r = base + rr
        r0 = jnp.clip(r - NA_ROWS // 2, 0, n_rows - NA_ROWS)
        start = pl.multiple_of((NA_ROWS // 2 + r0 - base) * w, w)
        delta = r - r0
        qr = q_ref[rr * w:(rr + 1) * w, :]
        qs = jnp.concatenate([jnp.where(lane_head == hh, qr, jnp.zeros_like(qr)) for hh in range(NA_GROUP)], axis=0)
        kb = kcat[pl.ds(start, NA_ROWS * w), :]
        vb = vcat[pl.ds(start, NA_ROWS * w), :]
        s = _nt_dot(qs, kb)
        bias = jnp.concatenate(
            [tbl_ref[2 * j - delta + NA_ROWS - 1].reshape(NA_GROUP * w, 2 * w) for j in range(NA_ROWS // 2)], axis=1)
        s = s + bias
        p = jnp.exp2(s - jnp.max(s, axis=1, keepdims=True))
        l = jnp.sum(p, axis=1, keepdims=True)
        o4 = jnp.dot(p.astype(BF16), vb, preferred_element_type=F32) / l
        o = jnp.zeros((w, LANE), F32)
        for hh in range(NA_GROUP):
            o = jnp.where(lane_head == hh, o4[hh * w:(hh + 1) * w, :], o)
        o_ref[rr * w:(rr + 1) * w, :] = o.astype(o_ref.dtype)


def _na_bias_table(rpb):
    col = np.arange(GRID_W)
    col_start = np.clip(col - NA_COLS // 2, 0, GRID_W - NA_COLS)
    in_window = (col[None, :] >= col_start[:, None]) & (col[None, :] < col_start[:, None] + NA_COLS)
    col_off = np.clip(col[None, :] - col[:, None] + NA_COLS - 1, 0, 2 * NA_COLS - 2)
    cb = jnp.where(in_window[None, None], rpb.astype(F32)[:, :, col_off] * LOG2E, NEG_INF)
    pair = jnp.concatenate([cb[:, :-1], cb[:, 1:]], axis=-1)
    n_pair = 2 * NA_ROWS - 2
    pair = pair.reshape(NA_HEADS // NA_GROUP, NA_GROUP, n_pair, GRID_W, 2 * GRID_W)
    return pair.transpose(0, 2, 1, 3, 4)


def _neighborhood_attention(qkv, rpb):
    bsz, t, _ = qkv.shape
    n_rows = t // GRID_W
    ng = NA_HEADS // NA_GROUP
    tq = NA_QROWS * GRID_W
    halo = (NA_ROWS // 2) * GRID_W
    per = tq // halo
    n_halo = t // halo
    tbl = _na_bias_table(rpb)

    def kv_specs(col0):
        return [
            pl.BlockSpec((None, halo, LANE), lambda g, b, i: (b, jnp.maximum(i * per - 1, 0), col0 + g)),
            pl.BlockSpec((None, tq, LANE), lambda g, b, i: (b, i, col0 + g)),
            pl.BlockSpec((None, halo, LANE), lambda g, b, i: (b, jnp.minimum((i + 1) * per, n_halo - 1), col0 + g)),
        ]

    cat_rows = tq + 2 * halo
    return pl.pallas_call(
        functools.partial(_na_kernel, n_rows=n_rows),
        grid=(ng, bsz, t // tq),
        in_specs=[pl.BlockSpec((None, tq, LANE), lambda g, b, i: (b, i, g))] + kv_specs(ng) + kv_specs(2 * ng)
        + [pl.BlockSpec((None, 2 * NA_ROWS - 2, NA_GROUP, GRID_W, 2 * GRID_W), lambda g, b, i: (g, 0, 0, 0, 0))],
        out_specs=pl.BlockSpec((None, tq, LANE), lambda g, b, i: (b, i, g)),
        out_shape=jax.ShapeDtypeStruct((bsz, t, NA_HEADS * NA_HEAD_DIM), BF16),
        scratch_shapes=[pltpu.VMEM((cat_rows, LANE), BF16), pltpu.VMEM((cat_rows, LANE), BF16)],
        compiler_params=_cparams(("parallel", "parallel", "parallel")),
        name="neighborhood_attention",
    )(qkv, *([qkv] * 6), tbl)


def _post_kernel(o_ref, w_ref, x_ref, g1_ref, n2_ref, sc_ref, sh_ref, wrh_ref, wrl_ref, xn_ref, hf_ref, lg_ref):
    y = jnp.dot(o_ref[...], w_ref[...], preferred_element_type=F32)
    xn = x_ref[...] + g1_ref[...] * y
    xn_ref[...] = xn
    hf = _rms(xn) * n2_ref[...] * (1.0 + sc_ref[...]) + sh_ref[...]
    hf_ref[...] = hf
    hh, hl = _split2(hf)
    wrh = wrh_ref[...]
    lg_ref[...] = (jnp.dot(hh, wrh, preferred_element_type=F32) + jnp.dot(hl, wrh, preferred_element_type=F32)
                   + jnp.dot(hh, wrl_ref[...], preferred_element_type=F32))


def _post_mixer(o, w_out, x, g1, n2g, sc2, sh2, wr_hi, wr_lo, *, tm):
    bsz, t, d = x.shape
    kdim = o.shape[-1]
    tm = min(tm, t)
    vec = pl.BlockSpec((None, 1, d), lambda b, i: (b, 0, 0))
    tile = pl.BlockSpec((None, tm, d), lambda b, i: (b, i, 0))
    return pl.pallas_call(
        _post_kernel,
        grid=(bsz, t // tm),
        in_specs=[
            pl.BlockSpec((None, tm, kdim), lambda b, i: (b, i, 0)),
            pl.BlockSpec((kdim, d), lambda b, i: (0, 0)),
            tile, vec,
            pl.BlockSpec((1, d), lambda b, i: (0, 0)),
            vec, vec,
            pl.BlockSpec((d, LANE), lambda b, i: (0, 0)),
            pl.BlockSpec((d, LANE), lambda b, i: (0, 0)),
        ],
        out_specs=[tile, tile, pl.BlockSpec((None, tm, LANE), lambda b, i: (b, i, 0))],
        out_shape=[jax.ShapeDtypeStruct((bsz, t, d), F32), jax.ShapeDtypeStruct((bsz, t, d), F32),
                   jax.ShapeDtypeStruct((bsz, t, LANE), F32)],
        compiler_params=_cparams(("parallel", "parallel")),
        name="post_mixer",
    )(o, w_out, x, g1, n2g.reshape(1, d), sc2, sh2, wr_hi, wr_lo)


ROUTER_ROWS = 8 + MOE_EXPERTS


def _router_kernel(lt_ref, b_ref, id_ref, w_ref):
    lt = lt_ref[...] + b_ref[...]
    tn = lt.shape[1]
    lg = lt[0:MOE_GROUPS]
    e = jnp.exp(lg - jnp.max(lg, axis=0, keepdims=True))
    gp = e / jnp.sum(e, axis=0, keepdims=True)
    g_p = jnp.max(gp, axis=0, keepdims=True)
    rg = lax.broadcasted_iota(jnp.int32, (MOE_GROUPS, tn), 0)
    g_idx = jnp.min(jnp.where(gp == g_p, rg, MOE_GROUPS), axis=0, keepdims=True)
    el = jnp.zeros((MOE_EPG, tn), F32)
    for g in range(MOE_GROUPS):
        el = jnp.where(g_idx == g, lt[8 + g * MOE_EPG:8 + (g + 1) * MOE_EPG], el)
    ee = jnp.exp(el - jnp.max(el, axis=0, keepdims=True))
    ep = ee / jnp.sum(ee, axis=0, keepdims=True)
    re = lax.broadcasted_iota(jnp.int32, (MOE_EPG, tn), 0)
    p1 = jnp.max(ep, axis=0, keepdims=True)
    i1 = jnp.min(jnp.where(ep == p1, re, MOE_EPG), axis=0, keepdims=True)
    ep2 = jnp.where(re == i1, -1.0, ep)
    p2 = jnp.max(ep2, axis=0, keepdims=True)
    i2 = jnp.min(jnp.where(ep2 == p2, re, MOE_EPG), axis=0, keepdims=True)
    den = p1 + p2
    id_ref[0:1, :] = g_idx * MOE_EPG + i1
    id_ref[1:2, :] = g_idx * MOE_EPG + i2
    w_ref[0:1, :] = g_p * (p1 / den)
    w_ref[1:2, :] = g_p * (p2 / den)


def _router(logits_t, bias_col, *, tn):
    n = logits_t.shape[1]
    tn = min(tn, n)
    return pl.pallas_call(
        _router_kernel,
        grid=(n // tn,),
        in_specs=[pl.BlockSpec((ROUTER_ROWS, tn), lambda i: (0, i)),
                  pl.BlockSpec((ROUTER_ROWS, 1), lambda i: (0, 0))],
        out_specs=[pl.BlockSpec((MOE_TOP_K, tn), lambda i: (0, i)), pl.BlockSpec((MOE_TOP_K, tn), lambda i: (0, i))],
        out_shape=[jax.ShapeDtypeStruct((MOE_TOP_K, n), jnp.int32), jax.ShapeDtypeStruct((MOE_TOP_K, n), F32)],
        compiler_params=_cparams(("parallel",)),
        name="router_topk",
    )(logits_t, bias_col)


MOE_DMA_UNROLL = 8


def _moe_kernel(pb_ref, pe_ref, r0_ref, r1_ref, fl_ref, np_ref, src_cur, src_nxt, dst_prv, dst_lst, hf_hbm,
                wgu_ref, wd_ref, y_hbm, wgu_bf, wd_bf, xbuf, obuf, gsem, ssem, *, n_blocks):
    s = pl.program_id(0)
    blk = xbuf.shape[1]
    ff = wd_ref.shape[0]
    b = pb_ref[s]
    slot = b % 2
    flags = fl_ref[s]
    active = s < np_ref[0]
    first = jnp.logical_and(active, (flags & 1) != 0)
    new_w = jnp.logical_and(active, (flags & 4) != 0)

    def gather_copy(src_ref, sl, r):
        return pltpu.make_async_copy(hf_hbm.at[pl.ds(src_ref[0, r], 1)], xbuf.at[sl, pl.ds(r, 1)], gsem.at[sl])

    def scatter_copy(dst_ref, sl, r):
        return pltpu.make_async_copy(obuf.at[sl, pl.ds(r, 1)], y_hbm.at[pl.ds(dst_ref[0, r], 1)], ssem.at[sl])

    def issue_loop(make):
        def body(g, carry):
            for j in range(MOE_DMA_UNROLL):
                make(g * MOE_DMA_UNROLL + j).start(priority=j % 2)
            return carry
        lax.fori_loop(0, blk // MOE_DMA_UNROLL, body, 0)

    def wait_gather(sl):
        pltpu.make_async_copy(hf_hbm.at[pl.ds(0, blk)], xbuf.at[sl], gsem.at[sl]).wait()

    def wait_scatter(sl):
        pltpu.make_async_copy(obuf.at[sl], y_hbm.at[pl.ds(0, blk)], ssem.at[sl]).wait()

    @pl.when(s == 0)
    def _():
        obuf[...] = jnp.zeros(obuf.shape, F32)
        issue_loop(lambda r: gather_copy(src_cur, 0, r))

    @pl.when(first)
    def _():
        wait_gather(slot)

        @pl.when(b >= 2)
        def _():
            wait_scatter(slot)

    @pl.when(new_w)
    def _():
        wgu_bf[...] = wgu_ref[...].astype(BF16)
        wd_bf[...] = wd_ref[...].astype(BF16)

    @pl.when(active)
    def _():
        do_gather = jnp.logical_and(first, b + 1 < n_blocks)
        do_scatter = jnp.logical_and(first, b >= 1)
        for r in range(blk):
            @pl.when(do_gather)
            def _():
                gather_copy(src_nxt, 1 - slot, r).start(priority=r % 2)

            @pl.when(do_scatter)
            def _():
                scatter_copy(dst_prv, 1 - slot, r).start(priority=(r + 1) % 2)

        x = xbuf[slot].astype(BF16)
        hgu = jnp.dot(x, wgu_bf[...], preferred_element_type=F32)
        g = hgu[:, :ff]
        u = hgu[:, ff:]
        act = (g * jax.nn.sigmoid(g) * u).astype(BF16)
        y = jnp.dot(act, wd_bf[...], preferred_element_type=F32)
        ri = lax.broadcasted_iota(jnp.int32, (blk, 1), 0)
        mine = jnp.logical_and(ri >= r0_ref[s], ri < r1_ref[s])
        obuf[slot] = jnp.where(mine, y, obuf[slot])

    @pl.when(s == pl.num_programs(0) - 1)
    def _():
        last_slot = (n_blocks - 1) % 2
        issue_loop(lambda r: scatter_copy(dst_lst, last_slot, r))
        wait_scatter(0)
        wait_scatter(1)


def _moe_plan(expert_id, blk):
    flat_e = expert_id.reshape(-1)
    n_asg = flat_e.shape[0]
    n_blocks = n_asg // blk
    n_steps = n_blocks + MOE_EXPERTS - 1
    onehot = (flat_e[:, None] == jnp.arange(MOE_EXPERTS, dtype=jnp.int32)[None, :]).astype(jnp.int32)
    csum = jnp.cumsum(onehot, axis=0)
    rank = jnp.sum(csum * onehot, axis=1) - 1
    counts = csum[-1]
    ends = jnp.cumsum(counts)
    starts = ends - counts
    slot = starts[flat_e] + rank
    asg = jnp.zeros((n_asg,), jnp.int32).at[slot].set(jnp.arange(n_asg, dtype=jnp.int32))
    src_tok = asg // MOE_TOP_K
    dst_row = (asg % MOE_TOP_K) * (n_asg // MOE_TOP_K) + asg // MOE_TOP_K
    bidx = jnp.arange(n_blocks, dtype=jnp.int32)
    e_lo = jnp.searchsorted(ends, bidx * blk, side='right').astype(jnp.int32)
    e_hi = jnp.searchsorted(ends, (bidx + 1) * blk - 1, side='right').astype(jnp.int32)
    n_pair_b = e_hi - e_lo + 1
    pair_end = jnp.cumsum(n_pair_b)
    pair_start = pair_end - n_pair_b
    n_pairs = pair_end[-1]
    sidx = jnp.arange(n_steps, dtype=jnp.int32)
    pb = jnp.minimum(jnp.searchsorted(pair_end, sidx, side='right').astype(jnp.int32), n_blocks - 1)
    pe = jnp.clip(e_lo[pb] + sidx - pair_start[pb], 0, MOE_EXPERTS - 1)
    pe = jnp.where(sidx < n_pairs, pe, pe[jnp.maximum(n_pairs - 1, 0)])
    r0 = jnp.clip(starts[pe] - pb * blk, 0, blk)
    r1 = jnp.clip(ends[pe] - pb * blk, 0, blk)
    prev_e = jnp.concatenate([jnp.full((1,), -1, jnp.int32), pe[:-1]])
    flags = ((sidx == pair_start[pb]).astype(jnp.int32) + 2 * (sidx == pair_end[pb] - 1).astype(jnp.int32)
             + 4 * (pe != prev_e).astype(jnp.int32))
    flags = jnp.where(sidx < n_pairs, flags, 0)
    i32 = lambda a: a.astype(jnp.int32)
    return (i32(pb), i32(pe), i32(r0), i32(r1), i32(flags), i32(n_pairs).reshape(1),
            src_tok.reshape(n_blocks, 1, blk), dst_row.reshape(n_blocks, 1, blk))


def _moe_experts(hf, expert_id, w_gu, w_d, layer, *, blk):
    n_tok, d = hf.shape
    n_asg = n_tok * MOE_TOP_K
    blk = min(blk, n_asg // 2)
    n_blocks = n_asg // blk
    pb, pe, r0, r1, flags, n_pairs, src_tok, dst_row = _moe_plan(expert_id, blk)
    ff = w_d.shape[2]
    smem = functools.partial(pl.BlockSpec, memory_space=pltpu.SMEM)
    grid_spec = pltpu.PrefetchScalarGridSpec(
        num_scalar_prefetch=6,
        grid=(pb.shape[0],),
        in_specs=[
            smem((None, 1, blk), lambda s, pb, *_: (pb[s], 0, 0)),
            smem((None, 1, blk), lambda s, pb, *_: (jnp.minimum(pb[s] + 1, n_blocks - 1), 0, 0)),
            smem((None, 1, blk), lambda s, pb, *_: (jnp.maximum(pb[s] - 1, 0), 0, 0)),
            smem((None, 1, blk), lambda s, pb, *_: (n_blocks - 1, 0, 0)),
            pl.BlockSpec(memory_space=pl.ANY),
            pl.BlockSpec((None, None, d, 2 * ff), lambda s, pb, pe, *_: (layer, pe[s], 0, 0)),
            pl.BlockSpec((None, None, ff, d), lambda s, pb, pe, *_: (layer, pe[s], 0, 0)),
        ],
        out_specs=pl.BlockSpec(memory_space=pl.ANY),
        scratch_shapes=[
            pltpu.VMEM((d, 2 * ff), BF16),
            pltpu.VMEM((ff, d), BF16),
            pltpu.VMEM((2, blk, d), F32),
            pltpu.VMEM((2, blk, d), F32),
            pltpu.SemaphoreType.DMA((2,)),
            pltpu.SemaphoreType.DMA((2,)),
        ],
    )
    return pl.pallas_call(
        functools.partial(_moe_kernel, n_blocks=n_blocks),
        grid_spec=grid_spec,
        out_shape=jax.ShapeDtypeStruct((n_asg, d), F32),
        compiler_params=_cparams(("arbitrary",)),
        name="moe_experts",
    )(pb, pe, r0, r1, flags, n_pairs, src_tok, src_tok, dst_row, dst_row, hf, w_gu, w_d)


def _combine_kernel(x_ref, y0_ref, y1_ref, w_ref, g2_ref, fg_ref, o_ref, *, final):
    w = w_ref[...]
    moe = w[:, 0:1] * y0_ref[...] + w[:, 1:2] * y1_ref[...]
    xn = x_ref[...] + g2_ref[...] * moe
    if final:
        xn = _rms(xn) * fg_ref[...]
    o_ref[...] = xn


def _combine(x, ybuf, wts, g2, final_g, *, final, tm):
    bsz, t, d = x.shape
    tm = min(tm, t)
    y2 = ybuf.reshape(MOE_TOP_K, bsz * t, d)
    nt = t // tm
    return pl.pallas_call(
        functools.partial(_combine_kernel, final=final),
        grid=(bsz, nt),
        in_specs=[
            pl.BlockSpec((None, tm, d), lambda b, i: (b, i, 0)),
            pl.BlockSpec((None, tm, d), lambda b, i: (0, b * nt + i, 0)),
            pl.BlockSpec((None, tm, d), lambda b, i: (1, b * nt + i, 0)),
            pl.BlockSpec((tm, MOE_TOP_K), lambda b, i: (b * nt + i, 0)),
            pl.BlockSpec((None, 1, d), lambda b, i: (b, 0, 0)),
            pl.BlockSpec((1, d), lambda b, i: (0, 0)),
        ],
        out_specs=pl.BlockSpec((None, tm, d), lambda b, i: (b, i, 0)),
        out_shape=jax.ShapeDtypeStruct((bsz, t, d), F32),
        compiler_params=_cparams(("parallel", "parallel")),
        name="moe_combine_final" if final else "moe_combine",
    )(x, y2, y2, wts, g2, final_g.reshape(1, d))


def _router_weights(rg_w, rg_b, re_w, re_b):
    d = rg_w.shape[0]
    wr = jnp.zeros((d, LANE), F32).at[:, 0:MOE_GROUPS].set(rg_w).at[:, 8:8 + MOE_EXPERTS].set(re_w)
    bias = jnp.zeros((ROUTER_ROWS, 1), F32).at[0:MOE_GROUPS, 0].set(rg_b).at[8:, 0].set(re_b)
    return tuple(_bf16_parts(wr, 2)) + (bias,)


def _col_scale(n, n_scaled, scale):
    return jnp.concatenate([jnp.full((n_scaled,), scale, F32), jnp.ones((n - n_scaled,), F32)])


def kernel(x, c, ada_w, ada_b, norm1_g, norm2_g, router_g_w, router_g_b, router_e_w, router_e_b, moe_w_gu, moe_w_d, da_w_in, da_w_out, da_lam_q1, da_lam_k1, da_lam_q2, da_lam_k2, da_subln_g, gla_w_in, gla_gate_w2_f, gla_gate_b_f, gla_gate_w2_b, gla_gate_b_b, gla_norm_g, gla_w_out, na_w_in, na_rpb, na_w_out, final_g):
    bsz, t, d = x.shape
    n_tok = bsz * t
    mod = _ada_mod(c, ada_w, ada_b)
    wts = ybuf = g2 = None
    for i in range(DEPTH):
        sh1, sc1, g1, sh2, sc2, g2_i = [mod[i, :, None, m * d:(m + 1) * d] for m in range(6)]
        if i > 0:
            x = _combine(x, ybuf, wts, g2, final_g, final=False, tm=512)
        kind, j = i % N_MIXERS, i // N_MIXERS
        if kind == 0:
            w_in = da_w_in[j].astype(BF16)
            cs = _col_scale(w_in.shape[1], DA_HEADS * DA_VAL_DIM, DA_HEAD_DIM ** -0.5 * LOG2E)
            qkv = _norm_proj(x, norm1_g[i], sc1, sh1, w_in, cs, tm=1024, tn=1024, out_dtype=BF16)
            lam_vecs = jnp.stack([da_lam_q1[j], da_lam_k1[j], da_lam_q2[j], da_lam_k2[j]]).astype(F32)
            lam_init = 0.8 - 0.6 * math.exp(-0.3 * i)
            o = _diff_attention(qkv, lam_vecs, da_subln_g[j], lam_init, tile=1024, key_tiles=4)
            w_out = da_w_out[j]
        elif kind == 1:
            n_main = 2 * GLA_HEADS * GLA_DK + 2 * GLA_HEADS * GLA_DV
            w_main = gla_w_in[j][:, :n_main].astype(BF16)
            w_gate = jnp.zeros((d, LANE), F32).at[:, :2 * GLA_GATE_RANK].set(gla_w_in[j][:, n_main:]).astype(BF16)
            cs = _col_scale(n_main, GLA_HEADS * GLA_DK, GLA_DK ** -0.5)
            qkvr = _norm_proj(x, norm1_g[i], sc1, sh1, w_main, cs, tm=1024, tn=1024, out_dtype=BF16)
            g_lr = _norm_proj(x, norm1_g[i], sc1, sh1, w_gate, jnp.ones((LANE,), F32), tm=1024, tn=LANE,
                              out_dtype=F32)
            outs = []
            for rev, w2, gb in ((False, gla_gate_w2_f[j], gla_gate_b_f[j]), (True, gla_gate_w2_b[j], gla_gate_b_b[j])):
                r0 = GLA_GATE_RANK if rev else 0
                w2p = jnp.zeros((LANE, w2.shape[1]), F32).at[r0:r0 + GLA_GATE_RANK].set(w2)
                w2h, w2l = _bf16_parts(w2p, 2)
                outs.append(_gla_scan(qkvr, g_lr, w2h, w2l, gb.reshape(1, -1).astype(F32), reverse=rev, blk=512))
            o = _gla_finish(outs[0], outs[1], qkvr, gla_norm_g[j], tm=512)
            w_out = gla_w_out[j]
        else:
            w_in = na_w_in[j].astype(BF16)
            cs = _col_scale(w_in.shape[1], NA_HEADS * NA_HEAD_DIM, NA_HEAD_DIM ** -0.5 * LOG2E)
            qkv = _norm_proj(x, norm1_g[i], sc1, sh1, w_in, cs, tm=1024, tn=1024, out_dtype=BF16)
            o = _neighborhood_attention(qkv, na_rpb[j])
            w_out = na_w_out[j]
        wr_hi, wr_lo, r_bias = _router_weights(router_g_w[i], router_g_b[i], router_e_w[i], router_e_b[i])
        x, hf, logits = _post_mixer(o, w_out.astype(BF16), x, g1, norm2_g[i], sc2, sh2, wr_hi, wr_lo, tm=256)
        logits_t = logits.reshape(n_tok, LANE).T[:ROUTER_ROWS]
        ids, wt = _router(logits_t, r_bias, tn=2048)
        ybuf = _moe_experts(hf.reshape(n_tok, d), ids.T, moe_w_gu, moe_w_d, i, blk=MOE_BLK)
        wts = wt.T
        g2 = g2_i
    return _combine(x, ybuf, wts, g2, final_g, final=True, tm=512)
```

```python
import functools
import math

import numpy as np
import jax
import jax.numpy as jnp
from jax import lax
from jax.experimental import pallas as pl
from jax.experimental.pallas import tpu as pltpu

F32 = jnp.float32
BF16 = jnp.bfloat16

D_MODEL = 2048
DEPTH = 4
N_MIXERS = 3
NORM_EPS = 1e-6
NEG_INF = -1e30
LOG2E = 1.4426950408889634

DA_HEADS = 8
DA_HEAD_DIM = 128
DA_VAL_DIM = 2 * DA_HEAD_DIM
DA_ROW_BLK = 16

GLA_HEADS = 4
GLA_DK = 256
GLA_DV = 512
GLA_GATE_RANK = 16
GLA_TAU = 16.0
GLA_CHUNK = 64

GRID_W = 64
NA_HEADS = 64
NA_HEAD_DIM = 32
NA_ROWS = 8
NA_COLS = 16
NA_GROUP = 4
NA_QROWS = 32

MOE_GROUPS = 4
MOE_EPG = 8
MOE_EXPERTS = 32
MOE_TOP_K = 2
MOE_D_FF = 512
MOE_BLK = 256

LANE = 128
VMEM_LIMIT = 56 * 1024 * 1024


def _cparams(sem):
    return pltpu.CompilerParams(dimension_semantics=sem, vmem_limit_bytes=VMEM_LIMIT)


def _split2(v):
    hi = v.astype(BF16)
    lo = (v - hi.astype(F32)).astype(BF16)
    return hi, lo


def _split3(v):
    hi = v.astype(BF16)
    r = v - hi.astype(F32)
    mid = r.astype(BF16)
    lo = (r - mid.astype(F32)).astype(BF16)
    return hi, mid, lo


def _bf16_parts(v, n):
    parts = []
    r = v.astype(F32)
    for _ in range(n):
        top = lax.bitcast_convert_type(
            lax.bitcast_convert_type(r, jnp.uint32) & jnp.uint32(0xFFFF0000), F32)
        parts.append(top.astype(BF16))
        r = r - top
    return parts


def _nt_dot(a, b):
    return lax.dot_general(a, b, (((1,), (1,)), ((), ())), preferred_element_type=F32)


def _tn_dot(a, b):
    return lax.dot_general(a, b, (((0,), (0,)), ((), ())), preferred_element_type=F32)


def _rms(x):
    return x * lax.rsqrt(jnp.mean(x * x, axis=-1, keepdims=True) + NORM_EPS)


def _ada_kernel(c_ref, w_ref, b_ref, o_ref):
    c = c_ref[...]
    cond = (c * jax.nn.sigmoid(c)).astype(BF16)
    o_ref[...] = jnp.dot(cond, w_ref[...].astype(BF16), preferred_element_type=F32) + b_ref[...]


def _ada_mod(c, ada_w, ada_b):
    depth, d, n = ada_w.shape
    bsz = c.shape[0]
    rows = 8
    c8 = jnp.zeros((rows, d), F32).at[:bsz].set(c)
    tn = 1024
    out = pl.pallas_call(
        _ada_kernel,
        grid=(depth, n // tn),
        in_specs=[
            pl.BlockSpec((rows, d), lambda l, j: (0, 0)),
            pl.BlockSpec((None, d, tn), lambda l, j: (l, 0, j)),
            pl.BlockSpec((None, 1, tn), lambda l, j: (l, 0, j)),
        ],
        out_specs=pl.BlockSpec((None, rows, tn), lambda l, j: (l, 0, j)),
        out_shape=jax.ShapeDtypeStruct((depth, rows, n), F32),
        compiler_params=_cparams(("parallel", "parallel")),
        name="ada_mod",
    )(c8, ada_w, ada_b.reshape(depth, 1, n))
    return out[:, :bsz]


def _proj_kernel(x_ref, g_ref, sc_ref, sh_ref, w_ref, cs_ref, o_ref, h_scr):
    @pl.when(pl.program_id(2) == 0)
    def _():
        h = _rms(x_ref[...]) * g_ref[...] * (1.0 + sc_ref[...]) + sh_ref[...]
        h_scr[...] = h.astype(BF16)

    acc = jnp.dot(h_scr[...], w_ref[...], preferred_element_type=F32)
    o_ref[...] = (acc * cs_ref[...]).astype(o_ref.dtype)


def _norm_proj(x, g, sc, sh, w, col_scale, *, tm, tn, out_dtype):
    bsz, t, d = x.shape
    n = w.shape[1]
    tm = min(tm, t)
    return pl.pallas_call(
        _proj_kernel,
        grid=(bsz, t // tm, n // tn),
        in_specs=[
            pl.BlockSpec((None, tm, d), lambda b, i, j: (b, i, 0)),
            pl.BlockSpec((1, d), lambda b, i, j: (0, 0)),
            pl.BlockSpec((None, 1, d), lambda b, i, j: (b, 0, 0)),
            pl.BlockSpec((None, 1, d), lambda b, i, j: (b, 0, 0)),
            pl.BlockSpec((d, tn), lambda b, i, j: (0, j)),
            pl.BlockSpec((1, tn), lambda b, i, j: (0, j)),
        ],
        out_specs=pl.BlockSpec((None, tm, tn), lambda b, i, j: (b, i, j)),
        out_shape=jax.ShapeDtypeStruct((bsz, t, n), out_dtype),
        scratch_shapes=[pltpu.VMEM((tm, d), BF16)],
        compiler_params=_cparams(("parallel", "parallel", "arbitrary")),
        name="norm_proj",
    )(x, g.reshape(1, d), sc, sh, w, col_scale.reshape(1, n))


def _da_kernel(c_ref, skip_ref, kidx_ref, q_ref, k_ref, v_ref, qf_ref, kf_ref, u_ref, lam_ref, g_ref, o_ref,
               qa_scr, m_scr, l_scr, acc_scr, s_scr, p_scr, *, lam_init):
    h = pl.program_id(1)
    qi = pl.program_id(2)
    kb = pl.program_id(3)
    nk = pl.num_programs(3)
    hd = DA_HEAD_DIM
    n_split, hr, tk = s_scr.shape[0] // 2, s_scr.shape[1], s_scr.shape[2]
    kt = lax.rem(qi + kb, nk)
    flat = ((pl.program_id(0) * pl.num_programs(1) + h) * pl.num_programs(2) + qi) * nk + kb

    @pl.when(kb == 0)
    def _():
        m_scr[...] = jnp.full(m_scr.shape, -jnp.inf, F32)
        l_scr[...] = jnp.zeros(l_scr.shape, F32)
        acc_scr[...] = jnp.zeros(acc_scr.shape, F32)
        q = q_ref[...]
        qf = qf_ref[...]
        for m in range(2):
            qm = q[:, m * hd:(m + 1) * hd]
            qa_scr[m] = jnp.concatenate([qm, qf], axis=1)

    def step(diag):
        sgn = jnp.where(kt > qi, -1.0, 1.0).astype(BF16)
        k = k_ref[...]
        kf = kf_ref[...] * sgn
        v = v_ref[...]
        c2 = 2.0 * c_ref[h]
        for m in range(2):
            ka = jnp.concatenate([k[:, m * hd:(m + 1) * hd], kf], axis=1)
            for sp in range(n_split):
                s_scr[m * n_split + sp] = _nt_dot(qa_scr[m, sp * hr:(sp + 1) * hr, :], ka)
        nlt = tk // LANE
        blocks = [slice(rb * DA_ROW_BLK, (rb + 1) * DA_ROW_BLK) for rb in range(hr // DA_ROW_BLK)]
        for m in range(2):
            for sp in range(n_split):
                ci = m * n_split + sp
                rs = slice(sp * hr, (sp + 1) * hr)
                pmax = []
                for loc in blocks:
                    s = s_scr[ci, loc, :]
                    if diag:
                        s = s - c2 * u_ref[sp * hr + loc.start:sp * hr + loc.stop, :]
                        s_scr[ci, loc, :] = s
                    pm = s[:, 0:LANE]
                    for lt in range(1, nlt):
                        pm = jnp.maximum(pm, s[:, lt * LANE:(lt + 1) * LANE])
                    pmax.append(pm)
                m_old = m_scr[m, rs, :]
                m_new = jnp.maximum(m_old, jnp.max(jnp.concatenate(pmax, axis=0), axis=1, keepdims=True))
                alpha = jnp.exp2(m_old - m_new)
                m_scr[m, rs, :] = m_new
                psum = []
                for loc in blocks:
                    s = s_scr[ci, loc, :]
                    mb = m_new[loc]
                    acc_l = None
                    for lt in range(nlt):
                        p = jnp.exp2(s[:, lt * LANE:(lt + 1) * LANE] - mb)
                        acc_l = p if acc_l is None else acc_l + p
                        p_scr[ci, loc, lt * LANE:(lt + 1) * LANE] = p.astype(BF16)
                    psum.append(acc_l)
                row_sum = jnp.sum(jnp.concatenate(psum, axis=0), axis=1, keepdims=True)
                l_scr[m, rs, :] = alpha * l_scr[m, rs, :] + row_sum
                pv = jnp.dot(p_scr[ci], v, preferred_element_type=F32)
                acc_scr[m, rs, :] = jnp.concatenate([alpha, alpha], axis=1) * acc_scr[m, rs, :] + pv

    @pl.when(kb == 0)
    def _():
        step(True)

    @pl.when(jnp.logical_and(kb > 0, skip_ref[flat] == 0))
    def _():
        step(False)

    @pl.when(kb == nk - 1)
    def _():
        lv = lam_ref[...]
        lam = (jnp.exp(jnp.sum(lv[0:1] * lv[1:2], axis=1, keepdims=True))
               - jnp.exp(jnp.sum(lv[2:3] * lv[3:4], axis=1, keepdims=True)) + lam_init)
        inv = [1.0 / l_scr[m] for m in range(2)]
        o = (acc_scr[0] * jnp.concatenate([inv[0], inv[0]], axis=1)
             - lam * (acc_scr[1] * jnp.concatenate([inv[1], inv[1]], axis=1)))
        o = _rms(o) * g_ref[...] * (1.0 - lam_init)
        o_ref[...] = o.astype(o_ref.dtype)


def _alibi_features(t, n_heads):
    slopes = 2.0 ** (-8.0 * np.arange(1, n_heads + 1) / n_heads)
    c = jnp.asarray(slopes * LOG2E, F32)
    cp = c[:, None] * jnp.arange(t, dtype=F32)[None, :]
    p3 = jnp.stack(_bf16_parts(cp, 3), axis=-1)
    ones = jnp.ones((n_heads, t, 3), BF16)
    pad = jnp.zeros((n_heads, t, LANE - 6), BF16)
    qf = jnp.concatenate([-p3, ones, pad], axis=-1)
    kf = jnp.concatenate([ones, p3, pad], axis=-1)
    return c, qf, kf


DA_SKIP_MARGIN = 200.0


def _norms_kernel(x_ref, o_ref):
    x = x_ref[...].astype(F32)
    cols = []
    for g in range(x.shape[1] // LANE):
        v = x[:, g * LANE:(g + 1) * LANE]
        cols.append(jnp.max(jnp.sum(v * v, axis=1, keepdims=True), axis=0, keepdims=True))
    o_ref[...] = jnp.concatenate(cols, axis=1)


def _da_tile_norms(qkv, tile):
    bsz, t, _ = qkv.shape
    width = 2 * DA_HEADS * DA_VAL_DIM
    return pl.pallas_call(
        _norms_kernel,
        grid=(bsz, t // tile),
        in_specs=[pl.BlockSpec((None, tile, width), lambda b, i: (b, i, 0))],
        out_specs=pl.BlockSpec((None, None, 1, width // LANE), lambda b, i: (b, i, 0, 0)),
        out_shape=jax.ShapeDtypeStruct((bsz, t // tile, 1, width // LANE), F32),
        compiler_params=_cparams(("parallel", "parallel")),
        name="da_tile_norms",
    )(qkv)


def _da_skip_plan(norms2, c, tile):
    bsz, nt = norms2.shape[0], norms2.shape[1]
    nh = DA_HEADS
    nrm = jnp.sqrt(norms2.reshape(bsz, nt, 2, nh, 2)) * 1.001
    nq, nkk = nrm[:, :, 0], nrm[:, :, 1]
    qi = jnp.arange(nt)[:, None]
    j = jnp.arange(nt)[None, :]
    kt = (qi + j) % nt
    bound = jnp.max(nq[:, :, None] * (nkk[:, kt] + nkk[:, :, None]), axis=-1)
    dist_min = jnp.maximum(jnp.abs(kt - qi) - 1, 0) * tile + 1
    far = c[None, None, None, :] * dist_min[None, :, :, None].astype(F32) >= DA_SKIP_MARGIN + bound
    skip = jnp.logical_and(far, (j > 0)[None, :, :, None]).transpose(0, 3, 1, 2)
    kidx = jnp.broadcast_to(kt[None, None], skip.shape)
    cols = [kidx[..., 0]]
    for jj in range(1, nt):
        cols.append(jnp.where(skip[..., jj], cols[-1], kidx[..., jj]))
    kidx = jnp.stack(cols, axis=-1)
    return skip.astype(jnp.int32).reshape(-1), kidx.astype(jnp.int32).reshape(-1)


def _diff_attention(qkv, lam_vecs, subln_g, lam_init, *, tile):
    bsz, t, _ = qkv.shape
    nh = DA_HEADS
    tq = tk = min(tile, t)
    nq = t // tq
    n_split = 2 if tq >= 4 * DA_ROW_BLK else 1
    c, qf, kf = _alibi_features(t, nh)
    skip, kidx = _da_skip_plan(_da_tile_norms(qkv, tq), c, tq)
    ii = jnp.arange(tq, dtype=F32)
    u = jnp.maximum(ii[None, :] - ii[:, None], 0.0)

    def key_tile(b, h, i, j, kidx_ref):
        return kidx_ref[((b * nh + h) * nq + i) * nq + j]

    grid_spec = pltpu.PrefetchScalarGridSpec(
        num_scalar_prefetch=3,
        grid=(bsz, nh, nq, nq),
        in_specs=[
            pl.BlockSpec((None, tq, DA_VAL_DIM), lambda b, h, i, j, c, sk, ki: (b, i, h)),
            pl.BlockSpec((None, tk, DA_VAL_DIM), lambda b, h, i, j, c, sk, ki: (b, key_tile(b, h, i, j, ki), nh + h)),
            pl.BlockSpec((None, tk, DA_VAL_DIM),
                         lambda b, h, i, j, c, sk, ki: (b, key_tile(b, h, i, j, ki), 2 * nh + h)),
            pl.BlockSpec((None, tq, LANE), lambda b, h, i, j, c, sk, ki: (h, i, 0)),
            pl.BlockSpec((None, tk, LANE), lambda b, h, i, j, c, sk, ki: (h, key_tile(b, h, i, j, ki), 0)),
            pl.BlockSpec((tq, tk), lambda b, h, i, j, c, sk, ki: (0, 0)),
            pl.BlockSpec((4, DA_HEAD_DIM), lambda b, h, i, j, c, sk, ki: (0, 0)),
            pl.BlockSpec((1, DA_VAL_DIM), lambda b, h, i, j, c, sk, ki: (0, 0)),
        ],
        out_specs=pl.BlockSpec((None, tq, DA_VAL_DIM), lambda b, h, i, j, c, sk, ki: (b, i, h)),
        scratch_shapes=[
            pltpu.VMEM((2, tq, 2 * DA_HEAD_DIM), BF16),
            pltpu.VMEM((2, tq, LANE), F32),
            pltpu.VMEM((2, tq, LANE), F32),
            pltpu.VMEM((2, tq, DA_VAL_DIM), F32),
            pltpu.VMEM((2 * n_split, tq // n_split, tk), F32),
            pltpu.VMEM((2 * n_split, tq // n_split, tk), BF16),
        ],
    )
    return pl.pallas_call(
        functools.partial(_da_kernel, lam_init=lam_init),
        grid_spec=grid_spec,
        out_shape=jax.ShapeDtypeStruct((bsz, t, nh * DA_VAL_DIM), BF16),
        compiler_params=_cparams(("parallel", "parallel", "parallel", "arbitrary")),
        name="diff_attention",
    )(c, skip, kidx, qkv, qkv, qkv, qf, kf, u, lam_vecs, subln_g.reshape(1, DA_VAL_DIM))


def _gla_kernel(q_ref, k_ref, v_ref, gl_ref, w2h_ref, w2l_ref, gb_ref, o_ref, st_scr, *, reverse, nchunk):
    cs = GLA_CHUNK

    @pl.when(pl.program_id(2) == 0)
    def _():
        st_scr[...] = jnp.zeros(st_scr.shape, F32)

    ri = lax.broadcasted_iota(jnp.int32, (cs, cs), 0)
    ci = lax.broadcasted_iota(jnp.int32, (cs, cs), 1)
    keep = (ci >= ri) if reverse else (ci <= ri)
    tri = jnp.where(keep, 1.0, 0.0).astype(BF16)
    w2h = w2h_ref[...]
    w2l = w2l_ref[...]
    order = range(nchunk - 1, -1, -1) if reverse else range(nchunk)
    for c in order:
        rows = slice(c * cs, (c + 1) * cs)
        gh, glo = _split2(gl_ref[rows, :])
        x = (jnp.dot(gh, w2h, preferred_element_type=F32) + jnp.dot(glo, w2h, preferred_element_type=F32)
             + jnp.dot(gh, w2l, preferred_element_type=F32) + gb_ref[...])
        la = (jnp.minimum(x, 0.0) - jnp.log1p(jnp.exp(-jnp.abs(x)))) * (1.0 / GLA_TAU)
        a1, a2, a3 = _split3(la)
        b = (jnp.dot(tri, a1, preferred_element_type=F32) + jnp.dot(tri, a2, preferred_element_type=F32)
             + jnp.dot(tri, a3, preferred_element_type=F32))
        tot = b[0:1] if reverse else b[cs - 1:cs]
        q = q_ref[rows, :].astype(F32)
        k = k_ref[rows, :].astype(F32)
        v = v_ref[rows, :]
        q_in = (q * jnp.exp(b)).astype(BF16)
        k_in = (k * jnp.exp(-b)).astype(BF16)
        k_out = (k * jnp.exp(tot - b)).astype(BF16)
        att = jnp.where(keep, _nt_dot(q_in, k_in), 0.0)
        st = st_scr[...]
        o = jnp.dot(att.astype(BF16), v, preferred_element_type=F32) + _nt_dot(q_in, st.astype(BF16))
        o_ref[rows, :] = o
        st_scr[...] = st * jnp.exp(tot) + _tn_dot(v, k_out)


def _gla_scan(qkvr, g_lr, w2h, w2l, gb, *, reverse, blk):
    bsz, t, _ = qkvr.shape
    nh = GLA_HEADS
    blk = min(blk, t)
    nblk = t // blk
    pos = (lambda i: nblk - 1 - i) if reverse else (lambda i: i)
    kq = (nh * GLA_DK) // GLA_DK
    kv = (2 * nh * GLA_DK) // GLA_DV
    return pl.pallas_call(
        functools.partial(_gla_kernel, reverse=reverse, nchunk=blk // GLA_CHUNK),
        grid=(bsz, nh, nblk),
        in_specs=[
            pl.BlockSpec((None, blk, GLA_DK), lambda b, h, i: (b, pos(i), h)),
            pl.BlockSpec((None, blk, GLA_DK), lambda b, h, i: (b, pos(i), kq + h)),
            pl.BlockSpec((None, blk, GLA_DV), lambda b, h, i: (b, pos(i), kv + h)),
            pl.BlockSpec((None, blk, LANE), lambda b, h, i: (b, pos(i), 0)),
            pl.BlockSpec((LANE, GLA_DK), lambda b, h, i: (0, h)),
            pl.BlockSpec((LANE, GLA_DK), lambda b, h, i: (0, h)),
            pl.BlockSpec((1, GLA_DK), lambda b, h, i: (0, h)),
        ],
        out_specs=pl.BlockSpec((None, blk, GLA_DV), lambda b, h, i: (b, pos(i), h)),
        out_shape=jax.ShapeDtypeStruct((bsz, t, nh * GLA_DV), F32),
        scratch_shapes=[pltpu.VMEM((GLA_DV, GLA_DK), F32)],
        compiler_params=_cparams(("parallel", "parallel", "arbitrary")),
        name="gla_scan_bwd" if reverse else "gla_scan_fwd",
    )(qkvr, qkvr, qkvr, g_lr, w2h, w2l, gb)


def _gla_fin_kernel(of_ref, ob_ref, r_ref, g_ref, o_ref):
    o = _rms(of_ref[...] + ob_ref[...]) * g_ref[...]
    r = r_ref[...].astype(F32)
    o_ref[...] = (o * (r * jax.nn.sigmoid(r))).astype(o_ref.dtype)


def _gla_finish(o_f, o_b, qkvr, norm_g, *, tm):
    bsz, t, _ = o_f.shape
    nh = GLA_HEADS
    tm = min(tm, t)
    kr = (2 * nh * GLA_DK + nh * GLA_DV) // GLA_DV
    return pl.pallas_call(
        _gla_fin_kernel,
        grid=(bsz, t // tm, nh),
        in_specs=[
            pl.BlockSpec((None, tm, GLA_DV), lambda b, i, h: (b, i, h)),
            pl.BlockSpec((None, tm, GLA_DV), lambda b, i, h: (b, i, h)),
            pl.BlockSpec((None, tm, GLA_DV), lambda b, i, h: (b, i, kr + h)),
            pl.BlockSpec((1, GLA_DV), lambda b, i, h: (0, 0)),
        ],
        out_specs=pl.BlockSpec((None, tm, GLA_DV), lambda b, i, h: (b, i, h)),
        out_shape=jax.ShapeDtypeStruct((bsz, t, nh * GLA_DV), BF16),
        compiler_params=_cparams(("parallel", "parallel", "parallel")),
        name="gla_finish",
    )(o_f, o_b, qkvr, norm_g.reshape(1, GLA_DV))


def _na_kernel(q_ref, kp, km, kn, vp, vm, vn, tbl_ref, o_ref, kcat, vcat, *, n_rows):
    i = pl.program_id(2)
    w = GRID_W
    halo = (NA_ROWS // 2) * w
    main = NA_QROWS * w
    for ref_p, ref_m, ref_n, cat in ((kp, km, kn, kcat), (vp, vm, vn, vcat)):
        cat[0:halo, :] = ref_p[...]
        cat[halo:halo + main, :] = ref_m[...]
        cat[halo + main:2 * halo + main, :] = ref_n[...]
    lane_head = lax.broadcasted_iota(jnp.int32, (w, LANE), 1) // NA_HEAD_DIM
    base = i * NA_QROWS
    for rr in range(NA_QROWS):
        r = base + rr
        r0 = jnp.clip(r - NA_ROWS // 2, 0, n_rows - NA_ROWS)
        start = pl.multiple_of((NA_ROWS // 2 + r0 - base) * w, w)
        delta = r - r0
        qr = q_ref[rr * w:(rr + 1) * w, :]
        qs = jnp.concatenate([jnp.where(lane_head == hh, qr, jnp.zeros_like(qr)) for hh in range(NA_GROUP)], axis=0)
        kb = kcat[pl.ds(start, NA_ROWS * w), :]
        vb = vcat[pl.ds(start, NA_ROWS * w), :]
        s = _nt_dot(qs, kb)
        bias = jnp.concatenate(
            [tbl_ref[2 * j - delta + NA_ROWS - 1].reshape(NA_GROUP * w, 2 * w) for j in range(NA_ROWS // 2)], axis=1)
        s = s + bias
        p = jnp.exp2(s - jnp.max(s, axis=1, keepdims=True))
        l = jnp.sum(p, axis=1, keepdims=True)
        o4 = jnp.dot(p.astype(BF16), vb, preferred_element_type=F32) / l
        o = jnp.zeros((w, LANE), F32)
        for hh in range(NA_GROUP):
            o = jnp.where(lane_head == hh, o4[hh * w:(hh + 1) * w, :], o)
        o_ref[rr * w:(rr + 1) * w, :] = o.astype(o_ref.dtype)


def _na_bias_table(rpb):
    col = np.arange(GRID_W)
    col_start = np.clip(col - NA_COLS // 2, 0, GRID_W - NA_COLS)
    in_window = (col[None, :] >= col_start[:, None]) & (col[None, :] < col_start[:, None] + NA_COLS)
    col_off = np.clip(col[None, :] - col[:, None] + NA_COLS - 1, 0, 2 * NA_COLS - 2)
    cb = jnp.where(in_window[None, None], rpb.astype(F32)[:, :, col_off] * LOG2E, NEG_INF)
    pair = jnp.concatenate([cb[:, :-1], cb[:, 1:]], axis=-1)
    n_pair = 2 * NA_ROWS - 2
    pair = pair.reshape(NA_HEADS // NA_GROUP, NA_GROUP, n_pair, GRID_W, 2 * GRID_W)
    return pair.transpose(0, 2, 1, 3, 4)


def _neighborhood_attention(qkv, rpb):
    bsz, t, _ = qkv.shape
    n_rows = t // GRID_W
    ng = NA_HEADS // NA_GROUP
    tq = NA_QROWS * GRID_W
    halo = (NA_ROWS // 2) * GRID_W
    per = tq // halo
    n_halo = t // halo
    tbl = _na_bias_table(rpb)

    def kv_specs(col0):
        return [
            pl.BlockSpec((None, halo, LANE), lambda g, b, i: (b, jnp.maximum(i * per - 1, 0), col0 + g)),
            pl.BlockSpec((None, tq, LANE), lambda g, b, i: (b, i, col0 + g)),
            pl.BlockSpec((None, halo, LANE), lambda g, b, i: (b, jnp.minimum((i + 1) * per, n_halo - 1), col0 + g)),
        ]

    cat_rows = tq + 2 * halo
    return pl.pallas_call(
        functools.partial(_na_kernel, n_rows=n_rows),
        grid=(ng, bsz, t // tq),
        in_specs=[pl.BlockSpec((None, tq, LANE), lambda g, b, i: (b, i, g))] + kv_specs(ng) + kv_specs(2 * ng)
        + [pl.BlockSpec((None, 2 * NA_ROWS - 2, NA_GROUP, GRID_W, 2 * GRID_W), lambda g, b, i: (g, 0, 0, 0, 0))],
        out_specs=pl.BlockSpec((None, tq, LANE), lambda g, b, i: (b, i, g)),
        out_shape=jax.ShapeDtypeStruct((bsz, t, NA_HEADS * NA_HEAD_DIM), BF16),
        scratch_shapes=[pltpu.VMEM((cat_rows, LANE), BF16), pltpu.VMEM((cat_rows, LANE), BF16)],
        compiler_params=_cparams(("parallel", "parallel", "parallel")),
        name="neighborhood_attention",
    )(qkv, *([qkv] * 6), tbl)


def _post_kernel(o_ref, w_ref, x_ref, g1_ref, n2_ref, sc_ref, sh_ref, wrh_ref, wrl_ref, xn_ref, hf_ref, lg_ref):
    y = jnp.dot(o_ref[...], w_ref[...], preferred_element_type=F32)
    xn = x_ref[...] + g1_ref[...] * y
    xn_ref[...] = xn
    hf = _rms(xn) * n2_ref[...] * (1.0 + sc_ref[...]) + sh_ref[...]
    hf_ref[...] = hf
    hh, hl = _split2(hf)
    wrh = wrh_ref[...]
    lg_ref[...] = (jnp.dot(hh, wrh, preferred_element_type=F32) + jnp.dot(hl, wrh, preferred_element_type=F32)
                   + jnp.dot(hh, wrl_ref[...], preferred_element_type=F32))


def _post_mixer(o, w_out, x, g1, n2g, sc2, sh2, wr_hi, wr_lo, *, tm):
    bsz, t, d = x.shape
    kdim = o.shape[-1]
    tm = min(tm, t)
    vec = pl.BlockSpec((None, 1, d), lambda b, i: (b, 0, 0))
    tile = pl.BlockSpec((None, tm, d), lambda b, i: (b, i, 0))
    return pl.pallas_call(
        _post_kernel,
        grid=(bsz, t // tm),
        in_specs=[
            pl.BlockSpec((None, tm, kdim), lambda b, i: (b, i, 0)),
            pl.BlockSpec((kdim, d), lambda b, i: (0, 0)),
            tile, vec,
            pl.BlockSpec((1, d), lambda b, i: (0, 0)),
            vec, vec,
            pl.BlockSpec((d, LANE), lambda b, i: (0, 0)),
            pl.BlockSpec((d, LANE), lambda b, i: (0, 0)),
        ],
        out_specs=[tile, tile, pl.BlockSpec((None, tm, LANE), lambda b, i: (b, i, 0))],
        out_shape=[jax.ShapeDtypeStruct((bsz, t, d), F32), jax.ShapeDtypeStruct((bsz, t, d), F32),
                   jax.ShapeDtypeStruct((bsz, t, LANE), F32)],
        compiler_params=_cparams(("parallel", "parallel")),
        name="post_mixer",
    )(o, w_out, x, g1, n2g.reshape(1, d), sc2, sh2, wr_hi, wr_lo)


ROUTER_ROWS = 8 + MOE_EXPERTS


def _router_kernel(lt_ref, b_ref, id_ref, w_ref):
    lt = lt_ref[...] + b_ref[...]
    tn = lt.shape[1]
    lg = lt[0:MOE_GROUPS]
    e = jnp.exp(lg - jnp.max(lg, axis=0, keepdims=True))
    gp = e / jnp.sum(e, axis=0, keepdims=True)
    g_p = jnp.max(gp, axis=0, keepdims=True)
    rg = lax.broadcasted_iota(jnp.int32, (MOE_GROUPS, tn), 0)
    g_idx = jnp.min(jnp.where(gp == g_p, rg, MOE_GROUPS), axis=0, keepdims=True)
    el = jnp.zeros((MOE_EPG, tn), F32)
    for g in range(MOE_GROUPS):
        el = jnp.where(g_idx == g, lt[8 + g * MOE_EPG:8 + (g + 1) * MOE_EPG], el)
    ee = jnp.exp(el - jnp.max(el, axis=0, keepdims=True))
    ep = ee / jnp.sum(ee, axis=0, keepdims=True)
    re = lax.broadcasted_iota(jnp.int32, (MOE_EPG, tn), 0)
    p1 = jnp.max(ep, axis=0, keepdims=True)
    i1 = jnp.min(jnp.where(ep == p1, re, MOE_EPG), axis=0, keepdims=True)
    ep2 = jnp.where(re == i1, -1.0, ep)
    p2 = jnp.max(ep2, axis=0, keepdims=True)
    i2 = jnp.min(jnp.where(ep2 == p2, re, MOE_EPG), axis=0, keepdims=True)
    den = p1 + p2
    id_ref[0:1, :] = g_idx * MOE_EPG + i1
    id_ref[1:2, :] = g_idx * MOE_EPG + i2
    w_ref[0:1, :] = g_p * (p1 / den)
    w_ref[1:2, :] = g_p * (p2 / den)


def _router(logits_t, bias_col, *, tn):
    n = logits_t.shape[1]
    tn = min(tn, n)
    return pl.pallas_call(
        _router_kernel,
        grid=(n // tn,),
        in_specs=[pl.BlockSpec((ROUTER_ROWS, tn), lambda i: (0, i)),
                  pl.BlockSpec((ROUTER_ROWS, 1), lambda i: (0, 0))],
        out_specs=[pl.BlockSpec((MOE_TOP_K, tn), lambda i: (0, i)), pl.BlockSpec((MOE_TOP_K, tn), lambda i: (0, i))],
        out_shape=[jax.ShapeDtypeStruct((MOE_TOP_K, n), jnp.int32), jax.ShapeDtypeStruct((MOE_TOP_K, n), F32)],
        compiler_params=_cparams(("parallel",)),
        name="router_topk",
    )(logits_t, bias_col)


MOE_DMA_UNROLL = 8


def _moe_kernel(pb_ref, pe_ref, r0_ref, r1_ref, fl_ref, np_ref, src_cur, src_nxt, dst_prv, dst_lst, hf_hbm,
                wgu_ref, wd_ref, y_hbm, wgu_bf, wd_bf, xbuf, obuf, gsem, ssem, *, n_blocks):
    s = pl.program_id(0)
    blk = xbuf.shape[1]
    ff = wd_ref.shape[0]
    b = pb_ref[s]
    slot = b % 2
    flags = fl_ref[s]
    active = s < np_ref[0]
    first = jnp.logical_and(active, (flags & 1) != 0)
    new_w = jnp.logical_and(active, (flags & 4) != 0)

    def gather_copy(src_ref, sl, r):
        return pltpu.make_async_copy(hf_hbm.at[pl.ds(src_ref[0, r], 1)], xbuf.at[sl, pl.ds(r, 1)], gsem.at[sl])

    def scatter_copy(dst_ref, sl, r):
        return pltpu.make_async_copy(obuf.at[sl, pl.ds(r, 1)], y_hbm.at[pl.ds(dst_ref[0, r], 1)], ssem.at[sl])

    def issue_loop(make):
        def body(g, carry):
            for j in range(MOE_DMA_UNROLL):
                make(g * MOE_DMA_UNROLL + j).start(priority=j % 2)
            return carry
        lax.fori_loop(0, blk // MOE_DMA_UNROLL, body, 0)

    def wait_gather(sl):
        pltpu.make_async_copy(hf_hbm.at[pl.ds(0, blk)], xbuf.at[sl], gsem.at[sl]).wait()

    def wait_scatter(sl):
        pltpu.make_async_copy(obuf.at[sl], y_hbm.at[pl.ds(0, blk)], ssem.at[sl]).wait()

    @pl.when(s == 0)
    def _():
        obuf[...] = jnp.zeros(obuf.shape, F32)
        issue_loop(lambda r: gather_copy(src_cur, 0, r))

    @pl.when(first)
    def _():
        wait_gather(slot)

        @pl.when(b >= 2)
        def _():
            wait_scatter(slot)

    @pl.when(new_w)
    def _():
        wgu_bf[...] = wgu_ref[...].astype(BF16)
        wd_bf[...] = wd_ref[...].astype(BF16)

    @pl.when(active)
    def _():
        do_gather = jnp.logical_and(first, b + 1 < n_blocks)
        do_scatter = jnp.logical_and(first, b >= 1)
        for r in range(blk):
            @pl.when(do_gather)
            def _():
                gather_copy(src_nxt, 1 - slot, r).start(priority=r % 2)

            @pl.when(do_scatter)
            def _():
                scatter_copy(dst_prv, 1 - slot, r).start(priority=(r + 1) % 2)

        x = xbuf[slot].astype(BF16)
        hgu = jnp.dot(x, wgu_bf[...], preferred_element_type=F32)
        g = hgu[:, :ff]
        u = hgu[:, ff:]
        act = (g * jax.nn.sigmoid(g) * u).astype(BF16)
        y = jnp.dot(act, wd_bf[...], preferred_element_type=F32)
        ri = lax.broadcasted_iota(jnp.int32, (blk, 1), 0)
        mine = jnp.logical_and(ri >= r0_ref[s], ri < r1_ref[s])
        obuf[slot] = jnp.where(mine, y, obuf[slot])

    @pl.when(s == pl.num_programs(0) - 1)
    def _():
        last_slot = (n_blocks - 1) % 2
        issue_loop(lambda r: scatter_copy(dst_lst, last_slot, r))
        wait_scatter(0)
        wait_scatter(1)


def _moe_plan(expert_id, blk):
    flat_e = expert_id.reshape(-1)
    n_asg = flat_e.shape[0]
    n_blocks = n_asg // blk
    n_steps = n_blocks + MOE_EXPERTS - 1
    onehot = (flat_e[:, None] == jnp.arange(MOE_EXPERTS, dtype=jnp.int32)[None, :]).astype(jnp.int32)
    csum = jnp.cumsum(onehot, axis=0)
    rank = jnp.sum(csum * onehot, axis=1) - 1
    counts = csum[-1]
    ends = jnp.cumsum(counts)
    starts = ends - counts
    slot = starts[flat_e] + rank
    asg = jnp.zeros((n_asg,), jnp.int32).at[slot].set(jnp.arange(n_asg, dtype=jnp.int32))
    src_tok = asg // MOE_TOP_K
    dst_row = (asg % MOE_TOP_K) * (n_asg // MOE_TOP_K) + asg // MOE_TOP_K
    bidx = jnp.arange(n_blocks, dtype=jnp.int32)
    e_lo = jnp.searchsorted(ends, bidx * blk, side='right').astype(jnp.int32)
    e_hi = jnp.searchsorted(ends, (bidx + 1) * blk - 1, side='right').astype(jnp.int32)
    n_pair_b = e_hi - e_lo + 1
    pair_end = jnp.cumsum(n_pair_b)
    pair_start = pair_end - n_pair_b
    n_pairs = pair_end[-1]
    sidx = jnp.arange(n_steps, dtype=jnp.int32)
    pb = jnp.minimum(jnp.searchsorted(pair_end, sidx, side='right').astype(jnp.int32), n_blocks - 1)
    pe = jnp.clip(e_lo[pb] + sidx - pair_start[pb], 0, MOE_EXPERTS - 1)
    pe = jnp.where(sidx < n_pairs, pe, pe[jnp.maximum(n_pairs - 1, 0)])
    r0 = jnp.clip(starts[pe] - pb * blk, 0, blk)
    r1 = jnp.clip(ends[pe] - pb * blk, 0, blk)
    prev_e = jnp.concatenate([jnp.full((1,), -1, jnp.int32), pe[:-1]])
    flags = ((sidx == pair_start[pb]).astype(jnp.int32) + 2 * (sidx == pair_end[pb] - 1).astype(jnp.int32)
             + 4 * (pe != prev_e).astype(jnp.int32))
    flags = jnp.where(sidx < n_pairs, flags, 0)
    i32 = lambda a: a.astype(jnp.int32)
    return (i32(pb), i32(pe), i32(r0), i32(r1), i32(flags), i32(n_pairs).reshape(1),
            src_tok.reshape(n_blocks, 1, blk), dst_row.reshape(n_blocks, 1, blk))


def _moe_experts(hf, expert_id, w_gu, w_d, layer, *, blk):
    n_tok, d = hf.shape
    n_asg = n_tok * MOE_TOP_K
    blk = min(blk, n_asg // 2)
    n_blocks = n_asg // blk
    pb, pe, r0, r1, flags, n_pairs, src_tok, dst_row = _moe_plan(expert_id, blk)
    ff = w_d.shape[2]
    smem = functools.partial(pl.BlockSpec, memory_space=pltpu.SMEM)
    grid_spec = pltpu.PrefetchScalarGridSpec(
        num_scalar_prefetch=6,
        grid=(pb.shape[0],),
        in_specs=[
            smem((None, 1, blk), lambda s, pb, *_: (pb[s], 0, 0)),
            smem((None, 1, blk), lambda s, pb, *_: (jnp.minimum(pb[s] + 1, n_blocks - 1), 0, 0)),
            smem((None, 1, blk), lambda s, pb, *_: (jnp.maximum(pb[s] - 1, 0), 0, 0)),
            smem((None, 1, blk), lambda s, pb, *_: (n_blocks - 1, 0, 0)),
            pl.BlockSpec(memory_space=pl.ANY),
            pl.BlockSpec((None, None, d, 2 * ff), lambda s, pb, pe, *_: (layer, pe[s], 0, 0)),
            pl.BlockSpec((None, None, ff, d), lambda s, pb, pe, *_: (layer, pe[s], 0, 0)),
        ],
        out_specs=pl.BlockSpec(memory_space=pl.ANY),
        scratch_shapes=[
            pltpu.VMEM((d, 2 * ff), BF16),
            pltpu.VMEM((ff, d), BF16),
            pltpu.VMEM((2, blk, d), F32),
            pltpu.VMEM((2, blk, d), F32),
            pltpu.SemaphoreType.DMA((2,)),
            pltpu.SemaphoreType.DMA((2,)),
        ],
    )
    return pl.pallas_call(
        functools.partial(_moe_kernel, n_blocks=n_blocks),
        grid_spec=grid_spec,
        out_shape=jax.ShapeDtypeStruct((n_asg, d), F32),
        compiler_params=_cparams(("arbitrary",)),
        name="moe_experts",
    )(pb, pe, r0, r1, flags, n_pairs, src_tok, src_tok, dst_row, dst_row, hf, w_gu, w_d)


def _combine_kernel(x_ref, y0_ref, y1_ref, w_ref, g2_ref, fg_ref, o_ref, *, final):
    w = w_ref[...]
    moe = w[:, 0:1] * y0_ref[...] + w[:, 1:2] * y1_ref[...]
    xn = x_ref[...] + g2_ref[...] * moe
    if final:
        xn = _rms(xn) * fg_ref[...]
    o_ref[...] = xn


def _combine(x, ybuf, wts, g2, final_g, *, final, tm):
    bsz, t, d = x.shape
    tm = min(tm, t)
    y2 = ybuf.reshape(MOE_TOP_K, bsz * t, d)
    nt = t // tm
    return pl.pallas_call(
        functools.partial(_combine_kernel, final=final),
        grid=(bsz, nt),
        in_specs=[
            pl.BlockSpec((None, tm, d), lambda b, i: (b, i, 0)),
            pl.BlockSpec((None, tm, d), lambda b, i: (0, b * nt + i, 0)),
            pl.BlockSpec((None, tm, d), lambda b, i: (1, b * nt + i, 0)),
            pl.BlockSpec((tm, MOE_TOP_K), lambda b, i: (b * nt + i, 0)),
            pl.BlockSpec((None, 1, d), lambda b, i: (b, 0, 0)),
            pl.BlockSpec((1, d), lambda b, i: (0, 0)),
        ],
        out_specs=pl.BlockSpec((None, tm, d), lambda b, i: (b, i, 0)),
        out_shape=jax.ShapeDtypeStruct((bsz, t, d), F32),
        compiler_params=_cparams(("parallel", "parallel")),
        name="moe_combine_final" if final else "moe_combine",
    )(x, y2, y2, wts, g2, final_g.reshape(1, d))


def _router_weights(rg_w, rg_b, re_w, re_b):
    d = rg_w.shape[0]
    wr = jnp.zeros((d, LANE), F32).at[:, 0:MOE_GROUPS].set(rg_w).at[:, 8:8 + MOE_EXPERTS].set(re_w)
    bias = jnp.zeros((ROUTER_ROWS, 1), F32).at[0:MOE_GROUPS, 0].set(rg_b).at[8:, 0].set(re_b)
    return tuple(_bf16_parts(wr, 2)) + (bias,)


def _col_scale(n, n_scaled, scale):
    return jnp.concatenate([jnp.full((n_scaled,), scale, F32), jnp.ones((n - n_scaled,), F32)])


def kernel(x, c, ada_w, ada_b, norm1_g, norm2_g, router_g_w, router_g_b, router_e_w, router_e_b, moe_w_gu, moe_w_d, da_w_in, da_w_out, da_lam_q1, da_lam_k1, da_lam_q2, da_lam_k2, da_subln_g, gla_w_in, gla_gate_w2_f, gla_gate_b_f, gla_gate_w2_b, gla_gate_b_b, gla_norm_g, gla_w_out, na_w_in, na_rpb, na_w_out, final_g):
    bsz, t, d = x.shape
    n_tok = bsz * t
    mod = _ada_mod(c, ada_w, ada_b)
    wts = ybuf = g2 = None
    for i in range(DEPTH):
        sh1, sc1, g1, sh2, sc2, g2_i = [mod[i, :, None, m * d:(m + 1) * d] for m in range(6)]
        if i > 0:
            x = _combine(x, ybuf, wts, g2, final_g, final=False, tm=512)
        kind, j = i % N_MIXERS, i // N_MIXERS
        if kind == 0:
            w_in = da_w_in[j].astype(BF16)
            cs = _col_scale(w_in.shape[1], DA_HEADS * DA_VAL_DIM, DA_HEAD_DIM ** -0.5 * LOG2E)
            qkv = _norm_proj(x, norm1_g[i], sc1, sh1, w_in, cs, tm=1024, tn=1024, out_dtype=BF16)
            lam_vecs = jnp.stack([da_lam_q1[j], da_lam_k1[j], da_lam_q2[j], da_lam_k2[j]]).astype(F32)
            lam_init = 0.8 - 0.6 * math.exp(-0.3 * i)
            o = _diff_attention(qkv, lam_vecs, da_subln_g[j], lam_init, tile=1024)
            w_out = da_w_out[j]
        elif kind == 1:
            n_main = 2 * GLA_HEADS * GLA_DK + 2 * GLA_HEADS * GLA_DV
            w_main = gla_w_in[j][:, :n_main].astype(BF16)
            w_gate = jnp.zeros((d, LANE), F32).at[:, :2 * GLA_GATE_RANK].set(gla_w_in[j][:, n_main:]).astype(BF16)
            cs = _col_scale(n_main, GLA_HEADS * GLA_DK, GLA_DK ** -0.5)
            qkvr = _norm_proj(x, norm1_g[i], sc1, sh1, w_main, cs, tm=1024, tn=1024, out_dtype=BF16)
            g_lr = _norm_proj(x, norm1_g[i], sc1, sh1, w_gate, jnp.ones((LANE,), F32), tm=1024, tn=LANE,
                              out_dtype=F32)
            outs = []
            for rev, w2, gb in ((False, gla_gate_w2_f[j], gla_gate_b_f[j]), (True, gla_gate_w2_b[j], gla_gate_b_b[j])):
                r0 = GLA_GATE_RANK if rev else 0
                w2p = jnp.zeros((LANE, w2.shape[1]), F32).at[r0:r0 + GLA_GATE_RANK].set(w2)
                w2h, w2l = _bf16_parts(w2p, 2)
                outs.append(_gla_scan(qkvr, g_lr, w2h, w2l, gb.reshape(1, -1).astype(F32), reverse=rev, blk=512))
            o = _gla_finish(outs[0], outs[1], qkvr, gla_norm_g[j], tm=512)
            w_out = gla_w_out[j]
        else:
            w_in = na_w_in[j].astype(BF16)
            cs = _col_scale(w_in.shape[1], NA_HEADS * NA_HEAD_DIM, NA_HEAD_DIM ** -0.5 * LOG2E)
            qkv = _norm_proj(x, norm1_g[i], sc1, sh1, w_in, cs, tm=1024, tn=1024, out_dtype=BF16)
            o = _neighborhood_attention(qkv, na_rpb[j])
            w_out = na_w_out[j]
        wr_hi, wr_lo, r_bias = _router_weights(router_g_w[i], router_g_b[i], router_e_w[i], router_e_b[i])
        x, hf, logits = _post_mixer(o, w_out.astype(BF16), x, g1, norm2_g[i], sc2, sh2, wr_hi, wr_lo, tm=256)
        logits_t = logits.reshape(n_tok, LANE).T[:ROUTER_ROWS]
        ids, wt = _router(logits_t, r_bias, tn=2048)
        ybuf = _moe_experts(hf.reshape(n_tok, d), ids.T, moe_w_gu, moe_w_d, i, blk=MOE_BLK)
        wts = wt.T
        g2 = g2_i
    return _combine(x, ybuf, wts, g2, final_g, final=True, tm=512)
```

```python
import functools
import math

import numpy as np
import jax
import jax.numpy as jnp
from jax import lax
from jax.experimental import pallas as pl
from jax.experimental.pallas import tpu as pltpu

F32 = jnp.float32
BF16 = jnp.bfloat16

D_MODEL = 2048
DEPTH = 4
N_MIXERS = 3
NORM_EPS = 1e-6
NEG_INF = -1e30
LOG2E = 1.4426950408889634

DA_HEADS = 8
DA_HEAD_DIM = 128
DA_VAL_DIM = 2 * DA_HEAD_DIM
DA_ROW_BLK = 16

GLA_HEADS = 4
GLA_DK = 256
GLA_DV = 512
GLA_GATE_RANK = 16
GLA_TAU = 16.0
GLA_CHUNK = 64

GRID_W = 64
NA_HEADS = 64
NA_HEAD_DIM = 32
NA_ROWS = 8
NA_COLS = 16
NA_GROUP = 4
NA_QROWS = 32

MOE_GROUPS = 4
MOE_EPG = 8
MOE_EXPERTS = 32
MOE_TOP_K = 2
MOE_D_FF = 512
MOE_BLK = 256

LANE = 128
VMEM_LIMIT = 56 * 1024 * 1024


def _cparams(sem):
    return pltpu.CompilerParams(dimension_semantics=sem, vmem_limit_bytes=VMEM_LIMIT)


def _split2(v):
    hi = v.astype(BF16)
    lo = (v - hi.astype(F32)).astype(BF16)
    return hi, lo


def _split3(v):
    hi = v.astype(BF16)
    r = v - hi.astype(F32)
    mid = r.astype(BF16)
    lo = (r - mid.astype(F32)).astype(BF16)
    return hi, mid, lo


def _bf16_parts(v, n):
    parts = []
    r = v.astype(F32)
    for _ in range(n):
        top = lax.bitcast_convert_type(
            lax.bitcast_convert_type(r, jnp.uint32) & jnp.uint32(0xFFFF0000), F32)
        parts.append(top.astype(BF16))
        r = r - top
    return parts


def _nt_dot(a, b):
    return lax.dot_general(a, b, (((1,), (1,)), ((), ())), preferred_element_type=F32)


def _tn_dot(a, b):
    return lax.dot_general(a, b, (((0,), (0,)), ((), ())), preferred_element_type=F32)


def _rms(x):
    return x * lax.rsqrt(jnp.mean(x * x, axis=-1, keepdims=True) + NORM_EPS)


def _ada_kernel(c_ref, w_ref, b_ref, o_ref):
    c = c_ref[...]
    cond = (c * jax.nn.sigmoid(c)).astype(BF16)
    o_ref[...] = jnp.dot(cond, w_ref[...].astype(BF16), preferred_element_type=F32) + b_ref[...]


def _ada_mod(c, ada_w, ada_b):
    depth, d, n = ada_w.shape
    bsz = c.shape[0]
    rows = 8
    c8 = jnp.zeros((rows, d), F32).at[:bsz].set(c)
    tn = 1024
    out = pl.pallas_call(
        _ada_kernel,
        grid=(depth, n // tn),
        in_specs=[
            pl.BlockSpec((rows, d), lambda l, j: (0, 0)),
            pl.BlockSpec((None, d, tn), lambda l, j: (l, 0, j)),
            pl.BlockSpec((None, 1, tn), lambda l, j: (l, 0, j)),
        ],
        out_specs=pl.BlockSpec((None, rows, tn), lambda l, j: (l, 0, j)),
        out_shape=jax.ShapeDtypeStruct((depth, rows, n), F32),
        compiler_params=_cparams(("parallel", "parallel")),
        name="ada_mod",
    )(c8, ada_w, ada_b.reshape(depth, 1, n))
    return out[:, :bsz]


def _proj_kernel(x_ref, g_ref, sc_ref, sh_ref, w_ref, cs_ref, o_ref, h_scr):
    @pl.when(pl.program_id(2) == 0)
    def _():
        h = _rms(x_ref[...]) * g_ref[...] * (1.0 + sc_ref[...]) + sh_ref[...]
        h_scr[...] = h.astype(BF16)

    acc = jnp.dot(h_scr[...], w_ref[...], preferred_element_type=F32)
    o_ref[...] = (acc * cs_ref[...]).astype(o_ref.dtype)


def _norm_proj(x, g, sc, sh, w, col_scale, *, tm, tn, out_dtype):
    bsz, t, d = x.shape
    n = w.shape[1]
    tm = min(tm, t)
    return pl.pallas_call(
        _proj_kernel,
        grid=(bsz, t // tm, n // tn),
        in_specs=[
            pl.BlockSpec((None, tm, d), lambda b, i, j: (b, i, 0)),
            pl.BlockSpec((1, d), lambda b, i, j: (0, 0)),
            pl.BlockSpec((None, 1, d), lambda b, i, j: (b, 0, 0)),
            pl.BlockSpec((None, 1, d), lambda b, i, j: (b, 0, 0)),
            pl.BlockSpec((d, tn), lambda b, i, j: (0, j)),
            pl.BlockSpec((1, tn), lambda b, i, j: (0, j)),
        ],
        out_specs=pl.BlockSpec((None, tm, tn), lambda b, i, j: (b, i, j)),
        out_shape=jax.ShapeDtypeStruct((bsz, t, n), out_dtype),
        scratch_shapes=[pltpu.VMEM((tm, d), BF16)],
        compiler_params=_cparams(("parallel", "parallel", "arbitrary")),
        name="norm_proj",
    )(x, g.reshape(1, d), sc, sh, w, col_scale.reshape(1, n))


def _da_kernel(c_ref, skip_ref, kidx_ref, q_ref, k_ref, v_ref, qf_ref, kf_ref, u_ref, lam_ref, g_ref, o_ref,
               qa_scr, m_scr, l_scr, acc_scr, s_scr, p_scr, *, lam_init):
    h = pl.program_id(1)
    qi = pl.program_id(2)
    kb = pl.program_id(3)
    nk = pl.num_programs(3)
    hd = DA_HEAD_DIM
    n_split, hr, tk = s_scr.shape[0] // 2, s_scr.shape[1], s_scr.shape[2]
    kt = lax.rem(qi + kb, nk)
    flat = ((pl.program_id(0) * pl.num_programs(1) + h) * pl.num_programs(2) + qi) * nk + kb

    @pl.when(kb == 0)
    def _():
        m_scr[...] = jnp.full(m_scr.shape, -jnp.inf, F32)
        l_scr[...] = jnp.zeros(l_scr.shape, F32)
        acc_scr[...] = jnp.zeros(acc_scr.shape, F32)
        q = q_ref[...]
        qf = qf_ref[...]
        for m in range(2):
            qm = q[:, m * hd:(m + 1) * hd]
            qa_scr[m] = jnp.concatenate([qm, qf], axis=1)

    def step(diag):
        sgn = jnp.where(kt > qi, -1.0, 1.0).astype(BF16)
        k = k_ref[...]
        kf = kf_ref[...] * sgn
        v = v_ref[...]
        c2 = 2.0 * c_ref[h]
        for m in range(2):
            ka = jnp.concatenate([k[:, m * hd:(m + 1) * hd], kf], axis=1)
            for sp in range(n_split):
                s_scr[m * n_split + sp] = _nt_dot(qa_scr[m, sp * hr:(sp + 1) * hr, :], ka)
        nlt = tk // LANE
        blocks = [slice(rb * DA_ROW_BLK, (rb + 1) * DA_ROW_BLK) for rb in range(hr // DA_ROW_BLK)]
        for m in range(2):
            for sp in range(n_split):
                ci = m * n_split + sp
                rs = slice(sp * hr, (sp + 1) * hr)
                pmax = []
                for loc in blocks:
                    s = s_scr[ci, loc, :]
                    if diag:
                        s = s - c2 * u_ref[sp * hr + loc.start:sp * hr + loc.stop, :]
                        s_scr[ci, loc, :] = s
                    pm = s[:, 0:LANE]
                    for lt in range(1, nlt):
                        pm = jnp.maximum(pm, s[:, lt * LANE:(lt + 1) * LANE])
                    pmax.append(pm)
                m_old = m_scr[m, rs, :]
                m_new = jnp.maximum(m_old, jnp.max(jnp.concatenate(pmax, axis=0), axis=1, keepdims=True))
                alpha = jnp.exp2(m_old - m_new)
                m_scr[m, rs, :] = m_new
                psum = []
                for loc in blocks:
                    s = s_scr[ci, loc, :]
                    mb = m_new[loc]
                    acc_l = None
                    for lt in range(nlt):
                        p = jnp.exp2(s[:, lt * LANE:(lt + 1) * LANE] - mb)
                        acc_l = p if acc_l is None else acc_l + p
                        p_scr[ci, loc, lt * LANE:(lt + 1) * LANE] = p.astype(BF16)
                    psum.append(acc_l)
                row_sum = jnp.sum(jnp.concatenate(psum, axis=0), axis=1, keepdims=True)
                l_scr[m, rs, :] = alpha * l_scr[m, rs, :] + row_sum
                pv = jnp.dot(p_scr[ci], v, preferred_element_type=F32)
                acc_scr[m, rs, :] = jnp.concatenate([alpha, alpha], axis=1) * acc_scr[m, rs, :] + pv

    @pl.when(kb == 0)
    def _():
        step(True)

    @pl.when(jnp.logical_and(kb > 0, skip_ref[flat] == 0))
    def _():
        step(False)

    @pl.when(kb == nk - 1)
    def _():
        lv = lam_ref[...]
        lam = (jnp.exp(jnp.sum(lv[0:1] * lv[1:2], axis=1, keepdims=True))
               - jnp.exp(jnp.sum(lv[2:3] * lv[3:4], axis=1, keepdims=True)) + lam_init)
        inv = [1.0 / l_scr[m] for m in range(2)]
        o = (acc_scr[0] * jnp.concatenate([inv[0], inv[0]], axis=1)
             - lam * (acc_scr[1] * jnp.concatenate([inv[1], inv[1]], axis=1)))
        o = _rms(o) * g_ref[...] * (1.0 - lam_init)
        o_ref[...] = o.astype(o_ref.dtype)


def _alibi_features(t, n_heads):
    slopes = 2.0 ** (-8.0 * np.arange(1, n_heads + 1) / n_heads)
    c = jnp.asarray(slopes * LOG2E, F32)
    cp = c[:, None] * jnp.arange(t, dtype=F32)[None, :]
    p3 = jnp.stack(_bf16_parts(cp, 3), axis=-1)
    ones = jnp.ones((n_heads, t, 3), BF16)
    pad = jnp.zeros((n_heads, t, LANE - 6), BF16)
    qf = jnp.concatenate([-p3, ones, pad], axis=-1)
    kf = jnp.concatenate([ones, p3, pad], axis=-1)
    return c, qf, kf


DA_SKIP_MARGIN = 200.0


def _norms_kernel(x_ref, o_ref):
    x = x_ref[...].astype(F32)
    cols = []
    for g in range(x.shape[1] // LANE):
        v = x[:, g * LANE:(g + 1) * LANE]
        cols.append(jnp.max(jnp.sum(v * v, axis=1, keepdims=True), axis=0, keepdims=True))
    o_ref[...] = jnp.concatenate(cols, axis=1)


def _da_tile_norms(qkv, tile):
    bsz, t, _ = qkv.shape
    width = 2 * DA_HEADS * DA_VAL_DIM
    return pl.pallas_call(
        _norms_kernel,
        grid=(bsz, t // tile),
        in_specs=[pl.BlockSpec((None, tile, width), lambda b, i: (b, i, 0))],
        out_specs=pl.BlockSpec((None, None, 1, width // LANE), lambda b, i: (b, i, 0, 0)),
        out_shape=jax.ShapeDtypeStruct((bsz, t // tile, 1, width // LANE), F32),
        compiler_params=_cparams(("parallel", "parallel")),
        name="da_tile_norms",
    )(qkv)


def _da_skip_plan(norms2, c, tile):
    bsz, nt = norms2.shape[0], norms2.shape[1]
    nh = DA_HEADS
    nrm = jnp.sqrt(norms2.reshape(bsz, nt, 2, nh, 2)) * 1.001
    nq, nkk = nrm[:, :, 0], nrm[:, :, 1]
    qi = jnp.arange(nt)[:, None]
    j = jnp.arange(nt)[None, :]
    kt = (qi + j) % nt
    bound = jnp.max(nq[:, :, None] * (nkk[:, kt] + nkk[:, :, None]), axis=-1)
    dist_min = jnp.maximum(jnp.abs(kt - qi) - 1, 0) * tile + 1
    far = c[None, None, None, :] * dist_min[None, :, :, None].astype(F32) >= DA_SKIP_MARGIN + bound
    skip = jnp.logical_and(far, (j > 0)[None, :, :, None]).transpose(0, 3, 1, 2)
    kidx = jnp.broadcast_to(kt[None, None], skip.shape)
    cols = [kidx[..., 0]]
    for jj in range(1, nt):
        cols.append(jnp.where(skip[..., jj], cols[-1], kidx[..., jj]))
    kidx = jnp.stack(cols, axis=-1)
    return skip.astype(jnp.int32).reshape(-1), kidx.astype(jnp.int32).reshape(-1)


def _diff_attention(qkv, lam_vecs, subln_g, lam_init, *, tile):
    bsz, t, _ = qkv.shape
    nh = DA_HEADS
    tq = tk = min(tile, t)
    nq = t // tq
    n_split = 2 if tq >= 4 * DA_ROW_BLK else 1
    c, qf, kf = _alibi_features(t, nh)
    skip, kidx = _da_skip_plan(_da_tile_norms(qkv, tq), c, tq)
    ii = jnp.arange(tq, dtype=F32)
    u = jnp.maximum(ii[None, :] - ii[:, None], 0.0)

    def key_tile(b, h, i, j, kidx_ref):
        return kidx_ref[((b * nh + h) * nq + i) * nq + j]

    grid_spec = pltpu.PrefetchScalarGridSpec(
        num_scalar_prefetch=3,
        grid=(bsz, nh, nq, nq),
        in_specs=[
            pl.BlockSpec((None, tq, DA_VAL_DIM), lambda b, h, i, j, c, sk, ki: (b, i, h)),
            pl.BlockSpec((None, tk, DA_VAL_DIM), lambda b, h, i, j, c, sk, ki: (b, key_tile(b, h, i, j, ki), nh + h)),
            pl.BlockSpec((None, tk, DA_VAL_DIM),
                         lambda b, h, i, j, c, sk, ki: (b, key_tile(b, h, i, j, ki), 2 * nh + h)),
            pl.BlockSpec((None, tq, LANE), lambda b, h, i, j, c, sk, ki: (h, i, 0)),
            pl.BlockSpec((None, tk, LANE), lambda b, h, i, j, c, sk, ki: (h, key_tile(b, h, i, j, ki), 0)),
            pl.BlockSpec((tq, tk), lambda b, h, i, j, c, sk, ki: (0, 0)),
            pl.BlockSpec((4, DA_HEAD_DIM), lambda b, h, i, j, c, sk, ki: (0, 0)),
            pl.BlockSpec((1, DA_VAL_DIM), lambda b, h, i, j, c, sk, ki: (0, 0)),
        ],
        out_specs=pl.BlockSpec((None, tq, DA_VAL_DIM), lambda b, h, i, j, c, sk, ki: (b, i, h)),
        scratch_shapes=[
            pltpu.VMEM((2, tq, 2 * DA_HEAD_DIM), BF16),
            pltpu.VMEM((2, tq, LANE), F32),
            pltpu.VMEM((2, tq, LANE), F32),
            pltpu.VMEM((2, tq, DA_VAL_DIM), F32),
            pltpu.VMEM((2 * n_split, tq // n_split, tk), F32),
            pltpu.VMEM((2 * n_split, tq // n_split, tk), BF16),
        ],
    )
    return pl.pallas_call(
        functools.partial(_da_kernel, lam_init=lam_init),
        grid_spec=grid_spec,
        out_shape=jax.ShapeDtypeStruct((bsz, t, nh * DA_VAL_DIM), BF16),
        compiler_params=_cparams(("parallel", "parallel", "parallel", "arbitrary")),
        name="diff_attention",
    )(c, skip, kidx, qkv, qkv, qkv, qf, kf, u, lam_vecs, subln_g.reshape(1, DA_VAL_DIM))


def _gla_kernel(q_ref, k_ref, v_ref, gl_ref, w2h_ref, w2l_ref, gb_ref, o_ref, st_scr, *, reverse, nchunk):
    cs = GLA_CHUNK
    blk = nchunk * cs

    @pl.when(pl.program_id(2) == 0)
    def _():
        st_scr[...] = jnp.zeros(st_scr.shape, F32)

    ri = lax.broadcasted_iota(jnp.int32, (blk, blk), 0)
    ci = lax.broadcasted_iota(jnp.int32, (blk, blk), 1)
    same_chunk = (ri // cs) == (ci // cs)
    keep = jnp.logical_and(same_chunk, (ci >= ri) if reverse else (ci <= ri))
    tri = jnp.where(keep, 1.0, 0.0).astype(BF16)
    w2h = w2h_ref[...]
    gh, glo = _split2(gl_ref[...])
    x = (jnp.dot(gh, w2h, preferred_element_type=F32) + jnp.dot(glo, w2h, preferred_element_type=F32)
         + jnp.dot(gh, w2l_ref[...], preferred_element_type=F32) + gb_ref[...])
    la = (jnp.minimum(x, 0.0) - jnp.log1p(jnp.exp(-jnp.abs(x)))) * (1.0 / GLA_TAU)
    a1, a2, a3 = _split3(la)
    b = (jnp.dot(tri, a1, preferred_element_type=F32) + jnp.dot(tri, a2, preferred_element_type=F32)
         + jnp.dot(tri, a3, preferred_element_type=F32))
    q = q_ref[...].astype(F32)
    k = k_ref[...].astype(F32)
    v = v_ref[...]
    q_in = (q * jnp.exp(b)).astype(BF16)
    k_in = (k * jnp.exp(-b)).astype(BF16)
    att = jnp.where(keep, _nt_dot(q_in, k_in), 0.0)
    o_intra = jnp.dot(att.astype(BF16), v, preferred_element_type=F32)
    chunks = [slice(c * cs, (c + 1) * cs) for c in range(nchunk)]
    tots = [b[r.start:r.start + 1] if reverse else b[r.stop - 1:r.stop] for r in chunks]
    kvs = [_tn_dot(v[r], (k[r] * jnp.exp(tot - b[r])).astype(BF16)) for r, tot in zip(chunks, tots)]
    st = st_scr[...]
    for c in (range(nchunk - 1, -1, -1) if reverse else range(nchunk)):
        r = chunks[c]
        o_ref[r, :] = o_intra[r] + _nt_dot(q_in[r], st.astype(BF16))
        st = st * jnp.exp(tots[c]) + kvs[c]
    st_scr[...] = st


def _gla_scan(qkvr, g_lr, w2h, w2l, gb, *, reverse, blk):
    bsz, t, _ = qkvr.shape
    nh = GLA_HEADS
    blk = min(blk, t)
    nblk = t // blk
    pos = (lambda i: nblk - 1 - i) if reverse else (lambda i: i)
    kq = (nh * GLA_DK) // GLA_DK
    kv = (2 * nh * GLA_DK) // GLA_DV
    return pl.pallas_call(
        functools.partial(_gla_kernel, reverse=reverse, nchunk=blk // GLA_CHUNK),
        grid=(bsz, nh, nblk),
        in_specs=[
            pl.BlockSpec((None, blk, GLA_DK), lambda b, h, i: (b, pos(i), h)),
            pl.BlockSpec((None, blk, GLA_DK), lambda b, h, i: (b, pos(i), kq + h)),
            pl.BlockSpec((None, blk, GLA_DV), lambda b, h, i: (b, pos(i), kv + h)),
            pl.BlockSpec((None, blk, LANE), lambda b, h, i: (b, pos(i), 0)),
            pl.BlockSpec((LANE, GLA_DK), lambda b, h, i: (0, h)),
            pl.BlockSpec((LANE, GLA_DK), lambda b, h, i: (0, h)),
            pl.BlockSpec((1, GLA_DK), lambda b, h, i: (0, h)),
        ],
        out_specs=pl.BlockSpec((None, blk, GLA_DV), lambda b, h, i: (b, pos(i), h)),
        out_shape=jax.ShapeDtypeStruct((bsz, t, nh * GLA_DV), F32),
        scratch_shapes=[pltpu.VMEM((GLA_DV, GLA_DK), F32)],
        compiler_params=_cparams(("parallel", "parallel", "arbitrary")),
        name="gla_scan_bwd" if reverse else "gla_scan_fwd",
    )(qkvr, qkvr, qkvr, g_lr, w2h, w2l, gb)


def _gla_fin_kernel(of_ref, ob_ref, r_ref, g_ref, o_ref):
    o = _rms(of_ref[...] + ob_ref[...]) * g_ref[...]
    r = r_ref[...].astype(F32)
    o_ref[...] = (o * (r * jax.nn.sigmoid(r))).astype(o_ref.dtype)


def _gla_finish(o_f, o_b, qkvr, norm_g, *, tm):
    bsz, t, _ = o_f.shape
    nh = GLA_HEADS
    tm = min(tm, t)
    kr = (2 * nh * GLA_DK + nh * GLA_DV) // GLA_DV
    return pl.pallas_call(
        _gla_fin_kernel,
        grid=(bsz, t // tm, nh),
        in_specs=[
            pl.BlockSpec((None, tm, GLA_DV), lambda b, i, h: (b, i, h)),
            pl.BlockSpec((None, tm, GLA_DV), lambda b, i, h: (b, i, h)),
            pl.BlockSpec((None, tm, GLA_DV), lambda b, i, h: (b, i, kr + h)),
            pl.BlockSpec((1, GLA_DV), lambda b, i, h: (0, 0)),
        ],
        out_specs=pl.BlockSpec((None, tm, GLA_DV), lambda b, i, h: (b, i, h)),
        out_shape=jax.ShapeDtypeStruct((bsz, t, nh * GLA_DV), BF16),
        compiler_params=_cparams(("parallel", "parallel", "parallel")),
        name="gla_finish",
    )(o_f, o_b, qkvr, norm_g.reshape(1, GLA_DV))


def _na_kernel(q_ref, kp, km, kn, vp, vm, vn, tbl_ref, o_ref, kcat, vcat, *, n_rows):
    i = pl.program_id(2)
    w = GRID_W
    halo = (NA_ROWS // 2) * w
    main = NA_QROWS * w
    for ref_p, ref_m, ref_n, cat in ((kp, km, kn, kcat), (vp, vm, vn, vcat)):
        cat[0:halo, :] = ref_p[...]
        cat[halo:halo + main, :] = ref_m[...]
        cat[halo + main:2 * halo + main, :] = ref_n[...]
    lane_head = lax.broadcasted_iota(jnp.int32, (w, LANE), 1) // NA_HEAD_DIM
    base = i * NA_QROWS
    for rr in range(NA_QROWS):
        r = base + rr
        r0 = jnp.clip(r - NA_ROWS // 2, 0, n_rows - NA_ROWS)
        start = pl.multiple_of((NA_ROWS // 2 + r0 - base) * w, w)
        delta = r - r0
        qr = q_ref[rr * w:(rr + 1) * w, :]
        qs = jnp.concatenate([jnp.where(lane_head == hh, qr, jnp.zeros_like(qr)) for hh in range(NA_GROUP)], axis=0)
        kb = kcat[pl.ds(start, NA_ROWS * w), :]
        vb = vcat[pl.ds(start, NA_ROWS * w), :]
        s = _nt_dot(qs, kb)
        bias = jnp.concatenate(
            [tbl_ref[2 * j - delta + NA_ROWS - 1].reshape(NA_GROUP * w, 2 * w) for j in range(NA_ROWS // 2)], axis=1)
        s = s + bias
        p = jnp.exp2(s - jnp.max(s, axis=1, keepdims=True))
        l = jnp.sum(p, axis=1, keepdims=True)
        o4 = jnp.dot(p.astype(BF16), vb, preferred_element_type=F32) / l
        o = jnp.zeros((w, LANE), F32)
        for hh in range(NA_GROUP):
            o = jnp.where(lane_head == hh, o4[hh * w:(hh + 1) * w, :], o)
        o_ref[rr * w:(rr + 1) * w, :] = o.astype(o_ref.dtype)


def _na_bias_table(rpb):
    col = np.arange(GRID_W)
    col_start = np.clip(col - NA_COLS // 2, 0, GRID_W - NA_COLS)
    in_window = (col[None, :] >= col_start[:, None]) & (col[None, :] < col_start[:, None] + NA_COLS)
    col_off = np.clip(col[None, :] - col[:, None] + NA_COLS - 1, 0, 2 * NA_COLS - 2)
    cb = jnp.where(in_window[None, None], rpb.astype(F32)[:, :, col_off] * LOG2E, NEG_INF)
    pair = jnp.concatenate([cb[:, :-1], cb[:, 1:]], axis=-1)
    n_pair = 2 * NA_ROWS - 2
    pair = pair.reshape(NA_HEADS // NA_GROUP, NA_GROUP, n_pair, GRID_W, 2 * GRID_W)
    return pair.transpose(0, 2, 1, 3, 4)


def _neighborhood_attention(qkv, rpb):
    bsz, t, _ = qkv.shape
    n_rows = t // GRID_W
    ng = NA_HEADS // NA_GROUP
    tq = NA_QROWS * GRID_W
    halo = (NA_ROWS // 2) * GRID_W
    per = tq // halo
    n_halo = t // halo
    tbl = _na_bias_table(rpb)

    def kv_specs(col0):
        return [
            pl.BlockSpec((None, halo, LANE), lambda g, b, i: (b, jnp.maximum(i * per - 1, 0), col0 + g)),
            pl.BlockSpec((None, tq, LANE), lambda g, b, i: (b, i, col0 + g)),
            pl.BlockSpec((None, halo, LANE), lambda g, b, i: (b, jnp.minimum((i + 1) * per, n_halo - 1), col0 + g)),
        ]

    cat_rows = tq + 2 * halo
    return pl.pallas_call(
        functools.partial(_na_kernel, n_rows=n_rows),
        grid=(ng, bsz, t // tq),
        in_specs=[pl.BlockSpec((None, tq, LANE), lambda g, b, i: (b, i, g))] + kv_specs(ng) + kv_specs(2 * ng)
        + [pl.BlockSpec((None, 2 * NA_ROWS - 2, NA_GROUP, GRID_W, 2 * GRID_W), lambda g, b, i: (g, 0, 0, 0, 0))],
        out_specs=pl.BlockSpec((None, tq, LANE), lambda g, b, i: (b, i, g)),
        out_shape=jax.ShapeDtypeStruct((bsz, t, NA_HEADS * NA_HEAD_DIM), BF16),
        scratch_shapes=[pltpu.VMEM((cat_rows, LANE), BF16), pltpu.VMEM((cat_rows, LANE), BF16)],
        compiler_params=_cparams(("parallel", "parallel", "parallel")),
        name="neighborhood_attention",
    )(qkv, *([qkv] * 6), tbl)


def _post_kernel(o_ref, w_ref, x_ref, g1_ref, n2_ref, sc_ref, sh_ref, wrh_ref, wrl_ref, xn_ref, hf_ref, lg_ref):
    y = jnp.dot(o_ref[...], w_ref[...], preferred_element_type=F32)
    xn = x_ref[...] + g1_ref[...] * y
    xn_ref[...] = xn
    hf = _rms(xn) * n2_ref[...] * (1.0 + sc_ref[...]) + sh_ref[...]
    hf_ref[...] = hf
    hh, hl = _split2(hf)
    wrh = wrh_ref[...]
    lg_ref[...] = (jnp.dot(hh, wrh, preferred_element_type=F32) + jnp.dot(hl, wrh, preferred_element_type=F32)
                   + jnp.dot(hh, wrl_ref[...], preferred_element_type=F32))


def _post_mixer(o, w_out, x, g1, n2g, sc2, sh2, wr_hi, wr_lo, *, tm):
    bsz, t, d = x.shape
    kdim = o.shape[-1]
    tm = min(tm, t)
    vec = pl.BlockSpec((None, 1, d), lambda b, i: (b, 0, 0))
    tile = pl.BlockSpec((None, tm, d), lambda b, i: (b, i, 0))
    return pl.pallas_call(
        _post_kernel,
        grid=(bsz, t // tm),
        in_specs=[
            pl.BlockSpec((None, tm, kdim), lambda b, i: (b, i, 0)),
            pl.BlockSpec((kdim, d), lambda b, i: (0, 0), pipeline_mode=pl.Buffered(1)),
            tile, vec,
            pl.BlockSpec((1, d), lambda b, i: (0, 0)),
            vec, vec,
            pl.BlockSpec((d, LANE), lambda b, i: (0, 0)),
            pl.BlockSpec((d, LANE), lambda b, i: (0, 0)),
        ],
        out_specs=[tile, tile, pl.BlockSpec((None, tm, LANE), lambda b, i: (b, i, 0))],
        out_shape=[jax.ShapeDtypeStruct((bsz, t, d), F32), jax.ShapeDtypeStruct((bsz, t, d), F32),
                   jax.ShapeDtypeStruct((bsz, t, LANE), F32)],
        compiler_params=_cparams(("parallel", "parallel")),
        name="post_mixer",
    )(o, w_out, x, g1, n2g.reshape(1, d), sc2, sh2, wr_hi, wr_lo)


ROUTER_ROWS = 8 + MOE_EXPERTS


def _router_kernel(lt_ref, b_ref, id_ref, w_ref):
    lt = lt_ref[...] + b_ref[...]
    tn = lt.shape[1]
    lg = lt[0:MOE_GROUPS]
    e = jnp.exp(lg - jnp.max(lg, axis=0, keepdims=True))
    gp = e / jnp.sum(e, axis=0, keepdims=True)
    g_p = jnp.max(gp, axis=0, keepdims=True)
    rg = lax.broadcasted_iota(jnp.int32, (MOE_GROUPS, tn), 0)
    g_idx = jnp.min(jnp.where(gp == g_p, rg, MOE_GROUPS), axis=0, keepdims=True)
    el = jnp.zeros((MOE_EPG, tn), F32)
    for g in range(MOE_GROUPS):
        el = jnp.where(g_idx == g, lt[8 + g * MOE_EPG:8 + (g + 1) * MOE_EPG], el)
    ee = jnp.exp(el - jnp.max(el, axis=0, keepdims=True))
    ep = ee / jnp.sum(ee, axis=0, keepdims=True)
    re = lax.broadcasted_iota(jnp.int32, (MOE_EPG, tn), 0)
    p1 = jnp.max(ep, axis=0, keepdims=True)
    i1 = jnp.min(jnp.where(ep == p1, re, MOE_EPG), axis=0, keepdims=True)
    ep2 = jnp.where(re == i1, -1.0, ep)
    p2 = jnp.max(ep2, axis=0, keepdims=True)
    i2 = jnp.min(jnp.where(ep2 == p2, re, MOE_EPG), axis=0, keepdims=True)
    den = p1 + p2
    id_ref[0:1, :] = g_idx * MOE_EPG + i1
    id_ref[1:2, :] = g_idx * MOE_EPG + i2
    w_ref[0:1, :] = g_p * (p1 / den)
    w_ref[1:2, :] = g_p * (p2 / den)


def _router(logits_t, bias_col, *, tn):
    n = logits_t.shape[1]
    tn = min(tn, n)
    return pl.pallas_call(
        _router_kernel,
        grid=(n // tn,),
        in_specs=[pl.BlockSpec((ROUTER_ROWS, tn), lambda i: (0, i)),
                  pl.BlockSpec((ROUTER_ROWS, 1), lambda i: (0, 0))],
        out_specs=[pl.BlockSpec((MOE_TOP_K, tn), lambda i: (0, i)), pl.BlockSpec((MOE_TOP_K, tn), lambda i: (0, i))],
        out_shape=[jax.ShapeDtypeStruct((MOE_TOP_K, n), jnp.int32), jax.ShapeDtypeStruct((MOE_TOP_K, n), F32)],
        compiler_params=_cparams(("parallel",)),
        name="router_topk",
    )(logits_t, bias_col)


MOE_DMA_UNROLL = 8


def _moe_kernel(pb_ref, pe_ref, r0_ref, r1_ref, fl_ref, np_ref, src_cur, src_nxt, dst_prv, dst_lst, hf_hbm,
                wgu_ref, wd_ref, y_hbm, wgu_bf, wd_bf, xbuf, obuf, gsem, ssem, *, n_blocks):
    s = pl.program_id(0)
    blk = xbuf.shape[1]
    ff = wd_ref.shape[0]
    b = pb_ref[s]
    slot = b % 2
    flags = fl_ref[s]
    active = s < np_ref[0]
    first = jnp.logical_and(active, (flags & 1) != 0)
    new_w = jnp.logical_and(active, (flags & 4) != 0)

    def gather_copy(src_ref, sl, r):
        return pltpu.make_async_copy(hf_hbm.at[pl.ds(src_ref[0, r], 1)], xbuf.at[sl, pl.ds(r, 1)], gsem.at[sl])

    def scatter_copy(dst_ref, sl, r):
        return pltpu.make_async_copy(obuf.at[sl, pl.ds(r, 1)], y_hbm.at[pl.ds(dst_ref[0, r], 1)], ssem.at[sl])

    def issue_loop(make):
        def body(g, carry):
            for j in range(MOE_DMA_UNROLL):
                make(g * MOE_DMA_UNROLL + j).start(priority=j % 2)
            return carry
        lax.fori_loop(0, blk // MOE_DMA_UNROLL, body, 0)

    def wait_gather(sl):
        pltpu.make_async_copy(hf_hbm.at[pl.ds(0, blk)], xbuf.at[sl], gsem.at[sl]).wait()

    def wait_scatter(sl):
        pltpu.make_async_copy(obuf.at[sl], y_hbm.at[pl.ds(0, blk)], ssem.at[sl]).wait()

    @pl.when(s == 0)
    def _():
        obuf[...] = jnp.zeros(obuf.shape, F32)
        issue_loop(lambda r: gather_copy(src_cur, 0, r))

    @pl.when(first)
    def _():
        wait_gather(slot)

        @pl.when(b >= 2)
        def _():
            wait_scatter(slot)

    @pl.when(new_w)
    def _():
        wgu_bf[...] = wgu_ref[...].astype(BF16)
        wd_bf[...] = wd_ref[...].astype(BF16)

    @pl.when(active)
    def _():
        do_gather = jnp.logical_and(first, b + 1 < n_blocks)
        do_scatter = jnp.logical_and(first, b >= 1)
        for r in range(blk):
            @pl.when(do_gather)
            def _():
                gather_copy(src_nxt, 1 - slot, r).start(priority=r % 2)

            @pl.when(do_scatter)
            def _():
                scatter_copy(dst_prv, 1 - slot, r).start(priority=(r + 1) % 2)

        x = xbuf[slot].astype(BF16)
        hgu = jnp.dot(x, wgu_bf[...], preferred_element_type=F32)
        g = hgu[:, :ff]
        u = hgu[:, ff:]
        act = (g * jax.nn.sigmoid(g) * u).astype(BF16)
        y = jnp.dot(act, wd_bf[...], preferred_element_type=F32)
        ri = lax.broadcasted_iota(jnp.int32, (blk, 1), 0)
        mine = jnp.logical_and(ri >= r0_ref[s], ri < r1_ref[s])
        obuf[slot] = jnp.where(mine, y, obuf[slot])

    @pl.when(s == pl.num_programs(0) - 1)
    def _():
        last_slot = (n_blocks - 1) % 2
        issue_loop(lambda r: scatter_copy(dst_lst, last_slot, r))
        wait_scatter(0)
        wait_scatter(1)


def _moe_plan(expert_id, blk):
    flat_e = expert_id.reshape(-1)
    n_asg = flat_e.shape[0]
    n_blocks = n_asg // blk
    n_steps = n_blocks + MOE_EXPERTS - 1
    onehot = (flat_e[:, None] == jnp.arange(MOE_EXPERTS, dtype=jnp.int32)[None, :]).astype(jnp.int32)
    csum = jnp.cumsum(onehot, axis=0)
    rank = jnp.sum(csum * onehot, axis=1) - 1
    counts = csum[-1]
    ends = jnp.cumsum(counts)
    starts = ends - counts
    slot = starts[flat_e] + rank
    asg = jnp.zeros((n_asg,), jnp.int32).at[slot].set(jnp.arange(n_asg, dtype=jnp.int32))
    src_tok = asg // MOE_TOP_K
    dst_row = (asg % MOE_TOP_K) * (n_asg // MOE_TOP_K) + asg // MOE_TOP_K
    bidx = jnp.arange(n_blocks, dtype=jnp.int32)
    e_lo = jnp.searchsorted(ends, bidx * blk, side='right').astype(jnp.int32)
    e_hi = jnp.searchsorted(ends, (bidx + 1) * blk - 1, side='right').astype(jnp.int32)
    n_pair_b = e_hi - e_lo + 1
    pair_end = jnp.cumsum(n_pair_b)
    pair_start = pair_end - n_pair_b
    n_pairs = pair_end[-1]
    sidx = jnp.arange(n_steps, dtype=jnp.int32)
    pb = jnp.minimum(jnp.searchsorted(pair_end, sidx, side='right').astype(jnp.int32), n_blocks - 1)
    pe = jnp.clip(e_lo[pb] + sidx - pair_start[pb], 0, MOE_EXPERTS - 1)
    pe = jnp.where(sidx < n_pairs, pe, pe[jnp.maximum(n_pairs - 1, 0)])
    r0 = jnp.clip(starts[pe] - pb * blk, 0, blk)
    r1 = jnp.clip(ends[pe] - pb * blk, 0, blk)
    prev_e = jnp.concatenate([jnp.full((1,), -1, jnp.int32), pe[:-1]])
    flags = ((sidx == pair_start[pb]).astype(jnp.int32) + 2 * (sidx == pair_end[pb] - 1).astype(jnp.int32)
             + 4 * (pe != prev_e).astype(jnp.int32))
    flags = jnp.where(sidx < n_pairs, flags, 0)
    i32 = lambda a: a.astype(jnp.int32)
    return (i32(pb), i32(pe), i32(r0), i32(r1), i32(flags), i32(n_pairs).reshape(1),
            src_tok.reshape(n_blocks, 1, blk), dst_row.reshape(n_blocks, 1, blk))


def _moe_experts(hf, expert_id, w_gu, w_d, layer, *, blk):
    n_tok, d = hf.shape
    n_asg = n_tok * MOE_TOP_K
    blk = min(blk, n_asg // 2)
    n_blocks = n_asg // blk
    pb, pe, r0, r1, flags, n_pairs, src_tok, dst_row = _moe_plan(expert_id, blk)
    ff = w_d.shape[2]
    smem = functools.partial(pl.BlockSpec, memory_space=pltpu.SMEM)
    grid_spec = pltpu.PrefetchScalarGridSpec(
        num_scalar_prefetch=6,
        grid=(pb.shape[0],),
        in_specs=[
            smem((None, 1, blk), lambda s, pb, *_: (pb[s], 0, 0)),
            smem((None, 1, blk), lambda s, pb, *_: (jnp.minimum(pb[s] + 1, n_blocks - 1), 0, 0)),
            smem((None, 1, blk), lambda s, pb, *_: (jnp.maximum(pb[s] - 1, 0), 0, 0)),
            smem((None, 1, blk), lambda s, pb, *_: (n_blocks - 1, 0, 0)),
            pl.BlockSpec(memory_space=pl.ANY),
            pl.BlockSpec((None, None, d, 2 * ff), lambda s, pb, pe, *_: (layer, pe[s], 0, 0)),
            pl.BlockSpec((None, None, ff, d), lambda s, pb, pe, *_: (layer, pe[s], 0, 0)),
        ],
        out_specs=pl.BlockSpec(memory_space=pl.ANY),
        scratch_shapes=[
            pltpu.VMEM((d, 2 * ff), BF16),
            pltpu.VMEM((ff, d), BF16),
            pltpu.VMEM((2, blk, d), F32),
            pltpu.VMEM((2, blk, d), F32),
            pltpu.SemaphoreType.DMA((2,)),
            pltpu.SemaphoreType.DMA((2,)),
        ],
    )
    return pl.pallas_call(
        functools.partial(_moe_kernel, n_blocks=n_blocks),
        grid_spec=grid_spec,
        out_shape=jax.ShapeDtypeStruct((n_asg, d), F32),
        compiler_params=_cparams(("arbitrary",)),
        name="moe_experts",
    )(pb, pe, r0, r1, flags, n_pairs, src_tok, src_tok, dst_row, dst_row, hf, w_gu, w_d)


def _combine_kernel(x_ref, y0_ref, y1_ref, w_ref, g2_ref, fg_ref, o_ref, *, final):
    w = w_ref[...]
    moe = w[:, 0:1] * y0_ref[...] + w[:, 1:2] * y1_ref[...]
    xn = x_ref[...] + g2_ref[...] * moe
    if final:
        xn = _rms(xn) * fg_ref[...]
    o_ref[...] = xn


def _combine(x, ybuf, wts, g2, final_g, *, final, tm):
    bsz, t, d = x.shape
    tm = min(tm, t)
    y2 = ybuf.reshape(MOE_TOP_K, bsz * t, d)
    nt = t // tm
    return pl.pallas_call(
        functools.partial(_combine_kernel, final=final),
        grid=(bsz, nt),
        in_specs=[
            pl.BlockSpec((None, tm, d), lambda b, i: (b, i, 0)),
            pl.BlockSpec((None, tm, d), lambda b, i: (0, b * nt + i, 0)),
            pl.BlockSpec((None, tm, d), lambda b, i: (1, b * nt + i, 0)),
            pl.BlockSpec((tm, MOE_TOP_K), lambda b, i: (b * nt + i, 0)),
            pl.BlockSpec((None, 1, d), lambda b, i: (b, 0, 0)),
            pl.BlockSpec((1, d), lambda b, i: (0, 0)),
        ],
        out_specs=pl.BlockSpec((None, tm, d), lambda b, i: (b, i, 0)),
        out_shape=jax.ShapeDtypeStruct((bsz, t, d), F32),
        compiler_params=_cparams(("parallel", "parallel")),
        name="moe_combine_final" if final else "moe_combine",
    )(x, y2, y2, wts, g2, final_g.reshape(1, d))


def _router_weights(rg_w, rg_b, re_w, re_b):
    d = rg_w.shape[0]
    wr = jnp.zeros((d, LANE), F32).at[:, 0:MOE_GROUPS].set(rg_w).at[:, 8:8 + MOE_EXPERTS].set(re_w)
    bias = jnp.zeros((ROUTER_ROWS, 1), F32).at[0:MOE_GROUPS, 0].set(rg_b).at[8:, 0].set(re_b)
    return tuple(_bf16_parts(wr, 2)) + (bias,)


def _col_scale(n, n_scaled, scale):
    return jnp.concatenate([jnp.full((n_scaled,), scale, F32), jnp.ones((n - n_scaled,), F32)])


def kernel(x, c, ada_w, ada_b, norm1_g, norm2_g, router_g_w, router_g_b, router_e_w, router_e_b, moe_w_gu, moe_w_d, da_w_in, da_w_out, da_lam_q1, da_lam_k1, da_lam_q2, da_lam_k2, da_subln_g, gla_w_in, gla_gate_w2_f, gla_gate_b_f, gla_gate_w2_b, gla_gate_b_b, gla_norm_g, gla_w_out, na_w_in, na_rpb, na_w_out, final_g):
    bsz, t, d = x.shape
    n_tok = bsz * t
    mod = _ada_mod(c, ada_w, ada_b)
    wts = ybuf = g2 = None
    for i in range(DEPTH):
        sh1, sc1, g1, sh2, sc2, g2_i = [mod[i, :, None, m * d:(m + 1) * d] for m in range(6)]
        if i > 0:
            x = _combine(x, ybuf, wts, g2, final_g, final=False, tm=512)
        kind, j = i % N_MIXERS, i // N_MIXERS
        if kind == 0:
            w_in = da_w_in[j].astype(BF16)
            cs = _col_scale(w_in.shape[1], DA_HEADS * DA_VAL_DIM, DA_HEAD_DIM ** -0.5 * LOG2E)
            qkv = _norm_proj(x, norm1_g[i], sc1, sh1, w_in, cs, tm=1024, tn=1024, out_dtype=BF16)
            lam_vecs = jnp.stack([da_lam_q1[j], da_lam_k1[j], da_lam_q2[j], da_lam_k2[j]]).astype(F32)
            lam_init = 0.8 - 0.6 * math.exp(-0.3 * i)
            o = _diff_attention(qkv, lam_vecs, da_subln_g[j], lam_init, tile=1024)
            w_out = da_w_out[j]
        elif kind == 1:
            n_main = 2 * GLA_HEADS * GLA_DK + 2 * GLA_HEADS * GLA_DV
            w_main = gla_w_in[j][:, :n_main].astype(BF16)
            w_gate = jnp.zeros((d, LANE), F32).at[:, :2 * GLA_GATE_RANK].set(gla_w_in[j][:, n_main:]).astype(BF16)
            cs = _col_scale(n_main, GLA_HEADS * GLA_DK, GLA_DK ** -0.5)
            qkvr = _norm_proj(x, norm1_g[i], sc1, sh1, w_main, cs, tm=1024, tn=1024, out_dtype=BF16)
            g_lr = _norm_proj(x, norm1_g[i], sc1, sh1, w_gate, jnp.ones((LANE,), F32), tm=1024, tn=LANE,
                              out_dtype=F32)
            outs = []
            for rev, w2, gb in ((False, gla_gate_w2_f[j], gla_gate_b_f[j]), (True, gla_gate_w2_b[j], gla_gate_b_b[j])):
                r0 = GLA_GATE_RANK if rev else 0
                w2p = jnp.zeros((LANE, w2.shape[1]), F32).at[r0:r0 + GLA_GATE_RANK].set(w2)
                w2h, w2l = _bf16_parts(w2p, 2)
                outs.append(_gla_scan(qkvr, g_lr, w2h, w2l, gb.reshape(1, -1).astype(F32), reverse=rev, blk=512))
            o = _gla_finish(outs[0], outs[1], qkvr, gla_norm_g[j], tm=512)
            w_out = gla_w_out[j]
        else:
            w_in = na_w_in[j].astype(BF16)
            cs = _col_scale(w_in.shape[1], NA_HEADS * NA_HEAD_DIM, NA_HEAD_DIM ** -0.5 * LOG2E)
            qkv = _norm_proj(x, norm1_g[i], sc1, sh1, w_in, cs, tm=1024, tn=1024, out_dtype=BF16)
            o = _neighborhood_attention(qkv, na_rpb[j])
            w_out = na_w_out[j]
        wr_hi, wr_lo, r_bias = _router_weights(router_g_w[i], router_g_b[i], router_e_w[i], router_e_b[i])
        x, hf, logits = _post_mixer(o, w_out.astype(BF16), x, g1, norm2_g[i], sc2, sh2, wr_hi, wr_lo, tm=512)
        logits_t = logits.reshape(n_tok, LANE).T[:ROUTER_ROWS]
        ids, wt = _router(logits_t, r_bias, tn=2048)
        ybuf = _moe_experts(hf.reshape(n_tok, d), ids.T, moe_w_gu, moe_w_d, i, blk=MOE_BLK)
        wts = wt.T
        g2 = g2_i
    return _combine(x, ybuf, wts, g2, final_g, final=True, tm=512)
```

```python
import functools
import math

import numpy as np
import jax
import jax.numpy as jnp
from jax import lax
from jax.experimental import pallas as pl
from jax.experimental.pallas import tpu as pltpu

F32 = jnp.float32
BF16 = jnp.bfloat16

D_MODEL = 2048
DEPTH = 4
N_MIXERS = 3
NORM_EPS = 1e-6
NEG_INF = -1e30
LOG2E = 1.4426950408889634

DA_HEADS = 8
DA_HEAD_DIM = 128
DA_VAL_DIM = 2 * DA_HEAD_DIM
DA_ROW_BLK = 16

GLA_HEADS = 4
GLA_DK = 256
GLA_DV = 512
GLA_GATE_RANK = 16
GLA_TAU = 16.0
GLA_CHUNK = 64

GRID_W = 64
NA_HEADS = 64
NA_HEAD_DIM = 32
NA_ROWS = 8
NA_COLS = 16
NA_GROUP = 4
NA_QROWS = 32

MOE_GROUPS = 4
MOE_EPG = 8
MOE_EXPERTS = 32
MOE_TOP_K = 2
MOE_D_FF = 512
MOE_BLK = 256

LANE = 128
VMEM_LIMIT = 56 * 1024 * 1024


def _cparams(sem):
    return pltpu.CompilerParams(dimension_semantics=sem, vmem_limit_bytes=VMEM_LIMIT)


def _split2(v):
    hi = v.astype(BF16)
    lo = (v - hi.astype(F32)).astype(BF16)
    return hi, lo


def _split3(v):
    hi = v.astype(BF16)
    r = v - hi.astype(F32)
    mid = r.astype(BF16)
    lo = (r - mid.astype(F32)).astype(BF16)
    return hi, mid, lo


def _bf16_parts(v, n):
    parts = []
    r = v.astype(F32)
    for _ in range(n):
        top = lax.bitcast_convert_type(
            lax.bitcast_convert_type(r, jnp.uint32) & jnp.uint32(0xFFFF0000), F32)
        parts.append(top.astype(BF16))
        r = r - top
    return parts


def _nt_dot(a, b):
    return lax.dot_general(a, b, (((1,), (1,)), ((), ())), preferred_element_type=F32)


def _tn_dot(a, b):
    return lax.dot_general(a, b, (((0,), (0,)), ((), ())), preferred_element_type=F32)


def _rms(x):
    return x * lax.rsqrt(jnp.mean(x * x, axis=-1, keepdims=True) + NORM_EPS)


def _ada_kernel(c_ref, w_ref, b_ref, o_ref):
    c = c_ref[...]
    cond = (c * jax.nn.sigmoid(c)).astype(BF16)
    o_ref[...] = jnp.dot(cond, w_ref[...].astype(BF16), preferred_element_type=F32) + b_ref[...]


def _ada_mod(c, ada_w, ada_b):
    depth, d, n = ada_w.shape
    bsz = c.shape[0]
    rows = 8
    c8 = jnp.zeros((rows, d), F32).at[:bsz].set(c)
    tn = 1024
    out = pl.pallas_call(
        _ada_kernel,
        grid=(depth, n // tn),
        in_specs=[
            pl.BlockSpec((rows, d), lambda l, j: (0, 0)),
            pl.BlockSpec((None, d, tn), lambda l, j: (l, 0, j)),
            pl.BlockSpec((None, 1, tn), lambda l, j: (l, 0, j)),
        ],
        out_specs=pl.BlockSpec((None, rows, tn), lambda l, j: (l, 0, j)),
        out_shape=jax.ShapeDtypeStruct((depth, rows, n), F32),
        compiler_params=_cparams(("parallel", "parallel")),
        name="ada_mod",
    )(c8, ada_w, ada_b.reshape(depth, 1, n))
    return out[:, :bsz]


def _proj_kernel(x_ref, g_ref, sc_ref, sh_ref, w_ref, cs_ref, o_ref, h_scr):
    @pl.when(pl.program_id(2) == 0)
    def _():
        h = _rms(x_ref[...]) * g_ref[...] * (1.0 + sc_ref[...]) + sh_ref[...]
        h_scr[...] = h.astype(BF16)

    acc = jnp.dot(h_scr[...], w_ref[...], preferred_element_type=F32)
    o_ref[...] = (acc * cs_ref[...]).astype(o_ref.dtype)


def _norm_proj(x, g, sc, sh, w, col_scale, *, tm, tn, out_dtype):
    bsz, t, d = x.shape
    n = w.shape[1]
    tm = min(tm, t)
    return pl.pallas_call(
        _proj_kernel,
        grid=(bsz, t // tm, n // tn),
        in_specs=[
            pl.BlockSpec((None, tm, d), lambda b, i, j: (b, i, 0)),
            pl.BlockSpec((1, d), lambda b, i, j: (0, 0)),
            pl.BlockSpec((None, 1, d), lambda b, i, j: (b, 0, 0)),
            pl.BlockSpec((None, 1, d), lambda b, i, j: (b, 0, 0)),
            pl.BlockSpec((d, tn), lambda b, i, j: (0, j)),
            pl.BlockSpec((1, tn), lambda b, i, j: (0, j)),
        ],
        out_specs=pl.BlockSpec((None, tm, tn), lambda b, i, j: (b, i, j)),
        out_shape=jax.ShapeDtypeStruct((bsz, t, n), out_dtype),
        scratch_shapes=[pltpu.VMEM((tm, d), BF16)],
        compiler_params=_cparams(("parallel", "parallel", "arbitrary")),
        name="norm_proj",
    )(x, g.reshape(1, d), sc, sh, w, col_scale.reshape(1, n))


def _da_kernel(c_ref, skip_ref, kidx_ref, q_ref, k_ref, v_ref, qf_ref, kf_ref, u_ref, lam_ref, g_ref, o_ref,
               qa_scr, m_scr, l_scr, acc_scr, s_scr, p_scr, *, lam_init):
    h = pl.program_id(1)
    qi = pl.program_id(2)
    kb = pl.program_id(3)
    nk = pl.num_programs(3)
    hd = DA_HEAD_DIM
    n_split, hr, tk = s_scr.shape[0] // 2, s_scr.shape[1], s_scr.shape[2]
    kt = lax.rem(qi + kb, nk)
    flat = ((pl.program_id(0) * pl.num_programs(1) + h) * pl.num_programs(2) + qi) * nk + kb

    @pl.when(kb == 0)
    def _():
        m_scr[...] = jnp.full(m_scr.shape, -jnp.inf, F32)
        l_scr[...] = jnp.zeros(l_scr.shape, F32)
        acc_scr[...] = jnp.zeros(acc_scr.shape, F32)
        q = q_ref[...]
        qf = qf_ref[...]
        for m in range(2):
            qm = q[:, m * hd:(m + 1) * hd]
            qa_scr[m] = jnp.concatenate([qm, qf], axis=1)

    def step(diag):
        sgn = jnp.where(kt > qi, -1.0, 1.0).astype(BF16)
        k = k_ref[...]
        kf = kf_ref[...] * sgn
        v = v_ref[...]
        c2 = 2.0 * c_ref[h]
        for m in range(2):
            ka = jnp.concatenate([k[:, m * hd:(m + 1) * hd], kf], axis=1)
            for sp in range(n_split):
                s_scr[m * n_split + sp] = _nt_dot(qa_scr[m, sp * hr:(sp + 1) * hr, :], ka)
        nlt = tk // LANE
        blocks = [slice(rb * DA_ROW_BLK, (rb + 1) * DA_ROW_BLK) for rb in range(hr // DA_ROW_BLK)]
        for m in range(2):
            for sp in range(n_split):
                ci = m * n_split + sp
                rs = slice(sp * hr, (sp + 1) * hr)
                pmax = []
                for loc in blocks:
                    s = s_scr[ci, loc, :]
                    if diag:
                        s = s - c2 * u_ref[sp * hr + loc.start:sp * hr + loc.stop, :]
                        s_scr[ci, loc, :] = s
                    pm = s[:, 0:LANE]
                    for lt in range(1, nlt):
                        pm = jnp.maximum(pm, s[:, lt * LANE:(lt + 1) * LANE])
                    pmax.append(pm)
                m_old = m_scr[m, rs, :]
                m_new = jnp.maximum(m_old, jnp.max(jnp.concatenate(pmax, axis=0), axis=1, keepdims=True))
                alpha = jnp.exp2(m_old - m_new)
                m_scr[m, rs, :] = m_new
                psum = []
                for loc in blocks:
                    s = s_scr[ci, loc, :]
                    mb = m_new[loc]
                    acc_l = None
                    for lt in range(nlt):
                        p = jnp.exp2(s[:, lt * LANE:(lt + 1) * LANE] - mb)
                        acc_l = p if acc_l is None else acc_l + p
                        p_scr[ci, loc, lt * LANE:(lt + 1) * LANE] = p.astype(BF16)
                    psum.append(acc_l)
                row_sum = jnp.sum(jnp.concatenate(psum, axis=0), axis=1, keepdims=True)
                l_scr[m, rs, :] = alpha * l_scr[m, rs, :] + row_sum
                pv = jnp.dot(p_scr[ci], v, preferred_element_type=F32)
                acc_scr[m, rs, :] = jnp.concatenate([alpha, alpha], axis=1) * acc_scr[m, rs, :] + pv

    @pl.when(kb == 0)
    def _():
        step(True)

    @pl.when(jnp.logical_and(kb > 0, skip_ref[flat] == 0))
    def _():
        step(False)

    @pl.when(kb == nk - 1)
    def _():
        lv = lam_ref[...]
        lam = (jnp.exp(jnp.sum(lv[0:1] * lv[1:2], axis=1, keepdims=True))
               - jnp.exp(jnp.sum(lv[2:3] * lv[3:4], axis=1, keepdims=True)) + lam_init)
        inv = [1.0 / l_scr[m] for m in range(2)]
        o = (acc_scr[0] * jnp.concatenate([inv[0], inv[0]], axis=1)
             - lam * (acc_scr[1] * jnp.concatenate([inv[1], inv[1]], axis=1)))
        o = _rms(o) * g_ref[...] * (1.0 - lam_init)
        o_ref[...] = o.astype(o_ref.dtype)


def _alibi_features(t, n_heads):
    slopes = 2.0 ** (-8.0 * np.arange(1, n_heads + 1) / n_heads)
    c = jnp.asarray(slopes * LOG2E, F32)
    cp = c[:, None] * jnp.arange(t, dtype=F32)[None, :]
    p3 = jnp.stack(_bf16_parts(cp, 3), axis=-1)
    ones = jnp.ones((n_heads, t, 3), BF16)
    pad = jnp.zeros((n_heads, t, LANE - 6), BF16)
    qf = jnp.concatenate([-p3, ones, pad], axis=-1)
    kf = jnp.concatenate([ones, p3, pad], axis=-1)
    return c, qf, kf


DA_SKIP_MARGIN = 200.0


def _norms_kernel(x_ref, o_ref):
    x = x_ref[...].astype(F32)
    cols = []
    for g in range(x.shape[1] // LANE):
        v = x[:, g * LANE:(g + 1) * LANE]
        cols.append(jnp.max(jnp.sum(v * v, axis=1, keepdims=True), axis=0, keepdims=True))
    o_ref[...] = jnp.concatenate(cols, axis=1)


def _da_tile_norms(qkv, tile):
    bsz, t, _ = qkv.shape
    width = 2 * DA_HEADS * DA_VAL_DIM
    return pl.pallas_call(
        _norms_kernel,
        grid=(bsz, t // tile),
        in_specs=[pl.BlockSpec((None, tile, width), lambda b, i: (b, i, 0))],
        out_specs=pl.BlockSpec((None, None, 1, width // LANE), lambda b, i: (b, i, 0, 0)),
        out_shape=jax.ShapeDtypeStruct((bsz, t // tile, 1, width // LANE), F32),
        compiler_params=_cparams(("parallel", "parallel")),
        name="da_tile_norms",
    )(qkv)


def _da_skip_plan(norms2, c, tile):
    bsz, nt = norms2.shape[0], norms2.shape[1]
    nh = DA_HEADS
    nrm = jnp.sqrt(norms2.reshape(bsz, nt, 2, nh, 2)) * 1.001
    nq, nkk = nrm[:, :, 0], nrm[:, :, 1]
    qi = jnp.arange(nt)[:, None]
    j = jnp.arange(nt)[None, :]
    kt = (qi + j) % nt
    bound = jnp.max(nq[:, :, None] * (nkk[:, kt] + nkk[:, :, None]), axis=-1)
    dist_min = jnp.maximum(jnp.abs(kt - qi) - 1, 0) * tile + 1
    far = c[None, None, None, :] * dist_min[None, :, :, None].astype(F32) >= DA_SKIP_MARGIN + bound
    skip = jnp.logical_and(far, (j > 0)[None, :, :, None]).transpose(0, 3, 1, 2)
    kidx = jnp.broadcast_to(kt[None, None], skip.shape)
    cols = [kidx[..., 0]]
    for jj in range(1, nt):
        cols.append(jnp.where(skip[..., jj], cols[-1], kidx[..., jj]))
    kidx = jnp.stack(cols, axis=-1)
    return skip.astype(jnp.int32).reshape(-1), kidx.astype(jnp.int32).reshape(-1)


def _diff_attention(qkv, lam_vecs, subln_g, lam_init, *, tile):
    bsz, t, _ = qkv.shape
    nh = DA_HEADS
    tq = tk = min(tile, t)
    nq = t // tq
    n_split = 2 if tq >= 4 * DA_ROW_BLK else 1
    c, qf, kf = _alibi_features(t, nh)
    skip, kidx = _da_skip_plan(_da_tile_norms(qkv, tq), c, tq)
    ii = jnp.arange(tq, dtype=F32)
    u = jnp.maximum(ii[None, :] - ii[:, None], 0.0)

    def key_tile(b, h, i, j, kidx_ref):
        return kidx_ref[((b * nh + h) * nq + i) * nq + j]

    grid_spec = pltpu.PrefetchScalarGridSpec(
        num_scalar_prefetch=3,
        grid=(bsz, nh, nq, nq),
        in_specs=[
            pl.BlockSpec((None, tq, DA_VAL_DIM), lambda b, h, i, j, c, sk, ki: (b, i, h)),
            pl.BlockSpec((None, tk, DA_VAL_DIM), lambda b, h, i, j, c, sk, ki: (b, key_tile(b, h, i, j, ki), nh + h)),
            pl.BlockSpec((None, tk, DA_VAL_DIM),
                         lambda b, h, i, j, c, sk, ki: (b, key_tile(b, h, i, j, ki), 2 * nh + h)),
            pl.BlockSpec((None, tq, LANE), lambda b, h, i, j, c, sk, ki: (h, i, 0)),
            pl.BlockSpec((None, tk, LANE), lambda b, h, i, j, c, sk, ki: (h, key_tile(b, h, i, j, ki), 0)),
            pl.BlockSpec((tq, tk), lambda b, h, i, j, c, sk, ki: (0, 0)),
            pl.BlockSpec((4, DA_HEAD_DIM), lambda b, h, i, j, c, sk, ki: (0, 0)),
            pl.BlockSpec((1, DA_VAL_DIM), lambda b, h, i, j, c, sk, ki: (0, 0)),
        ],
        out_specs=pl.BlockSpec((None, tq, DA_VAL_DIM), lambda b, h, i, j, c, sk, ki: (b, i, h)),
        scratch_shapes=[
            pltpu.VMEM((2, tq, 2 * DA_HEAD_DIM), BF16),
            pltpu.VMEM((2, tq, LANE), F32),
            pltpu.VMEM((2, tq, LANE), F32),
            pltpu.VMEM((2, tq, DA_VAL_DIM), F32),
            pltpu.VMEM((2 * n_split, tq // n_split, tk), F32),
            pltpu.VMEM((2 * n_split, tq // n_split, tk), BF16),
        ],
    )
    return pl.pallas_call(
        functools.partial(_da_kernel, lam_init=lam_init),
        grid_spec=grid_spec,
        out_shape=jax.ShapeDtypeStruct((bsz, t, nh * DA_VAL_DIM), BF16),
        compiler_params=_cparams(("parallel", "parallel", "parallel", "arbitrary")),
        name="diff_attention",
    )(c, skip, kidx, qkv, qkv, qkv, qf, kf, u, lam_vecs, subln_g.reshape(1, DA_VAL_DIM))


def _gla_kernel(q_ref, k_ref, v_ref, gl_ref, w2h_ref, w2l_ref, gb_ref, o_ref, st_scr, *, reverse, nchunk):
    cs = GLA_CHUNK
    blk = nchunk * cs

    @pl.when(pl.program_id(2) == 0)
    def _():
        st_scr[...] = jnp.zeros(st_scr.shape, F32)

    ri = lax.broadcasted_iota(jnp.int32, (blk, blk), 0)
    ci = lax.broadcasted_iota(jnp.int32, (blk, blk), 1)
    same_chunk = (ri // cs) == (ci // cs)
    keep = jnp.logical_and(same_chunk, (ci >= ri) if reverse else (ci <= ri))
    tri = jnp.where(keep, 1.0, 0.0).astype(BF16)
    w2h = w2h_ref[...]
    gh, glo = _split2(gl_ref[...])
    x = (jnp.dot(gh, w2h, preferred_element_type=F32) + jnp.dot(glo, w2h, preferred_element_type=F32)
         + jnp.dot(gh, w2l_ref[...], preferred_element_type=F32) + gb_ref[...])
    la = (jnp.minimum(x, 0.0) - jnp.log1p(jnp.exp(-jnp.abs(x)))) * (1.0 / GLA_TAU)
    a1, a2, a3 = _split3(la)
    b = (jnp.dot(tri, a1, preferred_element_type=F32) + jnp.dot(tri, a2, preferred_element_type=F32)
         + jnp.dot(tri, a3, preferred_element_type=F32))
    q = q_ref[...].astype(F32)
    k = k_ref[...].astype(F32)
    v = v_ref[...]
    q_in = (q * jnp.exp(b)).astype(BF16)
    k_in = (k * jnp.exp(-b)).astype(BF16)
    att = jnp.where(keep, _nt_dot(q_in, k_in), 0.0)
    o_intra = jnp.dot(att.astype(BF16), v, preferred_element_type=F32)
    chunks = [slice(c * cs, (c + 1) * cs) for c in range(nchunk)]
    tots = [b[r.start:r.start + 1] if reverse else b[r.stop - 1:r.stop] for r in chunks]
    kvs = [_tn_dot(v[r], (k[r] * jnp.exp(tot - b[r])).astype(BF16)) for r, tot in zip(chunks, tots)]
    st = st_scr[...]
    for c in (range(nchunk - 1, -1, -1) if reverse else range(nchunk)):
        r = chunks[c]
        o_ref[r, :] = o_intra[r] + _nt_dot(q_in[r], st.astype(BF16))
        st = st * jnp.exp(tots[c]) + kvs[c]
    st_scr[...] = st


def _gla_scan(qkvr, g_lr, w2h, w2l, gb, *, reverse, blk):
    bsz, t, _ = qkvr.shape
    nh = GLA_HEADS
    blk = min(blk, t)
    nblk = t // blk
    pos = (lambda i: nblk - 1 - i) if reverse else (lambda i: i)
    kq = (nh * GLA_DK) // GLA_DK
    kv = (2 * nh * GLA_DK) // GLA_DV
    return pl.pallas_call(
        functools.partial(_gla_kernel, reverse=reverse, nchunk=blk // GLA_CHUNK),
        grid=(bsz, nh, nblk),
        in_specs=[
            pl.BlockSpec((None, blk, GLA_DK), lambda b, h, i: (b, pos(i), h)),
            pl.BlockSpec((None, blk, GLA_DK), lambda b, h, i: (b, pos(i), kq + h)),
            pl.BlockSpec((None, blk, GLA_DV), lambda b, h, i: (b, pos(i), kv + h)),
            pl.BlockSpec((None, blk, LANE), lambda b, h, i: (b, pos(i), 0)),
            pl.BlockSpec((LANE, GLA_DK), lambda b, h, i: (0, h)),
            pl.BlockSpec((LANE, GLA_DK), lambda b, h, i: (0, h)),
            pl.BlockSpec((1, GLA_DK), lambda b, h, i: (0, h)),
        ],
        out_specs=pl.BlockSpec((None, blk, GLA_DV), lambda b, h, i: (b, pos(i), h)),
        out_shape=jax.ShapeDtypeStruct((bsz, t, nh * GLA_DV), F32),
        scratch_shapes=[pltpu.VMEM((GLA_DV, GLA_DK), F32)],
        compiler_params=_cparams(("parallel", "parallel", "arbitrary")),
        name="gla_scan_bwd" if reverse else "gla_scan_fwd",
    )(qkvr, qkvr, qkvr, g_lr, w2h, w2l, gb)


def _gla_fin_kernel(of_ref, ob_ref, r_ref, g_ref, o_ref):
    o = _rms(of_ref[...] + ob_ref[...]) * g_ref[...]
    r = r_ref[...].astype(F32)
    o_ref[...] = (o * (r * jax.nn.sigmoid(r))).astype(o_ref.dtype)


def _gla_finish(o_f, o_b, qkvr, norm_g, *, tm):
    bsz, t, _ = o_f.shape
    nh = GLA_HEADS
    tm = min(tm, t)
    kr = (2 * nh * GLA_DK + nh * GLA_DV) // GLA_DV
    return pl.pallas_call(
        _gla_fin_kernel,
        grid=(bsz, t // tm, nh),
        in_specs=[
            pl.BlockSpec((None, tm, GLA_DV), lambda b, i, h: (b, i, h)),
            pl.BlockSpec((None, tm, GLA_DV), lambda b, i, h: (b, i, h)),
            pl.BlockSpec((None, tm, GLA_DV), lambda b, i, h: (b, i, kr + h)),
            pl.BlockSpec((1, GLA_DV), lambda b, i, h: (0, 0)),
        ],
        out_specs=pl.BlockSpec((None, tm, GLA_DV), lambda b, i, h: (b, i, h)),
        out_shape=jax.ShapeDtypeStruct((bsz, t, nh * GLA_DV), BF16),
        compiler_params=_cparams(("parallel", "parallel", "parallel")),
        name="gla_finish",
    )(o_f, o_b, qkvr, norm_g.reshape(1, GLA_DV))


def _na_kernel(q_ref, kp, km, kn, vp, vm, vn, tbl_ref, o_ref, kcat, vcat, *, n_rows):
    i = pl.program_id(2)
    w = GRID_W
    halo = (NA_ROWS // 2) * w
    main = NA_QROWS * w
    for ref_p, ref_m, ref_n, cat in ((kp, km, kn, kcat), (vp, vm, vn, vcat)):
        cat[0:halo, :] = ref_p[...]
        cat[halo:halo + main, :] = ref_m[...]
        cat[halo + main:2 * halo + main, :] = ref_n[...]
    lane_head = lax.broadcasted_iota(jnp.int32, (w, LANE), 1) // NA_HEAD_DIM
    base = i * NA_QROWS
    for rr in range(NA_QROWS):
        r = base + rr
        r0 = jnp.clip(r - NA_ROWS // 2, 0, n_rows - NA_ROWS)
        start = pl.multiple_of((NA_ROWS // 2 + r0 - base) * w, w)
        delta = r - r0
        qr = q_ref[rr * w:(rr + 1) * w, :]
        qs = jnp.concatenate([jnp.where(lane_head == hh, qr, jnp.zeros_like(qr)) for hh in range(NA_GROUP)], axis=0)
        kb = kcat[pl.ds(start, NA_ROWS * w), :]
        vb = vcat[pl.ds(start, NA_ROWS * w), :]
        s = _nt_dot(qs, kb)
        bias = jnp.concatenate(
            [tbl_ref[2 * j - delta + NA_ROWS - 1].reshape(NA_GROUP * w, 2 * w) for j in range(NA_ROWS // 2)], axis=1)
        s = s + bias
        p = jnp.exp2(s - jnp.max(s, axis=1, keepdims=True))
        l = jnp.sum(p, axis=1, keepdims=True)
        o4 = jnp.dot(p.astype(BF16), vb, preferred_element_type=F32) / l
        o = jnp.zeros((w, LANE), F32)
        for hh in range(NA_GROUP):
            o = jnp.where(lane_head == hh, o4[hh * w:(hh + 1) * w, :], o)
        o_ref[rr * w:(rr + 1) * w, :] = o.astype(o_ref.dtype)


def _na_bias_table(rpb):
    col = np.arange(GRID_W)
    col_start = np.clip(col - NA_COLS // 2, 0, GRID_W - NA_COLS)
    in_window = (col[None, :] >= col_start[:, None]) & (col[None, :] < col_start[:, None] + NA_COLS)
    col_off = np.clip(col[None, :] - col[:, None] + NA_COLS - 1, 0, 2 * NA_COLS - 2)
    cb = jnp.where(in_window[None, None], rpb.astype(F32)[:, :, col_off] * LOG2E, NEG_INF)
    pair = jnp.concatenate([cb[:, :-1], cb[:, 1:]], axis=-1)
    n_pair = 2 * NA_ROWS - 2
    pair = pair.reshape(NA_HEADS // NA_GROUP, NA_GROUP, n_pair, GRID_W, 2 * GRID_W)
    return pair.transpose(0, 2, 1, 3, 4)


def _neighborhood_attention(qkv, rpb):
    bsz, t, _ = qkv.shape
    n_rows = t // GRID_W
    ng = NA_HEADS // NA_GROUP
    tq = NA_QROWS * GRID_W
    halo = (NA_ROWS // 2) * GRID_W
    per = tq // halo
    n_halo = t // halo
    tbl = _na_bias_table(rpb)

    def kv_specs(col0):
        return [
            pl.BlockSpec((None, halo, LANE), lambda g, b, i: (b, jnp.maximum(i * per - 1, 0), col0 + g)),
            pl.BlockSpec((None, tq, LANE), lambda g, b, i: (b, i, col0 + g)),
            pl.BlockSpec((None, halo, LANE), lambda g, b, i: (b, jnp.minimum((i + 1) * per, n_halo - 1), col0 + g)),
        ]

    cat_rows = tq + 2 * halo
    return pl.pallas_call(
        functools.partial(_na_kernel, n_rows=n_rows),
        grid=(ng, bsz, t // tq),
        in_specs=[pl.BlockSpec((None, tq, LANE), lambda g, b, i: (b, i, g))] + kv_specs(ng) + kv_specs(2 * ng)
        + [pl.BlockSpec((None, 2 * NA_ROWS - 2, NA_GROUP, GRID_W, 2 * GRID_W), lambda g, b, i: (g, 0, 0, 0, 0))],
        out_specs=pl.BlockSpec((None, tq, LANE), lambda g, b, i: (b, i, g)),
        out_shape=jax.ShapeDtypeStruct((bsz, t, NA_HEADS * NA_HEAD_DIM), BF16),
        scratch_shapes=[pltpu.VMEM((cat_rows, LANE), BF16), pltpu.VMEM((cat_rows, LANE), BF16)],
        compiler_params=_cparams(("parallel", "parallel", "parallel")),
        name="neighborhood_attention",
    )(qkv, *([qkv] * 6), tbl)


def _post_kernel(o_ref, w_ref, x_ref, g1_ref, n2_ref, sc_ref, sh_ref, wrh_ref, wrl_ref, xn_ref, hf_ref, lg_ref):
    y = jnp.dot(o_ref[...], w_ref[...], preferred_element_type=F32)
    xn = x_ref[...] + g1_ref[...] * y
    xn_ref[...] = xn
    hf = _rms(xn) * n2_ref[...] * (1.0 + sc_ref[...]) + sh_ref[...]
    hf_ref[...] = hf
    hh, hl = _split2(hf)
    wrh = wrh_ref[...]
    lg_ref[...] = (jnp.dot(hh, wrh, preferred_element_type=F32) + jnp.dot(hl, wrh, preferred_element_type=F32)
                   + jnp.dot(hh, wrl_ref[...], preferred_element_type=F32))


def _post_mixer(o, w_out, x, g1, n2g, sc2, sh2, wr_hi, wr_lo, *, tm):
    bsz, t, d = x.shape
    kdim = o.shape[-1]
    tm = min(tm, t)
    vec = pl.BlockSpec((None, 1, d), lambda b, i: (b, 0, 0))
    tile = pl.BlockSpec((None, tm, d), lambda b, i: (b, i, 0))
    return pl.pallas_call(
        _post_kernel,
        grid=(bsz, t // tm),
        in_specs=[
            pl.BlockSpec((None, tm, kdim), lambda b, i: (b, i, 0)),
            pl.BlockSpec((kdim, d), lambda b, i: (0, 0)),
            tile, vec,
            pl.BlockSpec((1, d), lambda b, i: (0, 0)),
            vec, vec,
            pl.BlockSpec((d, LANE), lambda b, i: (0, 0)),
            pl.BlockSpec((d, LANE), lambda b, i: (0, 0)),
        ],
        out_specs=[tile, tile, pl.BlockSpec((None, tm, LANE), lambda b, i: (b, i, 0))],
        out_shape=[jax.ShapeDtypeStruct((bsz, t, d), F32), jax.ShapeDtypeStruct((bsz, t, d), F32),
                   jax.ShapeDtypeStruct((bsz, t, LANE), F32)],
        compiler_params=_cparams(("parallel", "parallel")),
        name="post_mixer",
    )(o, w_out, x, g1, n2g.reshape(1, d), sc2, sh2, wr_hi, wr_lo)


ROUTER_ROWS = 8 + MOE_EXPERTS


def _router_kernel(lt_ref, b_ref, id_ref, w_ref):
    lt = lt_ref[...] + b_ref[...]
    tn = lt.shape[1]
    lg = lt[0:MOE_GROUPS]
    e = jnp.exp(lg - jnp.max(lg, axis=0, keepdims=True))
    gp = e / jnp.sum(e, axis=0, keepdims=True)
    g_p = jnp.max(gp, axis=0, keepdims=True)
    rg = lax.broadcasted_iota(jnp.int32, (MOE_GROUPS, tn), 0)
    g_idx = jnp.min(jnp.where(gp == g_p, rg, MOE_GROUPS), axis=0, keepdims=True)
    el = jnp.zeros((MOE_EPG, tn), F32)
    for g in range(MOE_GROUPS):
        el = jnp.where(g_idx == g, lt[8 + g * MOE_EPG:8 + (g + 1) * MOE_EPG], el)
    ee = jnp.exp(el - jnp.max(el, axis=0, keepdims=True))
    ep = ee / jnp.sum(ee, axis=0, keepdims=True)
    re = lax.broadcasted_iota(jnp.int32, (MOE_EPG, tn), 0)
    p1 = jnp.max(ep, axis=0, keepdims=True)
    i1 = jnp.min(jnp.where(ep == p1, re, MOE_EPG), axis=0, keepdims=True)
    ep2 = jnp.where(re == i1, -1.0, ep)
    p2 = jnp.max(ep2, axis=0, keepdims=True)
    i2 = jnp.min(jnp.where(ep2 == p2, re, MOE_EPG), axis=0, keepdims=True)
    den = p1 + p2
    id_ref[0:1, :] = g_idx * MOE_EPG + i1
    id_ref[1:2, :] = g_idx * MOE_EPG + i2
    w_ref[0:1, :] = g_p * (p1 / den)
    w_ref[1:2, :] = g_p * (p2 / den)


def _router(logits_t, bias_col, *, tn):
    n = logits_t.shape[1]
    tn = min(tn, n)
    return pl.pallas_call(
        _router_kernel,
        grid=(n // tn,),
        in_specs=[pl.BlockSpec((ROUTER_ROWS, tn), lambda i: (0, i)),
                  pl.BlockSpec((ROUTER_ROWS, 1), lambda i: (0, 0))],
        out_specs=[pl.BlockSpec((MOE_TOP_K, tn), lambda i: (0, i)), pl.BlockSpec((MOE_TOP_K, tn), lambda i: (0, i))],
        out_shape=[jax.ShapeDtypeStruct((MOE_TOP_K, n), jnp.int32), jax.ShapeDtypeStruct((MOE_TOP_K, n), F32)],
        compiler_params=_cparams(("parallel",)),
        name="router_topk",
    )(logits_t, bias_col)


MOE_DMA_UNROLL = 8


def _moe_kernel(pb_ref, pe_ref, r0_ref, r1_ref, fl_ref, np_ref, src_cur, src_nxt, dst_prv, dst_lst, hf_hbm,
                wgu_ref, wd_ref, y_hbm, wgu_bf, wd_bf, xbuf, obuf, gsem, ssem, *, n_blocks):
    s = pl.program_id(0)
    blk = xbuf.shape[1]
    ff = wd_ref.shape[0]
    b = pb_ref[s]
    slot = b % 2
    flags = fl_ref[s]
    active = s < np_ref[0]
    first = jnp.logical_and(active, (flags & 1) != 0)
    new_w = jnp.logical_and(active, (flags & 4) != 0)

    def gather_copy(src_ref, sl, r):
        return pltpu.make_async_copy(hf_hbm.at[pl.ds(src_ref[0, r], 1)], xbuf.at[sl, pl.ds(r, 1)], gsem.at[sl])

    def scatter_copy(dst_ref, sl, r):
        return pltpu.make_async_copy(obuf.at[sl, pl.ds(r, 1)], y_hbm.at[pl.ds(dst_ref[0, r], 1)], ssem.at[sl])

    def issue_loop(make):
        def body(g, carry):
            for j in range(MOE_DMA_UNROLL):
                make(g * MOE_DMA_UNROLL + j).start(priority=j % 2)
            return carry
        lax.fori_loop(0, blk // MOE_DMA_UNROLL, body, 0)

    def wait_gather(sl):
        pltpu.make_async_copy(hf_hbm.at[pl.ds(0, blk)], xbuf.at[sl], gsem.at[sl]).wait()

    def wait_scatter(sl):
        pltpu.make_async_copy(obuf.at[sl], y_hbm.at[pl.ds(0, blk)], ssem.at[sl]).wait()

    @pl.when(s == 0)
    def _():
        obuf[...] = jnp.zeros(obuf.shape, F32)
        issue_loop(lambda r: gather_copy(src_cur, 0, r))

    @pl.when(first)
    def _():
        wait_gather(slot)

        @pl.when(b >= 2)
        def _():
            wait_scatter(slot)

    @pl.when(new_w)
    def _():
        wgu_bf[...] = wgu_ref[...].astype(BF16)
        wd_bf[...] = wd_ref[...].astype(BF16)

    def compute():
        x = xbuf[slot].astype(BF16)
        hgu = jnp.dot(x, wgu_bf[...], preferred_element_type=F32)
        g = hgu[:, :ff]
        u = hgu[:, ff:]
        act = (g * jax.nn.sigmoid(g) * u).astype(BF16)
        y = jnp.dot(act, wd_bf[...], preferred_element_type=F32)
        ri = lax.broadcasted_iota(jnp.int32, (blk, 1), 0)
        mine = jnp.logical_and(ri >= r0_ref[s], ri < r1_ref[s])
        obuf[slot] = jnp.where(mine, y, obuf[slot])

    @pl.when(first)
    def _():
        do_gather = b + 1 < n_blocks
        do_scatter = b >= 1
        for r in range(blk):
            @pl.when(do_gather)
            def _():
                gather_copy(src_nxt, 1 - slot, r).start(priority=r % 2)

            @pl.when(do_scatter)
            def _():
                scatter_copy(dst_prv, 1 - slot, r).start(priority=(r + 1) % 2)

        compute()

    @pl.when(jnp.logical_and(active, jnp.logical_not(first)))
    def _():
        compute()

    @pl.when(s == pl.num_programs(0) - 1)
    def _():
        last_slot = (n_blocks - 1) % 2
        issue_loop(lambda r: scatter_copy(dst_lst, last_slot, r))
        wait_scatter(0)
        wait_scatter(1)


def _moe_plan(expert_id, blk):
    flat_e = expert_id.reshape(-1)
    n_asg = flat_e.shape[0]
    n_blocks = n_asg // blk
    n_steps = n_blocks + MOE_EXPERTS - 1
    _, asg = lax.sort_key_val(flat_e, jnp.arange(n_asg, dtype=jnp.int32))
    counts = jnp.sum((flat_e[:, None] == jnp.arange(MOE_EXPERTS, dtype=jnp.int32)[None, :]).astype(jnp.int32), axis=0)
    ends = jnp.cumsum(counts)
    starts = ends - counts
    src_tok = asg // MOE_TOP_K
    dst_row = (asg % MOE_TOP_K) * (n_asg // MOE_TOP_K) + asg // MOE_TOP_K
    bidx = jnp.arange(n_blocks, dtype=jnp.int32)
    e_lo = jnp.searchsorted(ends, bidx * blk, side='right').astype(jnp.int32)
    e_hi = jnp.searchsorted(ends, (bidx + 1) * blk - 1, side='right').astype(jnp.int32)
    n_pair_b = e_hi - e_lo + 1
    pair_end = jnp.cumsum(n_pair_b)
    pair_start = pair_end - n_pair_b
    n_pairs = pair_end[-1]
    sidx = jnp.arange(n_steps, dtype=jnp.int32)
    pb = jnp.minimum(jnp.searchsorted(pair_end, sidx, side='right').astype(jnp.int32), n_blocks - 1)
    pe = jnp.clip(e_lo[pb] + sidx - pair_start[pb], 0, MOE_EXPERTS - 1)
    pe = jnp.where(sidx < n_pairs, pe, pe[jnp.maximum(n_pairs - 1, 0)])
    r0 = jnp.clip(starts[pe] - pb * blk, 0, blk)
    r1 = jnp.clip(ends[pe] - pb * blk, 0, blk)
    prev_e = jnp.concatenate([jnp.full((1,), -1, jnp.int32), pe[:-1]])
    flags = ((sidx == pair_start[pb]).astype(jnp.int32) + 2 * (sidx == pair_end[pb] - 1).astype(jnp.int32)
             + 4 * (pe != prev_e).astype(jnp.int32))
    flags = jnp.where(sidx < n_pairs, flags, 0)
    i32 = lambda a: a.astype(jnp.int32)
    return (i32(pb), i32(pe), i32(r0), i32(r1), i32(flags), i32(n_pairs).reshape(1),
            src_tok.reshape(n_blocks, 1, blk), dst_row.reshape(n_blocks, 1, blk))


def _moe_experts(hf, expert_id, w_gu, w_d, layer, *, blk):
    n_tok, d = hf.shape
    n_asg = n_tok * MOE_TOP_K
    blk = min(blk, n_asg // 2)
    n_blocks = n_asg // blk
    pb, pe, r0, r1, flags, n_pairs, src_tok, dst_row = _moe_plan(expert_id, blk)
    ff = w_d.shape[2]
    smem = functools.partial(pl.BlockSpec, memory_space=pltpu.SMEM)
    grid_spec = pltpu.PrefetchScalarGridSpec(
        num_scalar_prefetch=6,
        grid=(pb.shape[0],),
        in_specs=[
            smem((None, 1, blk), lambda s, pb, *_: (pb[s], 0, 0)),
            smem((None, 1, blk), lambda s, pb, *_: (jnp.minimum(pb[s] + 1, n_blocks - 1), 0, 0)),
            smem((None, 1, blk), lambda s, pb, *_: (jnp.maximum(pb[s] - 1, 0), 0, 0)),
            smem((None, 1, blk), lambda s, pb, *_: (n_blocks - 1, 0, 0)),
            pl.BlockSpec(memory_space=pl.ANY),
            pl.BlockSpec((None, None, d, 2 * ff), lambda s, pb, pe, *_: (layer, pe[s], 0, 0)),
            pl.BlockSpec((None, None, ff, d), lambda s, pb, pe, *_: (layer, pe[s], 0, 0)),
        ],
        out_specs=pl.BlockSpec(memory_space=pl.ANY),
        scratch_shapes=[
            pltpu.VMEM((d, 2 * ff), BF16),
            pltpu.VMEM((ff, d), BF16),
            pltpu.VMEM((2, blk, d), F32),
            pltpu.VMEM((2, blk, d), F32),
            pltpu.SemaphoreType.DMA((2,)),
            pltpu.SemaphoreType.DMA((2,)),
        ],
    )
    return pl.pallas_call(
        functools.partial(_moe_kernel, n_blocks=n_blocks),
        grid_spec=grid_spec,
        out_shape=jax.ShapeDtypeStruct((n_asg, d), F32),
        compiler_params=_cparams(("arbitrary",)),
        name="moe_experts",
    )(pb, pe, r0, r1, flags, n_pairs, src_tok, src_tok, dst_row, dst_row, hf, w_gu, w_d)


def _combine_kernel(x_ref, y0_ref, y1_ref, w_ref, g2_ref, fg_ref, o_ref, *, final):
    w = w_ref[...]
    moe = w[:, 0:1] * y0_ref[...] + w[:, 1:2] * y1_ref[...]
    xn = x_ref[...] + g2_ref[...] * moe
    if final:
        xn = _rms(xn) * fg_ref[...]
    o_ref[...] = xn


def _combine(x, ybuf, wts, g2, final_g, *, final, tm):
    bsz, t, d = x.shape
    tm = min(tm, t)
    y2 = ybuf.reshape(MOE_TOP_K, bsz * t, d)
    nt = t // tm
    return pl.pallas_call(
        functools.partial(_combine_kernel, final=final),
        grid=(bsz, nt),
        in_specs=[
            pl.BlockSpec((None, tm, d), lambda b, i: (b, i, 0)),
            pl.BlockSpec((None, tm, d), lambda b, i: (0, b * nt + i, 0)),
            pl.BlockSpec((None, tm, d), lambda b, i: (1, b * nt + i, 0)),
            pl.BlockSpec((tm, MOE_TOP_K), lambda b, i: (b * nt + i, 0)),
            pl.BlockSpec((None, 1, d), lambda b, i: (b, 0, 0)),
            pl.BlockSpec((1, d), lambda b, i: (0, 0)),
        ],
        out_specs=pl.BlockSpec((None, tm, d), lambda b, i: (b, i, 0)),
        out_shape=jax.ShapeDtypeStruct((bsz, t, d), F32),
        compiler_params=_cparams(("parallel", "parallel")),
        name="moe_combine_final" if final else "moe_combine",
    )(x, y2, y2, wts, g2, final_g.reshape(1, d))


def _router_weights(rg_w, rg_b, re_w, re_b):
    d = rg_w.shape[0]
    wr = jnp.zeros((d, LANE), F32).at[:, 0:MOE_GROUPS].set(rg_w).at[:, 8:8 + MOE_EXPERTS].set(re_w)
    bias = jnp.zeros((ROUTER_ROWS, 1), F32).at[0:MOE_GROUPS, 0].set(rg_b).at[8:, 0].set(re_b)
    return tuple(_bf16_parts(wr, 2)) + (bias,)


def _col_scale(n, n_scaled, scale):
    return jnp.concatenate([jnp.full((n_scaled,), scale, F32), jnp.ones((n - n_scaled,), F32)])


def kernel(x, c, ada_w, ada_b, norm1_g, norm2_g, router_g_w, router_g_b, router_e_w, router_e_b, moe_w_gu, moe_w_d, da_w_in, da_w_out, da_lam_q1, da_lam_k1, da_lam_q2, da_lam_k2, da_subln_g, gla_w_in, gla_gate_w2_f, gla_gate_b_f, gla_gate_w2_b, gla_gate_b_b, gla_norm_g, gla_w_out, na_w_in, na_rpb, na_w_out, final_g):
    bsz, t, d = x.shape
    n_tok = bsz * t
    mod = _ada_mod(c, ada_w, ada_b)
    wts = ybuf = g2 = None
    for i in range(DEPTH):
        sh1, sc1, g1, sh2, sc2, g2_i = [mod[i, :, None, m * d:(m + 1) * d] for m in range(6)]
        if i > 0:
            x = _combine(x, ybuf, wts, g2, final_g, final=False, tm=512)
        kind, j = i % N_MIXERS, i // N_MIXERS
        if kind == 0:
            w_in = da_w_in[j].astype(BF16)
            cs = _col_scale(w_in.shape[1], DA_HEADS * DA_VAL_DIM, DA_HEAD_DIM ** -0.5 * LOG2E)
            qkv = _norm_proj(x, norm1_g[i], sc1, sh1, w_in, cs, tm=1024, tn=1024, out_dtype=BF16)
            lam_vecs = jnp.stack([da_lam_q1[j], da_lam_k1[j], da_lam_q2[j], da_lam_k2[j]]).astype(F32)
            lam_init = 0.8 - 0.6 * math.exp(-0.3 * i)
            o = _diff_attention(qkv, lam_vecs, da_subln_g[j], lam_init, tile=1024)
            w_out = da_w_out[j]
        elif kind == 1:
            n_main = 2 * GLA_HEADS * GLA_DK + 2 * GLA_HEADS * GLA_DV
            w_main = gla_w_in[j][:, :n_main].astype(BF16)
            w_gate = jnp.zeros((d, LANE), F32).at[:, :2 * GLA_GATE_RANK].set(gla_w_in[j][:, n_main:]).astype(BF16)
            cs = _col_scale(n_main, GLA_HEADS * GLA_DK, GLA_DK ** -0.5)
            qkvr = _norm_proj(x, norm1_g[i], sc1, sh1, w_main, cs, tm=1024, tn=1024, out_dtype=BF16)
            g_lr = _norm_proj(x, norm1_g[i], sc1, sh1, w_gate, jnp.ones((LANE,), F32), tm=1024, tn=LANE,
                              out_dtype=F32)
            outs = []
            for rev, w2, gb in ((False, gla_gate_w2_f[j], gla_gate_b_f[j]), (True, gla_gate_w2_b[j], gla_gate_b_b[j])):
                r0 = GLA_GATE_RANK if rev else 0
                w2p = jnp.zeros((LANE, w2.shape[1]), F32).at[r0:r0 + GLA_GATE_RANK].set(w2)
                w2h, w2l = _bf16_parts(w2p, 2)
                outs.append(_gla_scan(qkvr, g_lr, w2h, w2l, gb.reshape(1, -1).astype(F32), reverse=rev, blk=512))
            o = _gla_finish(outs[0], outs[1], qkvr, gla_norm_g[j], tm=512)
            w_out = gla_w_out[j]
        else:
            w_in = na_w_in[j].astype(BF16)
            cs = _col_scale(w_in.shape[1], NA_HEADS * NA_HEAD_DIM, NA_HEAD_DIM ** -0.5 * LOG2E)
            qkv = _norm_proj(x, norm1_g[i], sc1, sh1, w_in, cs, tm=1024, tn=1024, out_dtype=BF16)
            o = _neighborhood_attention(qkv, na_rpb[j])
            w_out = na_w_out[j]
        wr_hi, wr_lo, r_bias = _router_weights(router_g_w[i], router_g_b[i], router_e_w[i], router_e_b[i])
        x, hf, logits = _post_mixer(o, w_out.astype(BF16), x, g1, norm2_g[i], sc2, sh2, wr_hi, wr_lo, tm=256)
        logits_t = logits.reshape(n_tok, LANE).T[:ROUTER_ROWS]
        ids, wt = _router(logits_t, r_bias, tn=2048)
        ybuf = _moe_experts(hf.reshape(n_tok, d), ids.T, moe_w_gu, moe_w_d, i, blk=MOE_BLK)
        wts = wt.T
        g2 = g2_i
    return _combine(x, ybuf, wts, g2, final_g, final=True, tm=512)
```

```python
import functools
import math

import numpy as np
import jax
import jax.numpy as jnp
from jax import lax
from jax.experimental import pallas as pl
from jax.experimental.pallas import tpu as pltpu

F32 = jnp.float32
BF16 = jnp.bfloat16

D_MODEL = 2048
DEPTH = 4
N_MIXERS = 3
NORM_EPS = 1e-6
NEG_INF = -1e30
LOG2E = 1.4426950408889634

DA_HEADS = 8
DA_HEAD_DIM = 128
DA_VAL_DIM = 2 * DA_HEAD_DIM
DA_ROW_BLK = 16

GLA_HEADS = 4
GLA_DK = 256
GLA_DV = 512
GLA_GATE_RANK = 16
GLA_TAU = 16.0
GLA_CHUNK = 64

GRID_W = 64
NA_HEADS = 64
NA_HEAD_DIM = 32
NA_ROWS = 8
NA_COLS = 16
NA_GROUP = 4
NA_QROWS = 32

MOE_GROUPS = 4
MOE_EPG = 8
MOE_EXPERTS = 32
MOE_TOP_K = 2
MOE_D_FF = 512
MOE_BLK = 256
MOE_SLAB = D_MODEL // 128
MOE_PITCH = 20

LANE = 128
VMEM_LIMIT = 56 * 1024 * 1024


def _cparams(sem):
    return pltpu.CompilerParams(dimension_semantics=sem, vmem_limit_bytes=VMEM_LIMIT)


def _split2(v):
    hi = v.astype(BF16)
    lo = (v - hi.astype(F32)).astype(BF16)
    return hi, lo


def _split3(v):
    hi = v.astype(BF16)
    r = v - hi.astype(F32)
    mid = r.astype(BF16)
    lo = (r - mid.astype(F32)).astype(BF16)
    return hi, mid, lo


def _bf16_parts(v, n):
    parts = []
    r = v.astype(F32)
    for _ in range(n):
        top = lax.bitcast_convert_type(
            lax.bitcast_convert_type(r, jnp.uint32) & jnp.uint32(0xFFFF0000), F32)
        parts.append(top.astype(BF16))
        r = r - top
    return parts


def _nt_dot(a, b):
    return lax.dot_general(a, b, (((1,), (1,)), ((), ())), preferred_element_type=F32)


def _tn_dot(a, b):
    return lax.dot_general(a, b, (((0,), (0,)), ((), ())), preferred_element_type=F32)


def _rms(x):
    return x * lax.rsqrt(jnp.mean(x * x, axis=-1, keepdims=True) + NORM_EPS)


def _ada_kernel(c_ref, w_ref, b_ref, o_ref):
    c = c_ref[...]
    cond = (c * jax.nn.sigmoid(c)).astype(BF16)
    o_ref[...] = jnp.dot(cond, w_ref[...].astype(BF16), preferred_element_type=F32) + b_ref[...]


def _ada_mod(c, ada_w, ada_b):
    depth, d, n = ada_w.shape
    bsz = c.shape[0]
    rows = 8
    c8 = jnp.zeros((rows, d), F32).at[:bsz].set(c)
    tn = 1024
    out = pl.pallas_call(
        _ada_kernel,
        grid=(depth, n // tn),
        in_specs=[
            pl.BlockSpec((rows, d), lambda l, j: (0, 0)),
            pl.BlockSpec((None, d, tn), lambda l, j: (l, 0, j)),
            pl.BlockSpec((None, 1, tn), lambda l, j: (l, 0, j)),
        ],
        out_specs=pl.BlockSpec((None, rows, tn), lambda l, j: (l, 0, j)),
        out_shape=jax.ShapeDtypeStruct((depth, rows, n), F32),
        compiler_params=_cparams(("parallel", "parallel")),
        name="ada_mod",
    )(c8, ada_w, ada_b.reshape(depth, 1, n))
    return out[:, :bsz]


def _proj_kernel(x_ref, g_ref, sc_ref, sh_ref, w_ref, cs_ref, o_ref, h_scr):
    @pl.when(pl.program_id(2) == 0)
    def _():
        h = _rms(x_ref[...]) * g_ref[...] * (1.0 + sc_ref[...]) + sh_ref[...]
        h_scr[...] = h.astype(BF16)

    acc = jnp.dot(h_scr[...], w_ref[...], preferred_element_type=F32)
    o_ref[...] = (acc * cs_ref[...]).astype(o_ref.dtype)


def _norm_proj(x, g, sc, sh, w, col_scale, *, tm, tn, out_dtype):
    bsz, t, d = x.shape
    n = w.shape[1]
    tm = min(tm, t)
    return pl.pallas_call(
        _proj_kernel,
        grid=(bsz, t // tm, n // tn),
        in_specs=[
            pl.BlockSpec((None, tm, d), lambda b, i, j: (b, i, 0)),
            pl.BlockSpec((1, d), lambda b, i, j: (0, 0)),
            pl.BlockSpec((None, 1, d), lambda b, i, j: (b, 0, 0)),
            pl.BlockSpec((None, 1, d), lambda b, i, j: (b, 0, 0)),
            pl.BlockSpec((d, tn), lambda b, i, j: (0, j)),
            pl.BlockSpec((1, tn), lambda b, i, j: (0, j)),
        ],
        out_specs=pl.BlockSpec((None, tm, tn), lambda b, i, j: (b, i, j)),
        out_shape=jax.ShapeDtypeStruct((bsz, t, n), out_dtype),
        scratch_shapes=[pltpu.VMEM((tm, d), BF16)],
        compiler_params=_cparams(("parallel", "parallel", "arbitrary")),
        name="norm_proj",
    )(x, g.reshape(1, d), sc, sh, w, col_scale.reshape(1, n))


def _da_kernel(c_ref, skip_ref, kidx_ref, q_ref, k_ref, v_ref, qf_ref, kf_ref, u_ref, lam_ref, g_ref, o_ref,
               qa_scr, m_scr, l_scr, acc_scr, s_scr, p_scr, *, lam_init):
    h = pl.program_id(1)
    qi = pl.program_id(2)
    kb = pl.program_id(3)
    nk = pl.num_programs(3)
    hd = DA_HEAD_DIM
    n_split, hr, tk = s_scr.shape[0] // 2, s_scr.shape[1], s_scr.shape[2]
    kt = lax.rem(qi + kb, nk)
    flat = ((pl.program_id(0) * pl.num_programs(1) + h) * pl.num_programs(2) + qi) * nk + kb

    @pl.when(kb == 0)
    def _():
        m_scr[...] = jnp.full(m_scr.shape, -jnp.inf, F32)
        l_scr[...] = jnp.zeros(l_scr.shape, F32)
        acc_scr[...] = jnp.zeros(acc_scr.shape, F32)
        q = q_ref[...]
        qf = qf_ref[...]
        for m in range(2):
            qm = q[:, m * hd:(m + 1) * hd]
            qa_scr[m] = jnp.concatenate([qm, qf], axis=1)

    def step(diag):
        sgn = jnp.where(kt > qi, -1.0, 1.0).astype(BF16)
        k = k_ref[...]
        kf = kf_ref[...] * sgn
        v = v_ref[...]
        c2 = 2.0 * c_ref[h]
        for m in range(2):
            ka = jnp.concatenate([k[:, m * hd:(m + 1) * hd], kf], axis=1)
            for sp in range(n_split):
                s_scr[m * n_split + sp] = _nt_dot(qa_scr[m, sp * hr:(sp + 1) * hr, :], ka)
        nlt = tk // LANE
        blocks = [slice(rb * DA_ROW_BLK, (rb + 1) * DA_ROW_BLK) for rb in range(hr // DA_ROW_BLK)]
        for m in range(2):
            for sp in range(n_split):
                ci = m * n_split + sp
                rs = slice(sp * hr, (sp + 1) * hr)
                pmax = []
                for loc in blocks:
                    s = s_scr[ci, loc, :]
                    if diag:
                        s = s - c2 * u_ref[sp * hr + loc.start:sp * hr + loc.stop, :]
                        s_scr[ci, loc, :] = s
                    pm = s[:, 0:LANE]
                    for lt in range(1, nlt):
                        pm = jnp.maximum(pm, s[:, lt * LANE:(lt + 1) * LANE])
                    pmax.append(pm)
                m_old = m_scr[m, rs, :]
                m_new = jnp.maximum(m_old, jnp.max(jnp.concatenate(pmax, axis=0), axis=1, keepdims=True))
                alpha = jnp.exp2(m_old - m_new)
                m_scr[m, rs, :] = m_new
                psum = []
                for loc in blocks:
                    s = s_scr[ci, loc, :]
                    mb = m_new[loc]
                    acc_l = None
                    for lt in range(nlt):
                        p = jnp.exp2(s[:, lt * LANE:(lt + 1) * LANE] - mb)
                        acc_l = p if acc_l is None else acc_l + p
                        p_scr[ci, loc, lt * LANE:(lt + 1) * LANE] = p.astype(BF16)
                    psum.append(acc_l)
                row_sum = jnp.sum(jnp.concatenate(psum, axis=0), axis=1, keepdims=True)
                l_scr[m, rs, :] = alpha * l_scr[m, rs, :] + row_sum
                pv = jnp.dot(p_scr[ci], v, preferred_element_type=F32)
                acc_scr[m, rs, :] = jnp.concatenate([alpha, alpha], axis=1) * acc_scr[m, rs, :] + pv

    @pl.when(kb == 0)
    def _():
        step(True)

    @pl.when(jnp.logical_and(kb > 0, skip_ref[flat] == 0))
    def _():
        step(False)

    @pl.when(kb == nk - 1)
    def _():
        lv = lam_ref[...]
        lam = (jnp.exp(jnp.sum(lv[0:1] * lv[1:2], axis=1, keepdims=True))
               - jnp.exp(jnp.sum(lv[2:3] * lv[3:4], axis=1, keepdims=True)) + lam_init)
        inv = [1.0 / l_scr[m] for m in range(2)]
        o = (acc_scr[0] * jnp.concatenate([inv[0], inv[0]], axis=1)
             - lam * (acc_scr[1] * jnp.concatenate([inv[1], inv[1]], axis=1)))
        o = _rms(o) * g_ref[...] * (1.0 - lam_init)
        o_ref[...] = o.astype(o_ref.dtype)


def _alibi_features(t, n_heads):
    slopes = 2.0 ** (-8.0 * np.arange(1, n_heads + 1) / n_heads)
    c = jnp.asarray(slopes * LOG2E, F32)
    cp = c[:, None] * jnp.arange(t, dtype=F32)[None, :]
    p3 = jnp.stack(_bf16_parts(cp, 3), axis=-1)
    ones = jnp.ones((n_heads, t, 3), BF16)
    pad = jnp.zeros((n_heads, t, LANE - 6), BF16)
    qf = jnp.concatenate([-p3, ones, pad], axis=-1)
    kf = jnp.concatenate([ones, p3, pad], axis=-1)
    return c, qf, kf


DA_SKIP_MARGIN = 200.0


def _norms_kernel(x_ref, o_ref):
    x = x_ref[...].astype(F32)
    cols = []
    for g in range(x.shape[1] // LANE):
        v = x[:, g * LANE:(g + 1) * LANE]
        cols.append(jnp.max(jnp.sum(v * v, axis=1, keepdims=True), axis=0, keepdims=True))
    o_ref[...] = jnp.concatenate(cols, axis=1)


def _da_tile_norms(qkv, tile):
    bsz, t, _ = qkv.shape
    width = 2 * DA_HEADS * DA_VAL_DIM
    return pl.pallas_call(
        _norms_kernel,
        grid=(bsz, t // tile),
        in_specs=[pl.BlockSpec((None, tile, width), lambda b, i: (b, i, 0))],
        out_specs=pl.BlockSpec((None, None, 1, width // LANE), lambda b, i: (b, i, 0, 0)),
        out_shape=jax.ShapeDtypeStruct((bsz, t // tile, 1, width // LANE), F32),
        compiler_params=_cparams(("parallel", "parallel")),
        name="da_tile_norms",
    )(qkv)


def _da_skip_plan(norms2, c, tile):
    bsz, nt = norms2.shape[0], norms2.shape[1]
    nh = DA_HEADS
    nrm = jnp.sqrt(norms2.reshape(bsz, nt, 2, nh, 2)) * 1.001
    nq, nkk = nrm[:, :, 0], nrm[:, :, 1]
    qi = jnp.arange(nt)[:, None]
    j = jnp.arange(nt)[None, :]
    kt = (qi + j) % nt
    bound = jnp.max(nq[:, :, None] * (nkk[:, kt] + nkk[:, :, None]), axis=-1)
    dist_min = jnp.maximum(jnp.abs(kt - qi) - 1, 0) * tile + 1
    far = c[None, None, None, :] * dist_min[None, :, :, None].astype(F32) >= DA_SKIP_MARGIN + bound
    skip = jnp.logical_and(far, (j > 0)[None, :, :, None]).transpose(0, 3, 1, 2)
    kidx = jnp.broadcast_to(kt[None, None], skip.shape)
    cols = [kidx[..., 0]]
    for jj in range(1, nt):
        cols.append(jnp.where(skip[..., jj], cols[-1], kidx[..., jj]))
    kidx = jnp.stack(cols, axis=-1)
    return skip.astype(jnp.int32).reshape(-1), kidx.astype(jnp.int32).reshape(-1)


def _diff_attention(qkv, lam_vecs, subln_g, lam_init, *, tile):
    bsz, t, _ = qkv.shape
    nh = DA_HEADS
    tq = tk = min(tile, t)
    nq = t // tq
    n_split = 2 if tq >= 4 * DA_ROW_BLK else 1
    c, qf, kf = _alibi_features(t, nh)
    skip, kidx = _da_skip_plan(_da_tile_norms(qkv, tq), c, tq)
    ii = jnp.arange(tq, dtype=F32)
    u = jnp.maximum(ii[None, :] - ii[:, None], 0.0)

    def key_tile(b, h, i, j, kidx_ref):
        return kidx_ref[((b * nh + h) * nq + i) * nq + j]

    grid_spec = pltpu.PrefetchScalarGridSpec(
        num_scalar_prefetch=3,
        grid=(bsz, nh, nq, nq),
        in_specs=[
            pl.BlockSpec((None, tq, DA_VAL_DIM), lambda b, h, i, j, c, sk, ki: (b, i, h)),
            pl.BlockSpec((None, tk, DA_VAL_DIM), lambda b, h, i, j, c, sk, ki: (b, key_tile(b, h, i, j, ki), nh + h)),
            pl.BlockSpec((None, tk, DA_VAL_DIM),
                         lambda b, h, i, j, c, sk, ki: (b, key_tile(b, h, i, j, ki), 2 * nh + h)),
            pl.BlockSpec((None, tq, LANE), lambda b, h, i, j, c, sk, ki: (h, i, 0)),
            pl.BlockSpec((None, tk, LANE), lambda b, h, i, j, c, sk, ki: (h, key_tile(b, h, i, j, ki), 0)),
            pl.BlockSpec((tq, tk), lambda b, h, i, j, c, sk, ki: (0, 0)),
            pl.BlockSpec((4, DA_HEAD_DIM), lambda b, h, i, j, c, sk, ki: (0, 0)),
            pl.BlockSpec((1, DA_VAL_DIM), lambda b, h, i, j, c, sk, ki: (0, 0)),
        ],
        out_specs=pl.BlockSpec((None, tq, DA_VAL_DIM), lambda b, h, i, j, c, sk, ki: (b, i, h)),
        scratch_shapes=[
            pltpu.VMEM((2, tq, 2 * DA_HEAD_DIM), BF16),
            pltpu.VMEM((2, tq, LANE), F32),
            pltpu.VMEM((2, tq, LANE), F32),
            pltpu.VMEM((2, tq, DA_VAL_DIM), F32),
            pltpu.VMEM((2 * n_split, tq // n_split, tk), F32),
            pltpu.VMEM((2 * n_split, tq // n_split, tk), BF16),
        ],
    )
    return pl.pallas_call(
        functools.partial(_da_kernel, lam_init=lam_init),
        grid_spec=grid_spec,
        out_shape=jax.ShapeDtypeStruct((bsz, t, nh * DA_VAL_DIM), BF16),
        compiler_params=_cparams(("parallel", "parallel", "parallel", "arbitrary")),
        name="diff_attention",
    )(c, skip, kidx, qkv, qkv, qkv, qf, kf, u, lam_vecs, subln_g.reshape(1, DA_VAL_DIM))


def _gla_kernel(q_ref, k_ref, v_ref, gl_ref, w2h_ref, w2l_ref, gb_ref, o_ref, st_scr, *, reverse, nchunk):
    cs = GLA_CHUNK
    blk = nchunk * cs

    @pl.when(pl.program_id(2) == 0)
    def _():
        st_scr[...] = jnp.zeros(st_scr.shape, F32)

    ri = lax.broadcasted_iota(jnp.int32, (blk, blk), 0)
    ci = lax.broadcasted_iota(jnp.int32, (blk, blk), 1)
    same_chunk = (ri // cs) == (ci // cs)
    keep = jnp.logical_and(same_chunk, (ci >= ri) if reverse else (ci <= ri))
    tri = jnp.where(keep, 1.0, 0.0).astype(BF16)
    w2h = w2h_ref[...]
    gh, glo = _split2(gl_ref[...])
    x = (jnp.dot(gh, w2h, preferred_element_type=F32) + jnp.dot(glo, w2h, preferred_element_type=F32)
         + jnp.dot(gh, w2l_ref[...], preferred_element_type=F32) + gb_ref[...])
    la = (jnp.minimum(x, 0.0) - jnp.log1p(jnp.exp(-jnp.abs(x)))) * (1.0 / GLA_TAU)
    a1, a2, a3 = _split3(la)
    b = (jnp.dot(tri, a1, preferred_element_type=F32) + jnp.dot(tri, a2, preferred_element_type=F32)
         + jnp.dot(tri, a3, preferred_element_type=F32))
    q = q_ref[...].astype(F32)
    k = k_ref[...].astype(F32)
    v = v_ref[...]
    q_in = (q * jnp.exp(b)).astype(BF16)
    k_in = (k * jnp.exp(-b)).astype(BF16)
    att = jnp.where(keep, _nt_dot(q_in, k_in), 0.0)
    o_intra = jnp.dot(att.astype(BF16), v, preferred_element_type=F32)
    chunks = [slice(c * cs, (c + 1) * cs) for c in range(nchunk)]
    tots = [b[r.start:r.start + 1] if reverse else b[r.stop - 1:r.stop] for r in chunks]
    kvs = [_tn_dot(v[r], (k[r] * jnp.exp(tot - b[r])).astype(BF16)) for r, tot in zip(chunks, tots)]
    st = st_scr[...]
    for c in (range(nchunk - 1, -1, -1) if reverse else range(nchunk)):
        r = chunks[c]
        o_ref[r, :] = o_intra[r] + _nt_dot(q_in[r], st.astype(BF16))
        st = st * jnp.exp(tots[c]) + kvs[c]
    st_scr[...] = st


def _gla_scan(qkvr, g_lr, w2h, w2l, gb, *, reverse, blk):
    bsz, t, _ = qkvr.shape
    nh = GLA_HEADS
    blk = min(blk, t)
    nblk = t // blk
    pos = (lambda i: nblk - 1 - i) if reverse else (lambda i: i)
    kq = (nh * GLA_DK) // GLA_DK
    kv = (2 * nh * GLA_DK) // GLA_DV
    return pl.pallas_call(
        functools.partial(_gla_kernel, reverse=reverse, nchunk=blk // GLA_CHUNK),
        grid=(bsz, nh, nblk),
        in_specs=[
            pl.BlockSpec((None, blk, GLA_DK), lambda b, h, i: (b, pos(i), h)),
            pl.BlockSpec((None, blk, GLA_DK), lambda b, h, i: (b, pos(i), kq + h)),
            pl.BlockSpec((None, blk, GLA_DV), lambda b, h, i: (b, pos(i), kv + h)),
            pl.BlockSpec((None, blk, LANE), lambda b, h, i: (b, pos(i), 0)),
            pl.BlockSpec((LANE, GLA_DK), lambda b, h, i: (0, h)),
            pl.BlockSpec((LANE, GLA_DK), lambda b, h, i: (0, h)),
            pl.BlockSpec((1, GLA_DK), lambda b, h, i: (0, h)),
        ],
        out_specs=pl.BlockSpec((None, blk, GLA_DV), lambda b, h, i: (b, pos(i), h)),
        out_shape=jax.ShapeDtypeStruct((bsz, t, nh * GLA_DV), F32),
        scratch_shapes=[pltpu.VMEM((GLA_DV, GLA_DK), F32)],
        compiler_params=_cparams(("parallel", "parallel", "arbitrary")),
        name="gla_scan_bwd" if reverse else "gla_scan_fwd",
    )(qkvr, qkvr, qkvr, g_lr, w2h, w2l, gb)


def _gla_fin_kernel(of_ref, ob_ref, r_ref, g_ref, o_ref):
    o = _rms(of_ref[...] + ob_ref[...]) * g_ref[...]
    r = r_ref[...].astype(F32)
    o_ref[...] = (o * (r * jax.nn.sigmoid(r))).astype(o_ref.dtype)


def _gla_finish(o_f, o_b, qkvr, norm_g, *, tm):
    bsz, t, _ = o_f.shape
    nh = GLA_HEADS
    tm = min(tm, t)
    kr = (2 * nh * GLA_DK + nh * GLA_DV) // GLA_DV
    return pl.pallas_call(
        _gla_fin_kernel,
        grid=(bsz, t // tm, nh),
        in_specs=[
            pl.BlockSpec((None, tm, GLA_DV), lambda b, i, h: (b, i, h)),
            pl.BlockSpec((None, tm, GLA_DV), lambda b, i, h: (b, i, h)),
            pl.BlockSpec((None, tm, GLA_DV), lambda b, i, h: (b, i, kr + h)),
            pl.BlockSpec((1, GLA_DV), lambda b, i, h: (0, 0)),
        ],
        out_specs=pl.BlockSpec((None, tm, GLA_DV), lambda b, i, h: (b, i, h)),
        out_shape=jax.ShapeDtypeStruct((bsz, t, nh * GLA_DV), BF16),
        compiler_params=_cparams(("parallel", "parallel", "parallel")),
        name="gla_finish",
    )(o_f, o_b, qkvr, norm_g.reshape(1, GLA_DV))


def _na_kernel(q_ref, kp, km, kn, vp, vm, vn, tbl_ref, o_ref, kcat, vcat, *, n_rows):
    i = pl.program_id(2)
    w = GRID_W
    halo = (NA_ROWS // 2) * w
    main = NA_QROWS * w
    for ref_p, ref_m, ref_n, cat in ((kp, km, kn, kcat), (vp, vm, vn, vcat)):
        cat[0:halo, :] = ref_p[...]
        cat[halo:halo + main, :] = ref_m[...]
        cat[halo + main:2 * halo + main, :] = ref_n[...]
    lane_head = lax.broadcasted_iota(jnp.int32, (w, LANE), 1) // NA_HEAD_DIM
    base = i * NA_QROWS
    for rr in range(NA_QROWS):
        r = base + rr
        r0 = jnp.clip(r - NA_ROWS // 2, 0, n_rows - NA_ROWS)
        start = pl.multiple_of((NA_ROWS // 2 + r0 - base) * w, w)
        delta = r - r0
        qr = q_ref[rr * w:(rr + 1) * w, :]
        qs = jnp.concatenate([jnp.where(lane_head == hh, qr, jnp.zeros_like(qr)) for hh in range(NA_GROUP)], axis=0)
        kb = kcat[pl.ds(start, NA_ROWS * w), :]
        vb = vcat[pl.ds(start, NA_ROWS * w), :]
        s = _nt_dot(qs, kb)
        bias = jnp.concatenate(
            [tbl_ref[2 * j - delta + NA_ROWS - 1].reshape(NA_GROUP * w, 2 * w) for j in range(NA_ROWS // 2)], axis=1)
        s = s + bias
        p = jnp.exp2(s - jnp.max(s, axis=1, keepdims=True))
        l = jnp.sum(p, axis=1, keepdims=True)
        o4 = jnp.dot(p.astype(BF16), vb, preferred_element_type=F32) / l
        o = jnp.zeros((w, LANE), F32)
        for hh in range(NA_GROUP):
            o = jnp.where(lane_head == hh, o4[hh * w:(hh + 1) * w, :], o)
        o_ref[rr * w:(rr + 1) * w, :] = o.astype(o_ref.dtype)


def _na_bias_table(rpb):
    col = np.arange(GRID_W)
    col_start = np.clip(col - NA_COLS // 2, 0, GRID_W - NA_COLS)
    in_window = (col[None, :] >= col_start[:, None]) & (col[None, :] < col_start[:, None] + NA_COLS)
    col_off = np.clip(col[None, :] - col[:, None] + NA_COLS - 1, 0, 2 * NA_COLS - 2)
    cb = jnp.where(in_window[None, None], rpb.astype(F32)[:, :, col_off] * LOG2E, NEG_INF)
    pair = jnp.concatenate([cb[:, :-1], cb[:, 1:]], axis=-1)
    n_pair = 2 * NA_ROWS - 2
    pair = pair.reshape(NA_HEADS // NA_GROUP, NA_GROUP, n_pair, GRID_W, 2 * GRID_W)
    return pair.transpose(0, 2, 1, 3, 4)


def _neighborhood_attention(qkv, rpb):
    bsz, t, _ = qkv.shape
    n_rows = t // GRID_W
    ng = NA_HEADS // NA_GROUP
    tq = NA_QROWS * GRID_W
    halo = (NA_ROWS // 2) * GRID_W
    per = tq // halo
    n_halo = t // halo
    tbl = _na_bias_table(rpb)

    def kv_specs(col0):
        return [
            pl.BlockSpec((None, halo, LANE), lambda g, b, i: (b, jnp.maximum(i * per - 1, 0), col0 + g)),
            pl.BlockSpec((None, tq, LANE), lambda g, b, i: (b, i, col0 + g)),
            pl.BlockSpec((None, halo, LANE), lambda g, b, i: (b, jnp.minimum((i + 1) * per, n_halo - 1), col0 + g)),
        ]

    cat_rows = tq + 2 * halo
    return pl.pallas_call(
        functools.partial(_na_kernel, n_rows=n_rows),
        grid=(ng, bsz, t // tq),
        in_specs=[pl.BlockSpec((None, tq, LANE), lambda g, b, i: (b, i, g))] + kv_specs(ng) + kv_specs(2 * ng)
        + [pl.BlockSpec((None, 2 * NA_ROWS - 2, NA_GROUP, GRID_W, 2 * GRID_W), lambda g, b, i: (g, 0, 0, 0, 0))],
        out_specs=pl.BlockSpec((None, tq, LANE), lambda g, b, i: (b, i, g)),
        out_shape=jax.ShapeDtypeStruct((bsz, t, NA_HEADS * NA_HEAD_DIM), BF16),
        scratch_shapes=[pltpu.VMEM((cat_rows, LANE), BF16), pltpu.VMEM((cat_rows, LANE), BF16)],
        compiler_params=_cparams(("parallel", "parallel", "parallel")),
        name="neighborhood_attention",
    )(qkv, *([qkv] * 6), tbl)


def _post_kernel(o_ref, w_ref, x_ref, g1_ref, n2_ref, sc_ref, sh_ref, wrh_ref, wrl_ref, xn_ref, hf_ref, lg_ref):
    y = jnp.dot(o_ref[...], w_ref[...], preferred_element_type=F32)
    xn = x_ref[...] + g1_ref[...] * y
    xn_ref[...] = xn
    hf = _rms(xn) * n2_ref[...] * (1.0 + sc_ref[...]) + sh_ref[...]
    tm = hf.shape[0]
    for j in range(hf.shape[1] // LANE):
        hf_ref[pl.ds(j, tm, stride=MOE_SLAB), :] = hf[:, j * LANE:(j + 1) * LANE]
    hh, hl = _split2(hf)
    wrh = wrh_ref[...]
    lg_ref[...] = (jnp.dot(hh, wrh, preferred_element_type=F32) + jnp.dot(hl, wrh, preferred_element_type=F32)
                   + jnp.dot(hh, wrl_ref[...], preferred_element_type=F32))


def _post_mixer(o, w_out, x, g1, n2g, sc2, sh2, wr_hi, wr_lo, *, tm):
    bsz, t, d = x.shape
    kdim = o.shape[-1]
    tm = min(tm, t)
    vec = pl.BlockSpec((None, 1, d), lambda b, i: (b, 0, 0))
    tile = pl.BlockSpec((None, tm, d), lambda b, i: (b, i, 0))
    return pl.pallas_call(
        _post_kernel,
        grid=(bsz, t // tm),
        in_specs=[
            pl.BlockSpec((None, tm, kdim), lambda b, i: (b, i, 0)),
            pl.BlockSpec((kdim, d), lambda b, i: (0, 0)),
            tile, vec,
            pl.BlockSpec((1, d), lambda b, i: (0, 0)),
            vec, vec,
            pl.BlockSpec((d, LANE), lambda b, i: (0, 0)),
            pl.BlockSpec((d, LANE), lambda b, i: (0, 0)),
        ],
        out_specs=[tile, pl.BlockSpec((tm * MOE_SLAB, LANE), lambda b, i: (b * (t // tm) + i, 0)),
                   pl.BlockSpec((None, tm, LANE), lambda b, i: (b, i, 0))],
        out_shape=[jax.ShapeDtypeStruct((bsz, t, d), F32), jax.ShapeDtypeStruct((bsz * t * MOE_SLAB, LANE), F32),
                   jax.ShapeDtypeStruct((bsz, t, LANE), F32)],
        compiler_params=_cparams(("parallel", "parallel")),
        name="post_mixer",
    )(o, w_out, x, g1, n2g.reshape(1, d), sc2, sh2, wr_hi, wr_lo)


ROUTER_ROWS = 8 + MOE_EXPERTS


def _router_kernel(lt_ref, b_ref, id_ref, w_ref):
    lt = lt_ref[...] + b_ref[...]
    tn = lt.shape[1]
    lg = lt[0:MOE_GROUPS]
    e = jnp.exp(lg - jnp.max(lg, axis=0, keepdims=True))
    gp = e / jnp.sum(e, axis=0, keepdims=True)
    g_p = jnp.max(gp, axis=0, keepdims=True)
    rg = lax.broadcasted_iota(jnp.int32, (MOE_GROUPS, tn), 0)
    g_idx = jnp.min(jnp.where(gp == g_p, rg, MOE_GROUPS), axis=0, keepdims=True)
    el = jnp.zeros((MOE_EPG, tn), F32)
    for g in range(MOE_GROUPS):
        el = jnp.where(g_idx == g, lt[8 + g * MOE_EPG:8 + (g + 1) * MOE_EPG], el)
    ee = jnp.exp(el - jnp.max(el, axis=0, keepdims=True))
    ep = ee / jnp.sum(ee, axis=0, keepdims=True)
    re = lax.broadcasted_iota(jnp.int32, (MOE_EPG, tn), 0)
    p1 = jnp.max(ep, axis=0, keepdims=True)
    i1 = jnp.min(jnp.where(ep == p1, re, MOE_EPG), axis=0, keepdims=True)
    ep2 = jnp.where(re == i1, -1.0, ep)
    p2 = jnp.max(ep2, axis=0, keepdims=True)
    i2 = jnp.min(jnp.where(ep2 == p2, re, MOE_EPG), axis=0, keepdims=True)
    den = p1 + p2
    id_ref[0:1, :] = g_idx * MOE_EPG + i1
    id_ref[1:2, :] = g_idx * MOE_EPG + i2
    w_ref[0:1, :] = g_p * (p1 / den)
    w_ref[1:2, :] = g_p * (p2 / den)


def _router(logits_t, bias_col, *, tn):
    n = logits_t.shape[1]
    tn = min(tn, n)
    return pl.pallas_call(
        _router_kernel,
        grid=(n // tn,),
        in_specs=[pl.BlockSpec((ROUTER_ROWS, tn), lambda i: (0, i)),
                  pl.BlockSpec((ROUTER_ROWS, 1), lambda i: (0, 0))],
        out_specs=[pl.BlockSpec((MOE_TOP_K, tn), lambda i: (0, i)), pl.BlockSpec((MOE_TOP_K, tn), lambda i: (0, i))],
        out_shape=[jax.ShapeDtypeStruct((MOE_TOP_K, n), jnp.int32), jax.ShapeDtypeStruct((MOE_TOP_K, n), F32)],
        compiler_params=_cparams(("parallel",)),
        name="router_topk",
    )(logits_t, bias_col)


MOE_DMA_UNROLL = 8


def _moe_kernel(pb_ref, pe_ref, r0_ref, r1_ref, fl_ref, np_ref, src_cur, src_nxt, dst_prv, dst_lst, hf_hbm,
                wgu_ref, wd_ref, y_hbm, wgu_bf, wd_bf, xbuf, obuf, gsem, ssem, *, n_blocks):
    s = pl.program_id(0)
    blk = xbuf.shape[1] // MOE_PITCH
    ff = wd_ref.shape[0]
    b = pb_ref[s]
    slot = b % 2
    flags = fl_ref[s]
    active = s < np_ref[0]
    first = jnp.logical_and(active, (flags & 1) != 0)
    new_w = jnp.logical_and(active, (flags & 4) != 0)

    def gather_copy(src_ref, sl, r):
        return pltpu.make_async_copy(hf_hbm.at[pl.ds(pl.multiple_of(src_ref[0, r] * MOE_SLAB, MOE_SLAB), MOE_SLAB)],
                                     xbuf.at[sl, pl.ds(r * MOE_PITCH, MOE_SLAB)], gsem.at[sl])

    def scatter_copy(dst_ref, sl, r):
        return pltpu.make_async_copy(obuf.at[sl, pl.ds(r * MOE_PITCH, MOE_SLAB)],
                                     y_hbm.at[pl.ds(pl.multiple_of(dst_ref[0, r] * MOE_SLAB, MOE_SLAB), MOE_SLAB)],
                                     ssem.at[sl])

    def issue_loop(make):
        def body(g, carry):
            for j in range(MOE_DMA_UNROLL):
                make(g * MOE_DMA_UNROLL + j).start(priority=j % 2)
            return carry
        lax.fori_loop(0, blk // MOE_DMA_UNROLL, body, 0)

    def wait_gather(sl):
        pltpu.make_async_copy(hf_hbm.at[pl.ds(0, blk * MOE_SLAB)], xbuf.at[sl, pl.ds(0, blk * MOE_SLAB)],
                              gsem.at[sl]).wait()

    def wait_scatter(sl):
        pltpu.make_async_copy(obuf.at[sl, pl.ds(0, blk * MOE_SLAB)], y_hbm.at[pl.ds(0, blk * MOE_SLAB)],
                              ssem.at[sl]).wait()

    @pl.when(s == 0)
    def _():
        obuf[...] = jnp.zeros(obuf.shape, F32)
        issue_loop(lambda r: gather_copy(src_cur, 0, r))

    @pl.when(first)
    def _():
        wait_gather(slot)

        @pl.when(b >= 2)
        def _():
            wait_scatter(slot)

    @pl.when(new_w)
    def _():
        wgu_bf[...] = wgu_ref[...].astype(BF16)
        wd_bf[...] = wd_ref[...].astype(BF16)

    def compute():
        x = jnp.concatenate([xbuf[slot, pl.ds(j, blk, stride=MOE_PITCH), :] for j in range(MOE_SLAB)],
                            axis=1).astype(BF16)
        hgu = jnp.dot(x, wgu_bf[...], preferred_element_type=F32)
        g = hgu[:, :ff]
        u = hgu[:, ff:]
        act = (g * jax.nn.sigmoid(g) * u).astype(BF16)
        y = jnp.dot(act, wd_bf[...], preferred_element_type=F32)
        ri = lax.broadcasted_iota(jnp.int32, (blk, 1), 0)
        mine = jnp.logical_and(ri >= r0_ref[s], ri < r1_ref[s])
        for j in range(MOE_SLAB):
            rows = pl.ds(j, blk, stride=MOE_PITCH)
            obuf[slot, rows, :] = jnp.where(mine, y[:, j * LANE:(j + 1) * LANE], obuf[slot, rows, :])

    @pl.when(first)
    def _():
        do_gather = b + 1 < n_blocks
        do_scatter = b >= 1
        for r in range(blk):
            @pl.when(do_gather)
            def _():
                gather_copy(src_nxt, 1 - slot, r).start(priority=r % 2)

            @pl.when(do_scatter)
            def _():
                scatter_copy(dst_prv, 1 - slot, r).start(priority=(r + 1) % 2)

        compute()

    @pl.when(jnp.logical_and(active, jnp.logical_not(first)))
    def _():
        compute()

    @pl.when(s == pl.num_programs(0) - 1)
    def _():
        last_slot = (n_blocks - 1) % 2
        issue_loop(lambda r: scatter_copy(dst_lst, last_slot, r))
        wait_scatter(0)
        wait_scatter(1)


def _moe_plan(expert_id, blk):
    flat_e = expert_id.reshape(-1)
    n_asg = flat_e.shape[0]
    n_blocks = n_asg // blk
    n_steps = n_blocks + MOE_EXPERTS - 1
    _, asg = lax.sort_key_val(flat_e, jnp.arange(n_asg, dtype=jnp.int32))
    counts = jnp.sum((flat_e[:, None] == jnp.arange(MOE_EXPERTS, dtype=jnp.int32)[None, :]).astype(jnp.int32), axis=0)
    ends = jnp.cumsum(counts)
    starts = ends - counts
    src_tok = asg // MOE_TOP_K
    dst_row = (asg % MOE_TOP_K) * (n_asg // MOE_TOP_K) + asg // MOE_TOP_K
    bidx = jnp.arange(n_blocks, dtype=jnp.int32)
    e_lo = jnp.searchsorted(ends, bidx * blk, side='right').astype(jnp.int32)
    e_hi = jnp.searchsorted(ends, (bidx + 1) * blk - 1, side='right').astype(jnp.int32)
    n_pair_b = e_hi - e_lo + 1
    pair_end = jnp.cumsum(n_pair_b)
    pair_start = pair_end - n_pair_b
    n_pairs = pair_end[-1]
    sidx = jnp.arange(n_steps, dtype=jnp.int32)
    pb = jnp.minimum(jnp.searchsorted(pair_end, sidx, side='right').astype(jnp.int32), n_blocks - 1)
    pe = jnp.clip(e_lo[pb] + sidx - pair_start[pb], 0, MOE_EXPERTS - 1)
    pe = jnp.where(sidx < n_pairs, pe, pe[jnp.maximum(n_pairs - 1, 0)])
    r0 = jnp.clip(starts[pe] - pb * blk, 0, blk)
    r1 = jnp.clip(ends[pe] - pb * blk, 0, blk)
    prev_e = jnp.concatenate([jnp.full((1,), -1, jnp.int32), pe[:-1]])
    flags = ((sidx == pair_start[pb]).astype(jnp.int32) + 2 * (sidx == pair_end[pb] - 1).astype(jnp.int32)
             + 4 * (pe != prev_e).astype(jnp.int32))
    flags = jnp.where(sidx < n_pairs, flags, 0)
    i32 = lambda a: a.astype(jnp.int32)
    return (i32(pb), i32(pe), i32(r0), i32(r1), i32(flags), i32(n_pairs).reshape(1),
            src_tok.reshape(n_blocks, 1, blk), dst_row.reshape(n_blocks, 1, blk))


def _moe_experts(hf, expert_id, w_gu, w_d, layer, *, blk):
    n_tok, d = hf.shape[0] // MOE_SLAB, hf.shape[1] * MOE_SLAB
    n_asg = n_tok * MOE_TOP_K
    blk = min(blk, n_asg // 2)
    n_blocks = n_asg // blk
    pb, pe, r0, r1, flags, n_pairs, src_tok, dst_row = _moe_plan(expert_id, blk)
    ff = w_d.shape[2]
    smem = functools.partial(pl.BlockSpec, memory_space=pltpu.SMEM)
    grid_spec = pltpu.PrefetchScalarGridSpec(
        num_scalar_prefetch=6,
        grid=(pb.shape[0],),
        in_specs=[
            smem((None, 1, blk), lambda s, pb, *_: (pb[s], 0, 0)),
            smem((None, 1, blk), lambda s, pb, *_: (jnp.minimum(pb[s] + 1, n_blocks - 1), 0, 0)),
            smem((None, 1, blk), lambda s, pb, *_: (jnp.maximum(pb[s] - 1, 0), 0, 0)),
            smem((None, 1, blk), lambda s, pb, *_: (n_blocks - 1, 0, 0)),
            pl.BlockSpec(memory_space=pl.ANY),
            pl.BlockSpec((None, None, d, 2 * ff), lambda s, pb, pe, *_: (layer, pe[s], 0, 0)),
            pl.BlockSpec((None, None, ff, d), lambda s, pb, pe, *_: (layer, pe[s], 0, 0)),
        ],
        out_specs=pl.BlockSpec(memory_space=pl.ANY),
        scratch_shapes=[
            pltpu.VMEM((d, 2 * ff), BF16),
            pltpu.VMEM((ff, d), BF16),
            pltpu.VMEM((2, blk * MOE_PITCH, LANE), F32),
            pltpu.VMEM((2, blk * MOE_PITCH, LANE), F32),
            pltpu.SemaphoreType.DMA((2,)),
            pltpu.SemaphoreType.DMA((2,)),
        ],
    )
    return pl.pallas_call(
        functools.partial(_moe_kernel, n_blocks=n_blocks),
        grid_spec=grid_spec,
        out_shape=jax.ShapeDtypeStruct((n_asg * MOE_SLAB, LANE), F32),
        compiler_params=_cparams(("arbitrary",)),
        name="moe_experts",
    )(pb, pe, r0, r1, flags, n_pairs, src_tok, src_tok, dst_row, dst_row, hf, w_gu, w_d)


def _combine_kernel(x_ref, y0_ref, y1_ref, w_ref, g2_ref, fg_ref, o_ref, *, final):
    w = w_ref[...]
    tm = w.shape[0]

    def rows_to_lanes(y_ref):
        return jnp.concatenate([y_ref[pl.ds(j, tm, stride=MOE_SLAB), :] for j in range(MOE_SLAB)], axis=1)

    moe = w[:, 0:1] * rows_to_lanes(y0_ref) + w[:, 1:2] * rows_to_lanes(y1_ref)
    xn = x_ref[...] + g2_ref[...] * moe
    if final:
        xn = _rms(xn) * fg_ref[...]
    o_ref[...] = xn


def _combine(x, ybuf, wts, g2, final_g, *, final, tm):
    bsz, t, d = x.shape
    tm = min(tm, t)
    nt = t // tm
    nb = bsz * nt
    return pl.pallas_call(
        functools.partial(_combine_kernel, final=final),
        grid=(bsz, nt),
        in_specs=[
            pl.BlockSpec((None, tm, d), lambda b, i: (b, i, 0)),
            pl.BlockSpec((tm * MOE_SLAB, LANE), lambda b, i: (b * nt + i, 0)),
            pl.BlockSpec((tm * MOE_SLAB, LANE), lambda b, i: (nb + b * nt + i, 0)),
            pl.BlockSpec((tm, MOE_TOP_K), lambda b, i: (b * nt + i, 0)),
            pl.BlockSpec((None, 1, d), lambda b, i: (b, 0, 0)),
            pl.BlockSpec((1, d), lambda b, i: (0, 0)),
        ],
        out_specs=pl.BlockSpec((None, tm, d), lambda b, i: (b, i, 0)),
        out_shape=jax.ShapeDtypeStruct((bsz, t, d), F32),
        compiler_params=_cparams(("parallel", "parallel")),
        name="moe_combine_final" if final else "moe_combine",
    )(x, ybuf, ybuf, wts, g2, final_g.reshape(1, d))


def _router_weights(rg_w, rg_b, re_w, re_b):
    d = rg_w.shape[0]
    wr = jnp.zeros((d, LANE), F32).at[:, 0:MOE_GROUPS].set(rg_w).at[:, 8:8 + MOE_EXPERTS].set(re_w)
    bias = jnp.zeros((ROUTER_ROWS, 1), F32).at[0:MOE_GROUPS, 0].set(rg_b).at[8:, 0].set(re_b)
    return tuple(_bf16_parts(wr, 2)) + (bias,)


def _col_scale(n, n_scaled, scale):
    return jnp.concatenate([jnp.full((n_scaled,), scale, F32), jnp.ones((n - n_scaled,), F32)])


def kernel(x, c, ada_w, ada_b, norm1_g, norm2_g, router_g_w, router_g_b, router_e_w, router_e_b, moe_w_gu, moe_w_d, da_w_in, da_w_out, da_lam_q1, da_lam_k1, da_lam_q2, da_lam_k2, da_subln_g, gla_w_in, gla_gate_w2_f, gla_gate_b_f, gla_gate_w2_b, gla_gate_b_b, gla_norm_g, gla_w_out, na_w_in, na_rpb, na_w_out, final_g):
    bsz, t, d = x.shape
    n_tok = bsz * t
    mod = _ada_mod(c, ada_w, ada_b)
    wts = ybuf = g2 = None
    for i in range(DEPTH):
        sh1, sc1, g1, sh2, sc2, g2_i = [mod[i, :, None, m * d:(m + 1) * d] for m in range(6)]
        if i > 0:
            x = _combine(x, ybuf, wts, g2, final_g, final=False, tm=512)
        kind, j = i % N_MIXERS, i // N_MIXERS
        if kind == 0:
            w_in = da_w_in[j].astype(BF16)
            cs = _col_scale(w_in.shape[1], DA_HEADS * DA_VAL_DIM, DA_HEAD_DIM ** -0.5 * LOG2E)
            qkv = _norm_proj(x, norm1_g[i], sc1, sh1, w_in, cs, tm=1024, tn=1024, out_dtype=BF16)
            lam_vecs = jnp.stack([da_lam_q1[j], da_lam_k1[j], da_lam_q2[j], da_lam_k2[j]]).astype(F32)
            lam_init = 0.8 - 0.6 * math.exp(-0.3 * i)
            o = _diff_attention(qkv, lam_vecs, da_subln_g[j], lam_init, tile=1024)
            w_out = da_w_out[j]
        elif kind == 1:
            n_main = 2 * GLA_HEADS * GLA_DK + 2 * GLA_HEADS * GLA_DV
            w_main = gla_w_in[j][:, :n_main].astype(BF16)
            w_gate = jnp.zeros((d, LANE), F32).at[:, :2 * GLA_GATE_RANK].set(gla_w_in[j][:, n_main:]).astype(BF16)
            cs = _col_scale(n_main, GLA_HEADS * GLA_DK, GLA_DK ** -0.5)
            qkvr = _norm_proj(x, norm1_g[i], sc1, sh1, w_main, cs, tm=1024, tn=1024, out_dtype=BF16)
            g_lr = _norm_proj(x, norm1_g[i], sc1, sh1, w_gate, jnp.ones((LANE,), F32), tm=1024, tn=LANE,
                              out_dtype=F32)
            outs = []
            for rev, w2, gb in ((False, gla_gate_w2_f[j], gla_gate_b_f[j]), (True, gla_gate_w2_b[j], gla_gate_b_b[j])):
                r0 = GLA_GATE_RANK if rev else 0
                w2p = jnp.zeros((LANE, w2.shape[1]), F32).at[r0:r0 + GLA_GATE_RANK].set(w2)
                w2h, w2l = _bf16_parts(w2p, 2)
                outs.append(_gla_scan(qkvr, g_lr, w2h, w2l, gb.reshape(1, -1).astype(F32), reverse=rev, blk=512))
            o = _gla_finish(outs[0], outs[1], qkvr, gla_norm_g[j], tm=512)
            w_out = gla_w_out[j]
        else:
            w_in = na_w_in[j].astype(BF16)
            cs = _col_scale(w_in.shape[1], NA_HEADS * NA_HEAD_DIM, NA_HEAD_DIM ** -0.5 * LOG2E)
            qkv = _norm_proj(x, norm1_g[i], sc1, sh1, w_in, cs, tm=1024, tn=1024, out_dtype=BF16)
            o = _neighborhood_attention(qkv, na_rpb[j])
            w_out = na_w_out[j]
        wr_hi, wr_lo, r_bias = _router_weights(router_g_w[i], router_g_b[i], router_e_w[i], router_e_b[i])
        x, hf, logits = _post_mixer(o, w_out.astype(BF16), x, g1, norm2_g[i], sc2, sh2, wr_hi, wr_lo, tm=256)
        logits_t = logits.reshape(n_tok, LANE).T[:ROUTER_ROWS]
        ids, wt = _router(logits_t, r_bias, tn=2048)
        ybuf = _moe_experts(hf, ids.T, moe_w_gu, moe_w_d, i, blk=MOE_BLK)
        wts = wt.T
        g2 = g2_i
    return _combine(x, ybuf, wts, g2, final_g, final=True, tm=512)
```

```python
import functools
import math

import numpy as np
import jax
import jax.numpy as jnp
from jax import lax
from jax.experimental import pallas as pl
from jax.experimental.pallas import tpu as pltpu

F32 = jnp.float32
BF16 = jnp.bfloat16

D_MODEL = 2048
DEPTH = 4
N_MIXERS = 3
NORM_EPS = 1e-6
NEG_INF = -1e30
LOG2E = 1.4426950408889634

DA_HEADS = 8
DA_HEAD_DIM = 128
DA_VAL_DIM = 2 * DA_HEAD_DIM
DA_ROW_BLK = 16

GLA_HEADS = 4
GLA_DK = 256
GLA_DV = 512
GLA_GATE_RANK = 16
GLA_TAU = 16.0
GLA_CHUNK = 64

GRID_W = 64
NA_HEADS = 64
NA_HEAD_DIM = 32
NA_ROWS = 8
NA_COLS = 16
NA_GROUP = 4
NA_QROWS = 64

MOE_GROUPS = 4
MOE_EPG = 8
MOE_EXPERTS = 32
MOE_TOP_K = 2
MOE_D_FF = 512
MOE_BLK = 256
MOE_SLAB = D_MODEL // 128
MOE_PITCH = 20

LANE = 128
VMEM_LIMIT = 56 * 1024 * 1024


def _cparams(sem):
    return pltpu.CompilerParams(dimension_semantics=sem, vmem_limit_bytes=VMEM_LIMIT)


def _split2(v):
    hi = v.astype(BF16)
    lo = (v - hi.astype(F32)).astype(BF16)
    return hi, lo


def _split3(v):
    hi = v.astype(BF16)
    r = v - hi.astype(F32)
    mid = r.astype(BF16)
    lo = (r - mid.astype(F32)).astype(BF16)
    return hi, mid, lo


def _bf16_parts(v, n):
    parts = []
    r = v.astype(F32)
    for _ in range(n):
        top = lax.bitcast_convert_type(
            lax.bitcast_convert_type(r, jnp.uint32) & jnp.uint32(0xFFFF0000), F32)
        parts.append(top.astype(BF16))
        r = r - top
    return parts


def _nt_dot(a, b):
    return lax.dot_general(a, b, (((1,), (1,)), ((), ())), preferred_element_type=F32)


def _tn_dot(a, b):
    return lax.dot_general(a, b, (((0,), (0,)), ((), ())), preferred_element_type=F32)


def _rms(x):
    return x * lax.rsqrt(jnp.mean(x * x, axis=-1, keepdims=True) + NORM_EPS)


def _ada_kernel(c_ref, w_ref, b_ref, o_ref):
    c = c_ref[...]
    cond = (c * jax.nn.sigmoid(c)).astype(BF16)
    o_ref[...] = jnp.dot(cond, w_ref[...].astype(BF16), preferred_element_type=F32) + b_ref[...]


def _ada_mod(c, ada_w, ada_b):
    depth, d, n = ada_w.shape
    bsz = c.shape[0]
    rows = 8
    c8 = jnp.zeros((rows, d), F32).at[:bsz].set(c)
    tn = 1024
    out = pl.pallas_call(
        _ada_kernel,
        grid=(depth, n // tn),
        in_specs=[
            pl.BlockSpec((rows, d), lambda l, j: (0, 0)),
            pl.BlockSpec((None, d, tn), lambda l, j: (l, 0, j)),
            pl.BlockSpec((None, 1, tn), lambda l, j: (l, 0, j)),
        ],
        out_specs=pl.BlockSpec((None, rows, tn), lambda l, j: (l, 0, j)),
        out_shape=jax.ShapeDtypeStruct((depth, rows, n), F32),
        compiler_params=_cparams(("parallel", "parallel")),
        name="ada_mod",
    )(c8, ada_w, ada_b.reshape(depth, 1, n))
    return out[:, :bsz]


def _proj_kernel(x_ref, g_ref, sc_ref, sh_ref, w_ref, cs_ref, o_ref, h_scr):
    @pl.when(pl.program_id(2) == 0)
    def _():
        h = _rms(x_ref[...]) * g_ref[...] * (1.0 + sc_ref[...]) + sh_ref[...]
        h_scr[...] = h.astype(BF16)

    acc = jnp.dot(h_scr[...], w_ref[...], preferred_element_type=F32)
    o_ref[...] = (acc * cs_ref[...]).astype(o_ref.dtype)


def _norm_proj(x, g, sc, sh, w, col_scale, *, tm, tn, out_dtype):
    bsz, t, d = x.shape
    n = w.shape[1]
    tm = min(tm, t)
    return pl.pallas_call(
        _proj_kernel,
        grid=(bsz, t // tm, n // tn),
        in_specs=[
            pl.BlockSpec((None, tm, d), lambda b, i, j: (b, i, 0)),
            pl.BlockSpec((1, d), lambda b, i, j: (0, 0)),
            pl.BlockSpec((None, 1, d), lambda b, i, j: (b, 0, 0)),
            pl.BlockSpec((None, 1, d), lambda b, i, j: (b, 0, 0)),
            pl.BlockSpec((d, tn), lambda b, i, j: (0, j)),
            pl.BlockSpec((1, tn), lambda b, i, j: (0, j)),
        ],
        out_specs=pl.BlockSpec((None, tm, tn), lambda b, i, j: (b, i, j)),
        out_shape=jax.ShapeDtypeStruct((bsz, t, n), out_dtype),
        scratch_shapes=[pltpu.VMEM((tm, d), BF16)],
        compiler_params=_cparams(("parallel", "parallel", "arbitrary")),
        name="norm_proj",
    )(x, g.reshape(1, d), sc, sh, w, col_scale.reshape(1, n))


def _da_kernel(c_ref, skip_ref, kidx_ref, q_ref, k_ref, v_ref, qf_ref, kf_ref, u_ref, lam_ref, g_ref, o_ref,
               qa_scr, m_scr, l_scr, acc_scr, s_scr, p_scr, *, lam_init):
    h = pl.program_id(1)
    qi = pl.program_id(2)
    kb = pl.program_id(3)
    nk = pl.num_programs(3)
    hd = DA_HEAD_DIM
    n_split, hr, tk = s_scr.shape[0] // 2, s_scr.shape[1], s_scr.shape[2]
    kt = lax.rem(qi + kb, nk)
    flat = ((pl.program_id(0) * pl.num_programs(1) + h) * pl.num_programs(2) + qi) * nk + kb

    @pl.when(kb == 0)
    def _():
        m_scr[...] = jnp.full(m_scr.shape, -jnp.inf, F32)
        l_scr[...] = jnp.zeros(l_scr.shape, F32)
        acc_scr[...] = jnp.zeros(acc_scr.shape, F32)
        q = q_ref[...]
        qf = qf_ref[...]
        for m in range(2):
            qm = q[:, m * hd:(m + 1) * hd]
            qa_scr[m] = jnp.concatenate([qm, qf], axis=1)

    def step(diag):
        sgn = jnp.where(kt > qi, -1.0, 1.0).astype(BF16)
        k = k_ref[...]
        kf = kf_ref[...] * sgn
        v = v_ref[...]
        c2 = 2.0 * c_ref[h]
        for m in range(2):
            ka = jnp.concatenate([k[:, m * hd:(m + 1) * hd], kf], axis=1)
            for sp in range(n_split):
                s_scr[m * n_split + sp] = _nt_dot(qa_scr[m, sp * hr:(sp + 1) * hr, :], ka)
        nlt = tk // LANE
        blocks = [slice(rb * DA_ROW_BLK, (rb + 1) * DA_ROW_BLK) for rb in range(hr // DA_ROW_BLK)]
        for m in range(2):
            for sp in range(n_split):
                ci = m * n_split + sp
                rs = slice(sp * hr, (sp + 1) * hr)
                pmax = []
                for loc in blocks:
                    s = s_scr[ci, loc, :]
                    if diag:
                        s = s - c2 * u_ref[sp * hr + loc.start:sp * hr + loc.stop, :]
                        s_scr[ci, loc, :] = s
                    pm = s[:, 0:LANE]
                    for lt in range(1, nlt):
                        pm = jnp.maximum(pm, s[:, lt * LANE:(lt + 1) * LANE])
                    pmax.append(pm)
                m_old = m_scr[m, rs, :]
                m_new = jnp.maximum(m_old, jnp.max(jnp.concatenate(pmax, axis=0), axis=1, keepdims=True))
                alpha = jnp.exp2(m_old - m_new)
                m_scr[m, rs, :] = m_new
                psum = []
                for loc in blocks:
                    s = s_scr[ci, loc, :]
                    mb = m_new[loc]
                    acc_l = None
                    for lt in range(nlt):
                        p = jnp.exp2(s[:, lt * LANE:(lt + 1) * LANE] - mb)
                        acc_l = p if acc_l is None else acc_l + p
                        p_scr[ci, loc, lt * LANE:(lt + 1) * LANE] = p.astype(BF16)
                    psum.append(acc_l)
                row_sum = jnp.sum(jnp.concatenate(psum, axis=0), axis=1, keepdims=True)
                l_scr[m, rs, :] = alpha * l_scr[m, rs, :] + row_sum
                pv = jnp.dot(p_scr[ci], v, preferred_element_type=F32)
                acc_scr[m, rs, :] = jnp.concatenate([alpha, alpha], axis=1) * acc_scr[m, rs, :] + pv

    @pl.when(kb == 0)
    def _():
        step(True)

    @pl.when(jnp.logical_and(kb > 0, skip_ref[flat] == 0))
    def _():
        step(False)

    @pl.when(kb == nk - 1)
    def _():
        lv = lam_ref[...]
        lam = (jnp.exp(jnp.sum(lv[0:1] * lv[1:2], axis=1, keepdims=True))
               - jnp.exp(jnp.sum(lv[2:3] * lv[3:4], axis=1, keepdims=True)) + lam_init)
        inv = [1.0 / l_scr[m] for m in range(2)]
        o = (acc_scr[0] * jnp.concatenate([inv[0], inv[0]], axis=1)
             - lam * (acc_scr[1] * jnp.concatenate([inv[1], inv[1]], axis=1)))
        o = _rms(o) * g_ref[...] * (1.0 - lam_init)
        o_ref[...] = o.astype(o_ref.dtype)


def _alibi_features(t, n_heads):
    slopes = 2.0 ** (-8.0 * np.arange(1, n_heads + 1) / n_heads)
    c = jnp.asarray(slopes * LOG2E, F32)
    cp = c[:, None] * jnp.arange(t, dtype=F32)[None, :]
    p3 = jnp.stack(_bf16_parts(cp, 3), axis=-1)
    ones = jnp.ones((n_heads, t, 3), BF16)
    pad = jnp.zeros((n_heads, t, LANE - 6), BF16)
    qf = jnp.concatenate([-p3, ones, pad], axis=-1)
    kf = jnp.concatenate([ones, p3, pad], axis=-1)
    return c, qf, kf


DA_SKIP_MARGIN = 176.0


def _norms_kernel(x_ref, o_ref):
    x = x_ref[...].astype(F32)
    cols = []
    for g in range(x.shape[1] // LANE):
        v = x[:, g * LANE:(g + 1) * LANE]
        cols.append(jnp.max(jnp.sum(v * v, axis=1, keepdims=True), axis=0, keepdims=True))
    o_ref[...] = jnp.concatenate(cols, axis=1)


def _da_tile_norms(qkv, tile):
    bsz, t, _ = qkv.shape
    width = 2 * DA_HEADS * DA_VAL_DIM
    return pl.pallas_call(
        _norms_kernel,
        grid=(bsz, t // tile),
        in_specs=[pl.BlockSpec((None, tile, width), lambda b, i: (b, i, 0))],
        out_specs=pl.BlockSpec((None, None, 1, width // LANE), lambda b, i: (b, i, 0, 0)),
        out_shape=jax.ShapeDtypeStruct((bsz, t // tile, 1, width // LANE), F32),
        compiler_params=_cparams(("parallel", "parallel")),
        name="da_tile_norms",
    )(qkv)


def _da_skip_plan(norms2, c, tile):
    bsz, nt = norms2.shape[0], norms2.shape[1]
    nh = DA_HEADS
    nrm = jnp.sqrt(norms2.reshape(bsz, nt, 2, nh, 2)) * 1.001
    nq, nkk = nrm[:, :, 0], nrm[:, :, 1]
    qi = jnp.arange(nt)[:, None]
    j = jnp.arange(nt)[None, :]
    kt = (qi + j) % nt
    bound = jnp.max(nq[:, :, None] * (nkk[:, kt] + nkk[:, :, None]), axis=-1)
    dist_min = jnp.maximum(jnp.abs(kt - qi) - 1, 0) * tile + 1
    far = c[None, None, None, :] * dist_min[None, :, :, None].astype(F32) >= DA_SKIP_MARGIN + bound
    skip = jnp.logical_and(far, (j > 0)[None, :, :, None]).transpose(0, 3, 1, 2)
    kidx = jnp.broadcast_to(kt[None, None], skip.shape)
    cols = [kidx[..., 0]]
    for jj in range(1, nt):
        cols.append(jnp.where(skip[..., jj], cols[-1], kidx[..., jj]))
    kidx = jnp.stack(cols, axis=-1)
    return skip.astype(jnp.int32).reshape(-1), kidx.astype(jnp.int32).reshape(-1)


def _diff_attention(qkv, lam_vecs, subln_g, lam_init, *, tile):
    bsz, t, _ = qkv.shape
    nh = DA_HEADS
    tq = tk = min(tile, t)
    nq = t // tq
    n_split = 2 if tq >= 4 * DA_ROW_BLK else 1
    c, qf, kf = _alibi_features(t, nh)
    skip, kidx = _da_skip_plan(_da_tile_norms(qkv, tq), c, tq)
    ii = jnp.arange(tq, dtype=F32)
    u = jnp.maximum(ii[None, :] - ii[:, None], 0.0)

    def key_tile(b, h, i, j, kidx_ref):
        return kidx_ref[((b * nh + h) * nq + i) * nq + j]

    grid_spec = pltpu.PrefetchScalarGridSpec(
        num_scalar_prefetch=3,
        grid=(bsz, nh, nq, nq),
        in_specs=[
            pl.BlockSpec((None, tq, DA_VAL_DIM), lambda b, h, i, j, c, sk, ki: (b, i, h)),
            pl.BlockSpec((None, tk, DA_VAL_DIM), lambda b, h, i, j, c, sk, ki: (b, key_tile(b, h, i, j, ki), nh + h)),
            pl.BlockSpec((None, tk, DA_VAL_DIM),
                         lambda b, h, i, j, c, sk, ki: (b, key_tile(b, h, i, j, ki), 2 * nh + h)),
            pl.BlockSpec((None, tq, LANE), lambda b, h, i, j, c, sk, ki: (h, i, 0)),
            pl.BlockSpec((None, tk, LANE), lambda b, h, i, j, c, sk, ki: (h, key_tile(b, h, i, j, ki), 0)),
            pl.BlockSpec((tq, tk), lambda b, h, i, j, c, sk, ki: (0, 0)),
            pl.BlockSpec((4, DA_HEAD_DIM), lambda b, h, i, j, c, sk, ki: (0, 0)),
            pl.BlockSpec((1, DA_VAL_DIM), lambda b, h, i, j, c, sk, ki: (0, 0)),
        ],
        out_specs=pl.BlockSpec((None, tq, DA_VAL_DIM), lambda b, h, i, j, c, sk, ki: (b, i, h)),
        scratch_shapes=[
            pltpu.VMEM((2, tq, 2 * DA_HEAD_DIM), BF16),
            pltpu.VMEM((2, tq, LANE), F32),
            pltpu.VMEM((2, tq, LANE), F32),
            pltpu.VMEM((2, tq, DA_VAL_DIM), F32),
            pltpu.VMEM((2 * n_split, tq // n_split, tk), F32),
            pltpu.VMEM((2 * n_split, tq // n_split, tk), BF16),
        ],
    )
    return pl.pallas_call(
        functools.partial(_da_kernel, lam_init=lam_init),
        grid_spec=grid_spec,
        out_shape=jax.ShapeDtypeStruct((bsz, t, nh * DA_VAL_DIM), BF16),
        compiler_params=_cparams(("parallel", "parallel", "parallel", "arbitrary")),
        name="diff_attention",
    )(c, skip, kidx, qkv, qkv, qkv, qf, kf, u, lam_vecs, subln_g.reshape(1, DA_VAL_DIM))


def _gla_kernel(q_ref, k_ref, v_ref, gl_ref, w2h_ref, w2l_ref, gb_ref, o_ref, st_scr, *, reverse, nchunk):
    cs = GLA_CHUNK
    blk = nchunk * cs

    @pl.when(pl.program_id(2) == 0)
    def _():
        st_scr[...] = jnp.zeros(st_scr.shape, F32)

    ri = lax.broadcasted_iota(jnp.int32, (blk, blk), 0)
    ci = lax.broadcasted_iota(jnp.int32, (blk, blk), 1)
    same_chunk = (ri // cs) == (ci // cs)
    keep = jnp.logical_and(same_chunk, (ci >= ri) if reverse else (ci <= ri))
    tri = jnp.where(keep, 1.0, 0.0).astype(BF16)
    w2h = w2h_ref[...]
    gh, glo = _split2(gl_ref[...])
    x = (jnp.dot(gh, w2h, preferred_element_type=F32) + jnp.dot(glo, w2h, preferred_element_type=F32)
         + jnp.dot(gh, w2l_ref[...], preferred_element_type=F32) + gb_ref[...])
    la = (jnp.minimum(x, 0.0) - jnp.log1p(jnp.exp(-jnp.abs(x)))) * (1.0 / GLA_TAU)
    a1, a2, a3 = _split3(la)
    b = (jnp.dot(tri, a1, preferred_element_type=F32) + jnp.dot(tri, a2, preferred_element_type=F32)
         + jnp.dot(tri, a3, preferred_element_type=F32))
    q = q_ref[...].astype(F32)
    k = k_ref[...].astype(F32)
    v = v_ref[...]
    q_in = (q * jnp.exp(b)).astype(BF16)
    k_in = (k * jnp.exp(-b)).astype(BF16)
    att = jnp.where(keep, _nt_dot(q_in, k_in), 0.0)
    o_intra = jnp.dot(att.astype(BF16), v, preferred_element_type=F32)
    chunks = [slice(c * cs, (c + 1) * cs) for c in range(nchunk)]
    tots = [b[r.start:r.start + 1] if reverse else b[r.stop - 1:r.stop] for r in chunks]
    kvs = [_tn_dot(v[r], (k[r] * jnp.exp(tot - b[r])).astype(BF16)) for r, tot in zip(chunks, tots)]
    st = st_scr[...]
    for c in (range(nchunk - 1, -1, -1) if reverse else range(nchunk)):
        r = chunks[c]
        o_ref[r, :] = o_intra[r] + _nt_dot(q_in[r], st.astype(BF16))
        st = st * jnp.exp(tots[c]) + kvs[c]
    st_scr[...] = st


def _gla_scan(qkvr, g_lr, w2h, w2l, gb, *, reverse, blk):
    bsz, t, _ = qkvr.shape
    nh = GLA_HEADS
    blk = min(blk, t)
    nblk = t // blk
    pos = (lambda i: nblk - 1 - i) if reverse else (lambda i: i)
    kq = (nh * GLA_DK) // GLA_DK
    kv = (2 * nh * GLA_DK) // GLA_DV
    return pl.pallas_call(
        functools.partial(_gla_kernel, reverse=reverse, nchunk=blk // GLA_CHUNK),
        grid=(bsz, nh, nblk),
        in_specs=[
            pl.BlockSpec((None, blk, GLA_DK), lambda b, h, i: (b, pos(i), h)),
            pl.BlockSpec((None, blk, GLA_DK), lambda b, h, i: (b, pos(i), kq + h)),
            pl.BlockSpec((None, blk, GLA_DV), lambda b, h, i: (b, pos(i), kv + h)),
            pl.BlockSpec((None, blk, LANE), lambda b, h, i: (b, pos(i), 0)),
            pl.BlockSpec((LANE, GLA_DK), lambda b, h, i: (0, h)),
            pl.BlockSpec((LANE, GLA_DK), lambda b, h, i: (0, h)),
            pl.BlockSpec((1, GLA_DK), lambda b, h, i: (0, h)),
        ],
        out_specs=pl.BlockSpec((None, blk, GLA_DV), lambda b, h, i: (b, pos(i), h)),
        out_shape=jax.ShapeDtypeStruct((bsz, t, nh * GLA_DV), F32),
        scratch_shapes=[pltpu.VMEM((GLA_DV, GLA_DK), F32)],
        compiler_params=_cparams(("parallel", "parallel", "arbitrary")),
        name="gla_scan_bwd" if reverse else "gla_scan_fwd",
    )(qkvr, qkvr, qkvr, g_lr, w2h, w2l, gb)


def _gla_fin_kernel(of_ref, ob_ref, r_ref, g_ref, o_ref):
    o = _rms(of_ref[...] + ob_ref[...]) * g_ref[...]
    r = r_ref[...].astype(F32)
    o_ref[...] = (o * (r * jax.nn.sigmoid(r))).astype(o_ref.dtype)


def _gla_finish(o_f, o_b, qkvr, norm_g, *, tm):
    bsz, t, _ = o_f.shape
    nh = GLA_HEADS
    tm = min(tm, t)
    kr = (2 * nh * GLA_DK + nh * GLA_DV) // GLA_DV
    return pl.pallas_call(
        _gla_fin_kernel,
        grid=(bsz, t // tm, nh),
        in_specs=[
            pl.BlockSpec((None, tm, GLA_DV), lambda b, i, h: (b, i, h)),
            pl.BlockSpec((None, tm, GLA_DV), lambda b, i, h: (b, i, h)),
            pl.BlockSpec((None, tm, GLA_DV), lambda b, i, h: (b, i, kr + h)),
            pl.BlockSpec((1, GLA_DV), lambda b, i, h: (0, 0)),
        ],
        out_specs=pl.BlockSpec((None, tm, GLA_DV), lambda b, i, h: (b, i, h)),
        out_shape=jax.ShapeDtypeStruct((bsz, t, nh * GLA_DV), BF16),
        compiler_params=_cparams(("parallel", "parallel", "parallel")),
        name="gla_finish",
    )(o_f, o_b, qkvr, norm_g.reshape(1, GLA_DV))


def _na_kernel(q_ref, kp, km, kn, vp, vm, vn, tbl_ref, o_ref, kcat, vcat, *, n_rows):
    i = pl.program_id(2)
    w = GRID_W
    halo = (NA_ROWS // 2) * w
    main = NA_QROWS * w
    for ref_p, ref_m, ref_n, cat in ((kp, km, kn, kcat), (vp, vm, vn, vcat)):
        cat[0:halo, :] = ref_p[...]
        cat[halo:halo + main, :] = ref_m[...]
        cat[halo + main:2 * halo + main, :] = ref_n[...]
    lane_head = lax.broadcasted_iota(jnp.int32, (w, LANE), 1) // NA_HEAD_DIM
    base = i * NA_QROWS
    for rr in range(NA_QROWS):
        r = base + rr
        r0 = jnp.clip(r - NA_ROWS // 2, 0, n_rows - NA_ROWS)
        start = pl.multiple_of((NA_ROWS // 2 + r0 - base) * w, w)
        delta = r - r0
        qr = q_ref[rr * w:(rr + 1) * w, :]
        qs = jnp.concatenate([jnp.where(lane_head == hh, qr, jnp.zeros_like(qr)) for hh in range(NA_GROUP)], axis=0)
        kb = kcat[pl.ds(start, NA_ROWS * w), :]
        vb = vcat[pl.ds(start, NA_ROWS * w), :]
        s = _nt_dot(qs, kb)
        bias = jnp.concatenate(
            [tbl_ref[2 * j - delta + NA_ROWS - 1].reshape(NA_GROUP * w, 2 * w) for j in range(NA_ROWS // 2)], axis=1)
        s = s + bias
        p = jnp.exp2(s - jnp.max(s, axis=1, keepdims=True))
        l = jnp.sum(p, axis=1, keepdims=True)
        o4 = jnp.dot(p.astype(BF16), vb, preferred_element_type=F32) / l
        o = jnp.zeros((w, LANE), F32)
        for hh in range(NA_GROUP):
            o = jnp.where(lane_head == hh, o4[hh * w:(hh + 1) * w, :], o)
        o_ref[rr * w:(rr + 1) * w, :] = o.astype(o_ref.dtype)


def _na_bias_table(rpb):
    col = np.arange(GRID_W)
    col_start = np.clip(col - NA_COLS // 2, 0, GRID_W - NA_COLS)
    in_window = (col[None, :] >= col_start[:, None]) & (col[None, :] < col_start[:, None] + NA_COLS)
    col_off = np.clip(col[None, :] - col[:, None] + NA_COLS - 1, 0, 2 * NA_COLS - 2)
    cb = jnp.where(in_window[None, None], rpb.astype(F32)[:, :, col_off] * LOG2E, NEG_INF)
    pair = jnp.concatenate([cb[:, :-1], cb[:, 1:]], axis=-1)
    n_pair = 2 * NA_ROWS - 2
    pair = pair.reshape(NA_HEADS // NA_GROUP, NA_GROUP, n_pair, GRID_W, 2 * GRID_W)
    return pair.transpose(0, 2, 1, 3, 4)


def _neighborhood_attention(qkv, rpb):
    bsz, t, _ = qkv.shape
    n_rows = t // GRID_W
    ng = NA_HEADS // NA_GROUP
    tq = NA_QROWS * GRID_W
    halo = (NA_ROWS // 2) * GRID_W
    per = tq // halo
    n_halo = t // halo
    tbl = _na_bias_table(rpb)

    def kv_specs(col0):
        return [
            pl.BlockSpec((None, halo, LANE), lambda g, b, i: (b, jnp.maximum(i * per - 1, 0), col0 + g)),
            pl.BlockSpec((None, tq, LANE), lambda g, b, i: (b, i, col0 + g)),
            pl.BlockSpec((None, halo, LANE), lambda g, b, i: (b, jnp.minimum((i + 1) * per, n_halo - 1), col0 + g)),
        ]

    cat_rows = tq + 2 * halo
    return pl.pallas_call(
        functools.partial(_na_kernel, n_rows=n_rows),
        grid=(ng, bsz, t // tq),
        in_specs=[pl.BlockSpec((None, tq, LANE), lambda g, b, i: (b, i, g))] + kv_specs(ng) + kv_specs(2 * ng)
        + [pl.BlockSpec((None, 2 * NA_ROWS - 2, NA_GROUP, GRID_W, 2 * GRID_W), lambda g, b, i: (g, 0, 0, 0, 0))],
        out_specs=pl.BlockSpec((None, tq, LANE), lambda g, b, i: (b, i, g)),
        out_shape=jax.ShapeDtypeStruct((bsz, t, NA_HEADS * NA_HEAD_DIM), BF16),
        scratch_shapes=[pltpu.VMEM((cat_rows, LANE), BF16), pltpu.VMEM((cat_rows, LANE), BF16)],
        compiler_params=_cparams(("parallel", "parallel", "parallel")),
        name="neighborhood_attention",
    )(qkv, *([qkv] * 6), tbl)


def _post_kernel(o_ref, w_ref, x_ref, g1_ref, n2_ref, sc_ref, sh_ref, wrc_ref, xn_ref, hf_ref, lg_ref):
    y = jnp.dot(o_ref[...], w_ref[...], preferred_element_type=F32)
    xn = x_ref[...] + g1_ref[...] * y
    xn_ref[...] = xn
    hf = _rms(xn) * n2_ref[...] * (1.0 + sc_ref[...]) + sh_ref[...]
    tm = hf.shape[0]
    for j in range(hf.shape[1] // LANE):
        hf_ref[pl.ds(j, tm, stride=MOE_SLAB), :] = hf[:, j * LANE:(j + 1) * LANE]
    hh, hl = _split2(hf)
    wrc = wrc_ref[...]
    both = jnp.dot(hh, wrc, preferred_element_type=F32)
    lg_ref[...] = both[:, :LANE] + both[:, LANE:] + jnp.dot(hl, wrc[:, :LANE], preferred_element_type=F32)


def _post_mixer(o, w_out, x, g1, n2g, sc2, sh2, wr_cat, *, tm):
    bsz, t, d = x.shape
    kdim = o.shape[-1]
    tm = min(tm, t)
    vec = pl.BlockSpec((None, 1, d), lambda b, i: (b, 0, 0))
    tile = pl.BlockSpec((None, tm, d), lambda b, i: (b, i, 0))
    return pl.pallas_call(
        _post_kernel,
        grid=(bsz, t // tm),
        in_specs=[
            pl.BlockSpec((None, tm, kdim), lambda b, i: (b, i, 0)),
            pl.BlockSpec((kdim, d), lambda b, i: (0, 0)),
            tile, vec,
            pl.BlockSpec((1, d), lambda b, i: (0, 0)),
            vec, vec,
            pl.BlockSpec((d, 2 * LANE), lambda b, i: (0, 0)),
        ],
        out_specs=[tile, pl.BlockSpec((tm * MOE_SLAB, LANE), lambda b, i: (b * (t // tm) + i, 0)),
                   pl.BlockSpec((None, tm, LANE), lambda b, i: (b, i, 0))],
        out_shape=[jax.ShapeDtypeStruct((bsz, t, d), F32), jax.ShapeDtypeStruct((bsz * t * MOE_SLAB, LANE), F32),
                   jax.ShapeDtypeStruct((bsz, t, LANE), F32)],
        compiler_params=_cparams(("parallel", "parallel")),
        name="post_mixer",
    )(o, w_out, x, g1, n2g.reshape(1, d), sc2, sh2, wr_cat)


ROUTER_ROWS = 8 + MOE_EXPERTS


def _router_kernel(lt_ref, b_ref, id_ref, w_ref):
    lt = lt_ref[...] + b_ref[...]
    tn = lt.shape[1]
    lg = lt[0:MOE_GROUPS]
    e = jnp.exp(lg - jnp.max(lg, axis=0, keepdims=True))
    gp = e / jnp.sum(e, axis=0, keepdims=True)
    g_p = jnp.max(gp, axis=0, keepdims=True)
    rg = lax.broadcasted_iota(jnp.int32, (MOE_GROUPS, tn), 0)
    g_idx = jnp.min(jnp.where(gp == g_p, rg, MOE_GROUPS), axis=0, keepdims=True)
    el = jnp.zeros((MOE_EPG, tn), F32)
    for g in range(MOE_GROUPS):
        el = jnp.where(g_idx == g, lt[8 + g * MOE_EPG:8 + (g + 1) * MOE_EPG], el)
    ee = jnp.exp(el - jnp.max(el, axis=0, keepdims=True))
    ep = ee / jnp.sum(ee, axis=0, keepdims=True)
    re = lax.broadcasted_iota(jnp.int32, (MOE_EPG, tn), 0)
    p1 = jnp.max(ep, axis=0, keepdims=True)
    i1 = jnp.min(jnp.where(ep == p1, re, MOE_EPG), axis=0, keepdims=True)
    ep2 = jnp.where(re == i1, -1.0, ep)
    p2 = jnp.max(ep2, axis=0, keepdims=True)
    i2 = jnp.min(jnp.where(ep2 == p2, re, MOE_EPG), axis=0, keepdims=True)
    den = p1 + p2
    id_ref[0:1, :] = g_idx * MOE_EPG + i1
    id_ref[1:2, :] = g_idx * MOE_EPG + i2
    w_ref[0:1, :] = g_p * (p1 / den)
    w_ref[1:2, :] = g_p * (p2 / den)


def _router(logits_t, bias_col, *, tn):
    n = logits_t.shape[1]
    tn = min(tn, n)
    return pl.pallas_call(
        _router_kernel,
        grid=(n // tn,),
        in_specs=[pl.BlockSpec((ROUTER_ROWS, tn), lambda i: (0, i)),
                  pl.BlockSpec((ROUTER_ROWS, 1), lambda i: (0, 0))],
        out_specs=[pl.BlockSpec((MOE_TOP_K, tn), lambda i: (0, i)), pl.BlockSpec((MOE_TOP_K, tn), lambda i: (0, i))],
        out_shape=[jax.ShapeDtypeStruct((MOE_TOP_K, n), jnp.int32), jax.ShapeDtypeStruct((MOE_TOP_K, n), F32)],
        compiler_params=_cparams(("parallel",)),
        name="router_topk",
    )(logits_t, bias_col)


MOE_DMA_UNROLL = 8


def _moe_kernel(pb_ref, pe_ref, r0_ref, r1_ref, fl_ref, np_ref, src_cur, src_nxt, dst_prv, dst_lst, hf_hbm,
                wgu_ref, wd_ref, y_hbm, wgu_bf, wd_bf, xbuf, obuf, gsem, ssem, *, n_blocks):
    s = pl.program_id(0)
    blk = xbuf.shape[1] // MOE_PITCH
    ff = wd_ref.shape[0]
    b = pb_ref[s]
    slot = b % 2
    flags = fl_ref[s]
    active = s < np_ref[0]
    first = jnp.logical_and(active, (flags & 1) != 0)
    new_w = jnp.logical_and(active, (flags & 4) != 0)

    def gather_copy(src_ref, sl, r):
        return pltpu.make_async_copy(hf_hbm.at[pl.ds(pl.multiple_of(src_ref[0, r] * MOE_SLAB, MOE_SLAB), MOE_SLAB)],
                                     xbuf.at[sl, pl.ds(r * MOE_PITCH, MOE_SLAB)], gsem.at[sl])

    def scatter_copy(dst_ref, sl, r):
        return pltpu.make_async_copy(obuf.at[sl, pl.ds(r * MOE_PITCH, MOE_SLAB)],
                                     y_hbm.at[pl.ds(pl.multiple_of(dst_ref[0, r] * MOE_SLAB, MOE_SLAB), MOE_SLAB)],
                                     ssem.at[sl])

    def issue_loop(make):
        def body(g, carry):
            for j in range(MOE_DMA_UNROLL):
                make(g * MOE_DMA_UNROLL + j).start(priority=j % 2)
            return carry
        lax.fori_loop(0, blk // MOE_DMA_UNROLL, body, 0)

    def wait_gather(sl):
        pltpu.make_async_copy(hf_hbm.at[pl.ds(0, blk * MOE_SLAB)], xbuf.at[sl, pl.ds(0, blk * MOE_SLAB)],
                              gsem.at[sl]).wait()

    def wait_scatter(sl):
        pltpu.make_async_copy(obuf.at[sl, pl.ds(0, blk * MOE_SLAB)], y_hbm.at[pl.ds(0, blk * MOE_SLAB)],
                              ssem.at[sl]).wait()

    @pl.when(s == 0)
    def _():
        obuf[...] = jnp.zeros(obuf.shape, F32)
        issue_loop(lambda r: gather_copy(src_cur, 0, r))

    @pl.when(first)
    def _():
        wait_gather(slot)

        @pl.when(b >= 2)
        def _():
            wait_scatter(slot)

    @pl.when(new_w)
    def _():
        wgu_bf[...] = wgu_ref[...].astype(BF16)
        wd_bf[...] = wd_ref[...].astype(BF16)

    def compute():
        x = jnp.concatenate([xbuf[slot, pl.ds(j, blk, stride=MOE_PITCH), :] for j in range(MOE_SLAB)],
                            axis=1).astype(BF16)
        hgu = jnp.dot(x, wgu_bf[...], preferred_element_type=F32)
        g = hgu[:, :ff]
        u = hgu[:, ff:]
        act = (g * jax.nn.sigmoid(g) * u).astype(BF16)
        y = jnp.dot(act, wd_bf[...], preferred_element_type=F32)
        ri = lax.broadcasted_iota(jnp.int32, (blk, 1), 0)
        mine = jnp.logical_and(ri >= r0_ref[s], ri < r1_ref[s])
        for j in range(MOE_SLAB):
            rows = pl.ds(j, blk, stride=MOE_PITCH)
            obuf[slot, rows, :] = jnp.where(mine, y[:, j * LANE:(j + 1) * LANE], obuf[slot, rows, :])

    @pl.when(first)
    def _():
        do_gather = b + 1 < n_blocks
        do_scatter = b >= 1
        for r in range(blk):
            @pl.when(do_gather)
            def _():
                gather_copy(src_nxt, 1 - slot, r).start(priority=r % 2)

            @pl.when(do_scatter)
            def _():
                scatter_copy(dst_prv, 1 - slot, r).start(priority=(r + 1) % 2)

        compute()

    @pl.when(jnp.logical_and(active, jnp.logical_not(first)))
    def _():
        compute()

    @pl.when(s == pl.num_programs(0) - 1)
    def _():
        last_slot = (n_blocks - 1) % 2
        issue_loop(lambda r: scatter_copy(dst_lst, last_slot, r))
        wait_scatter(0)
        wait_scatter(1)


def _moe_plan(expert_id, blk):
    flat_e = expert_id.reshape(-1)
    n_asg = flat_e.shape[0]
    n_blocks = n_asg // blk
    n_steps = n_blocks + MOE_EXPERTS - 1
    _, asg = lax.sort_key_val(flat_e, jnp.arange(n_asg, dtype=jnp.int32))
    counts = jnp.sum((flat_e[:, None] == jnp.arange(MOE_EXPERTS, dtype=jnp.int32)[None, :]).astype(jnp.int32), axis=0)
    ends = jnp.cumsum(counts)
    starts = ends - counts
    src_tok = asg // MOE_TOP_K
    dst_row = (asg % MOE_TOP_K) * (n_asg // MOE_TOP_K) + asg // MOE_TOP_K
    bidx = jnp.arange(n_blocks, dtype=jnp.int32)
    e_lo = jnp.searchsorted(ends, bidx * blk, side='right').astype(jnp.int32)
    e_hi = jnp.searchsorted(ends, (bidx + 1) * blk - 1, side='right').astype(jnp.int32)
    n_pair_b = e_hi - e_lo + 1
    pair_end = jnp.cumsum(n_pair_b)
    pair_start = pair_end - n_pair_b
    n_pairs = pair_end[-1]
    sidx = jnp.arange(n_steps, dtype=jnp.int32)
    pb = jnp.minimum(jnp.searchsorted(pair_end, sidx, side='right').astype(jnp.int32), n_blocks - 1)
    pe = jnp.clip(e_lo[pb] + sidx - pair_start[pb], 0, MOE_EXPERTS - 1)
    pe = jnp.where(sidx < n_pairs, pe, pe[jnp.maximum(n_pairs - 1, 0)])
    r0 = jnp.clip(starts[pe] - pb * blk, 0, blk)
    r1 = jnp.clip(ends[pe] - pb * blk, 0, blk)
    prev_e = jnp.concatenate([jnp.full((1,), -1, jnp.int32), pe[:-1]])
    flags = ((sidx == pair_start[pb]).astype(jnp.int32) + 2 * (sidx == pair_end[pb] - 1).astype(jnp.int32)
             + 4 * (pe != prev_e).astype(jnp.int32))
    flags = jnp.where(sidx < n_pairs, flags, 0)
    i32 = lambda a: a.astype(jnp.int32)
    return (i32(pb), i32(pe), i32(r0), i32(r1), i32(flags), i32(n_pairs).reshape(1),
            src_tok.reshape(n_blocks, 1, blk), dst_row.reshape(n_blocks, 1, blk))


def _moe_experts(hf, expert_id, w_gu, w_d, layer, *, blk):
    n_tok, d = hf.shape[0] // MOE_SLAB, hf.shape[1] * MOE_SLAB
    n_asg = n_tok * MOE_TOP_K
    blk = min(blk, n_asg // 2)
    n_blocks = n_asg // blk
    pb, pe, r0, r1, flags, n_pairs, src_tok, dst_row = _moe_plan(expert_id, blk)
    ff = w_d.shape[2]
    smem = functools.partial(pl.BlockSpec, memory_space=pltpu.SMEM)
    grid_spec = pltpu.PrefetchScalarGridSpec(
        num_scalar_prefetch=6,
        grid=(pb.shape[0],),
        in_specs=[
            smem((None, 1, blk), lambda s, pb, *_: (pb[s], 0, 0)),
            smem((None, 1, blk), lambda s, pb, *_: (jnp.minimum(pb[s] + 1, n_blocks - 1), 0, 0)),
            smem((None, 1, blk), lambda s, pb, *_: (jnp.maximum(pb[s] - 1, 0), 0, 0)),
            smem((None, 1, blk), lambda s, pb, *_: (n_blocks - 1, 0, 0)),
            pl.BlockSpec(memory_space=pl.ANY),
            pl.BlockSpec((None, None, d, 2 * ff), lambda s, pb, pe, *_: (layer, pe[s], 0, 0)),
            pl.BlockSpec((None, None, ff, d), lambda s, pb, pe, *_: (layer, pe[s], 0, 0)),
        ],
        out_specs=pl.BlockSpec(memory_space=pl.ANY),
        scratch_shapes=[
            pltpu.VMEM((d, 2 * ff), BF16),
            pltpu.VMEM((ff, d), BF16),
            pltpu.VMEM((2, blk * MOE_PITCH, LANE), F32),
            pltpu.VMEM((2, blk * MOE_PITCH, LANE), F32),
            pltpu.SemaphoreType.DMA((2,)),
            pltpu.SemaphoreType.DMA((2,)),
        ],
    )
    return pl.pallas_call(
        functools.partial(_moe_kernel, n_blocks=n_blocks),
        grid_spec=grid_spec,
        out_shape=jax.ShapeDtypeStruct((n_asg * MOE_SLAB, LANE), F32),
        compiler_params=_cparams(("arbitrary",)),
        name="moe_experts",
    )(pb, pe, r0, r1, flags, n_pairs, src_tok, src_tok, dst_row, dst_row, hf, w_gu, w_d)


def _combine_kernel(x_ref, y0_ref, y1_ref, w_ref, g2_ref, fg_ref, o_ref, *, final):
    w = w_ref[...]
    tm = w.shape[0]

    def rows_to_lanes(y_ref):
        return jnp.concatenate([y_ref[pl.ds(j, tm, stride=MOE_SLAB), :] for j in range(MOE_SLAB)], axis=1)

    moe = w[:, 0:1] * rows_to_lanes(y0_ref) + w[:, 1:2] * rows_to_lanes(y1_ref)
    xn = x_ref[...] + g2_ref[...] * moe
    if final:
        xn = _rms(xn) * fg_ref[...]
    o_ref[...] = xn


def _combine(x, ybuf, wts, g2, final_g, *, final, tm):
    bsz, t, d = x.shape
    tm = min(tm, t)
    nt = t // tm
    nb = bsz * nt
    return pl.pallas_call(
        functools.partial(_combine_kernel, final=final),
        grid=(bsz, nt),
        in_specs=[
            pl.BlockSpec((None, tm, d), lambda b, i: (b, i, 0)),
            pl.BlockSpec((tm * MOE_SLAB, LANE), lambda b, i: (b * nt + i, 0)),
            pl.BlockSpec((tm * MOE_SLAB, LANE), lambda b, i: (nb + b * nt + i, 0)),
            pl.BlockSpec((tm, MOE_TOP_K), lambda b, i: (b * nt + i, 0)),
            pl.BlockSpec((None, 1, d), lambda b, i: (b, 0, 0)),
            pl.BlockSpec((1, d), lambda b, i: (0, 0)),
        ],
        out_specs=pl.BlockSpec((None, tm, d), lambda b, i: (b, i, 0)),
        out_shape=jax.ShapeDtypeStruct((bsz, t, d), F32),
        compiler_params=_cparams(("parallel", "parallel")),
        name="moe_combine_final" if final else "moe_combine",
    )(x, ybuf, ybuf, wts, g2, final_g.reshape(1, d))


def _router_weights(rg_w, rg_b, re_w, re_b):
    d = rg_w.shape[0]
    wr = jnp.zeros((d, LANE), F32).at[:, 0:MOE_GROUPS].set(rg_w).at[:, 8:8 + MOE_EXPERTS].set(re_w)
    bias = jnp.zeros((ROUTER_ROWS, 1), F32).at[0:MOE_GROUPS, 0].set(rg_b).at[8:, 0].set(re_b)
    return jnp.concatenate(_bf16_parts(wr, 2), axis=1), bias


def _col_scale(n, n_scaled, scale):
    return jnp.concatenate([jnp.full((n_scaled,), scale, F32), jnp.ones((n - n_scaled,), F32)])


def kernel(x, c, ada_w, ada_b, norm1_g, norm2_g, router_g_w, router_g_b, router_e_w, router_e_b, moe_w_gu, moe_w_d, da_w_in, da_w_out, da_lam_q1, da_lam_k1, da_lam_q2, da_lam_k2, da_subln_g, gla_w_in, gla_gate_w2_f, gla_gate_b_f, gla_gate_w2_b, gla_gate_b_b, gla_norm_g, gla_w_out, na_w_in, na_rpb, na_w_out, final_g):
    bsz, t, d = x.shape
    n_tok = bsz * t
    mod = _ada_mod(c, ada_w, ada_b)
    wts = ybuf = g2 = None
    for i in range(DEPTH):
        sh1, sc1, g1, sh2, sc2, g2_i = [mod[i, :, None, m * d:(m + 1) * d] for m in range(6)]
        if i > 0:
            x = _combine(x, ybuf, wts, g2, final_g, final=False, tm=512)
        kind, j = i % N_MIXERS, i // N_MIXERS
        if kind == 0:
            w_in = da_w_in[j].astype(BF16)
            cs = _col_scale(w_in.shape[1], DA_HEADS * DA_VAL_DIM, DA_HEAD_DIM ** -0.5 * LOG2E)
            qkv = _norm_proj(x, norm1_g[i], sc1, sh1, w_in, cs, tm=1024, tn=1024, out_dtype=BF16)
            lam_vecs = jnp.stack([da_lam_q1[j], da_lam_k1[j], da_lam_q2[j], da_lam_k2[j]]).astype(F32)
            lam_init = 0.8 - 0.6 * math.exp(-0.3 * i)
            o = _diff_attention(qkv, lam_vecs, da_subln_g[j], lam_init, tile=1024)
            w_out = da_w_out[j]
        elif kind == 1:
            n_main = 2 * GLA_HEADS * GLA_DK + 2 * GLA_HEADS * GLA_DV
            w_main = gla_w_in[j][:, :n_main].astype(BF16)
            w_gate = jnp.zeros((d, LANE), F32).at[:, :2 * GLA_GATE_RANK].set(gla_w_in[j][:, n_main:]).astype(BF16)
            cs = _col_scale(n_main, GLA_HEADS * GLA_DK, GLA_DK ** -0.5)
            qkvr = _norm_proj(x, norm1_g[i], sc1, sh1, w_main, cs, tm=1024, tn=1024, out_dtype=BF16)
            g_lr = _norm_proj(x, norm1_g[i], sc1, sh1, w_gate, jnp.ones((LANE,), F32), tm=1024, tn=LANE,
                              out_dtype=F32)
            outs = []
            for rev, w2, gb in ((False, gla_gate_w2_f[j], gla_gate_b_f[j]), (True, gla_gate_w2_b[j], gla_gate_b_b[j])):
                r0 = GLA_GATE_RANK if rev else 0
                w2p = jnp.zeros((LANE, w2.shape[1]), F32).at[r0:r0 + GLA_GATE_RANK].set(w2)
                w2h, w2l = _bf16_parts(w2p, 2)
                outs.append(_gla_scan(qkvr, g_lr, w2h, w2l, gb.reshape(1, -1).astype(F32), reverse=rev, blk=512))
            o = _gla_finish(outs[0], outs[1], qkvr, gla_norm_g[j], tm=512)
            w_out = gla_w_out[j]
        else:
            w_in = na_w_in[j].astype(BF16)
            cs = _col_scale(w_in.shape[1], NA_HEADS * NA_HEAD_DIM, NA_HEAD_DIM ** -0.5 * LOG2E)
            qkv = _norm_proj(x, norm1_g[i], sc1, sh1, w_in, cs, tm=1024, tn=1024, out_dtype=BF16)
            o = _neighborhood_attention(qkv, na_rpb[j])
            w_out = na_w_out[j]
        wr_cat, r_bias = _router_weights(router_g_w[i], router_g_b[i], router_e_w[i], router_e_b[i])
        x, hf, logits = _post_mixer(o, w_out.astype(BF16), x, g1, norm2_g[i], sc2, sh2, wr_cat, tm=256)
        logits_t = logits.reshape(n_tok, LANE).T[:ROUTER_ROWS]
        ids, wt = _router(logits_t, r_bias, tn=2048)
        ybuf = _moe_experts(hf, ids.T, moe_w_gu, moe_w_d, i, blk=MOE_BLK)
        wts = wt.T
        g2 = g2_i
    return _combine(x, ybuf, wts, g2, final_g, final=True, tm=512)
```

```python
import functools
import math

import numpy as np
import jax
import jax.numpy as jnp
from jax import lax
from jax.experimental import pallas as pl
from jax.experimental.pallas import tpu as pltpu

F32 = jnp.float32
BF16 = jnp.bfloat16

D_MODEL = 2048
DEPTH = 4
N_MIXERS = 3
NORM_EPS = 1e-6
NEG_INF = -1e30
LOG2E = 1.4426950408889634

DA_HEADS = 8
DA_HEAD_DIM = 128
DA_VAL_DIM = 2 * DA_HEAD_DIM
DA_ROW_BLK = 16

GLA_HEADS = 4
GLA_DK = 256
GLA_DV = 512
GLA_GATE_RANK = 16
GLA_TAU = 16.0
GLA_CHUNK = 64

GRID_W = 64
NA_HEADS = 64
NA_HEAD_DIM = 32
NA_ROWS = 8
NA_COLS = 16
NA_GROUP = 4
NA_QROWS = 64

MOE_GROUPS = 4
MOE_EPG = 8
MOE_EXPERTS = 32
MOE_TOP_K = 2
MOE_D_FF = 512
MOE_BLK = 256
MOE_SLAB = D_MODEL // 128
MOE_PITCH = 20

LANE = 128
VMEM_LIMIT = 56 * 1024 * 1024


def _cparams(sem):
    return pltpu.CompilerParams(dimension_semantics=sem, vmem_limit_bytes=VMEM_LIMIT)


def _split2(v):
    hi = v.astype(BF16)
    lo = (v - hi.astype(F32)).astype(BF16)
    return hi, lo


def _split3(v):
    hi = v.astype(BF16)
    r = v - hi.astype(F32)
    mid = r.astype(BF16)
    lo = (r - mid.astype(F32)).astype(BF16)
    return hi, mid, lo


def _bf16_parts(v, n):
    parts = []
    r = v.astype(F32)
    for _ in range(n):
        top = lax.bitcast_convert_type(
            lax.bitcast_convert_type(r, jnp.uint32) & jnp.uint32(0xFFFF0000), F32)
        parts.append(top.astype(BF16))
        r = r - top
    return parts


def _nt_dot(a, b):
    return lax.dot_general(a, b, (((1,), (1,)), ((), ())), preferred_element_type=F32)


def _tn_dot(a, b):
    return lax.dot_general(a, b, (((0,), (0,)), ((), ())), preferred_element_type=F32)


def _rms(x):
    return x * lax.rsqrt(jnp.mean(x * x, axis=-1, keepdims=True) + NORM_EPS)


def _ada_kernel(c_ref, w_ref, b_ref, o_ref):
    c = c_ref[...]
    cond = (c * jax.nn.sigmoid(c)).astype(BF16)
    o_ref[...] = jnp.dot(cond, w_ref[...].astype(BF16), preferred_element_type=F32) + b_ref[...]


def _ada_mod(c, ada_w, ada_b):
    depth, d, n = ada_w.shape
    bsz = c.shape[0]
    rows = 8
    c8 = jnp.zeros((rows, d), F32).at[:bsz].set(c)
    tn = 1024
    out = pl.pallas_call(
        _ada_kernel,
        grid=(depth, n // tn),
        in_specs=[
            pl.BlockSpec((rows, d), lambda l, j: (0, 0)),
            pl.BlockSpec((None, d, tn), lambda l, j: (l, 0, j)),
            pl.BlockSpec((None, 1, tn), lambda l, j: (l, 0, j)),
        ],
        out_specs=pl.BlockSpec((None, rows, tn), lambda l, j: (l, 0, j)),
        out_shape=jax.ShapeDtypeStruct((depth, rows, n), F32),
        compiler_params=_cparams(("parallel", "parallel")),
        name="ada_mod",
    )(c8, ada_w, ada_b.reshape(depth, 1, n))
    return out[:, :bsz]


def _proj_kernel(x_ref, g_ref, sc_ref, sh_ref, w_ref, cs_ref, o_ref, h_scr):
    @pl.when(pl.program_id(2) == 0)
    def _():
        h = _rms(x_ref[...]) * g_ref[...] * (1.0 + sc_ref[...]) + sh_ref[...]
        h_scr[...] = h.astype(BF16)

    acc = jnp.dot(h_scr[...], w_ref[...], preferred_element_type=F32)
    o_ref[...] = (acc * cs_ref[...]).astype(o_ref.dtype)


def _norm_proj(x, g, sc, sh, w, col_scale, *, tm, tn, out_dtype):
    bsz, t, d = x.shape
    n = w.shape[1]
    tm = min(tm, t)
    return pl.pallas_call(
        _proj_kernel,
        grid=(bsz, t // tm, n // tn),
        in_specs=[
            pl.BlockSpec((None, tm, d), lambda b, i, j: (b, i, 0)),
            pl.BlockSpec((1, d), lambda b, i, j: (0, 0)),
            pl.BlockSpec((None, 1, d), lambda b, i, j: (b, 0, 0)),
            pl.BlockSpec((None, 1, d), lambda b, i, j: (b, 0, 0)),
            pl.BlockSpec((d, tn), lambda b, i, j: (0, j)),
            pl.BlockSpec((1, tn), lambda b, i, j: (0, j)),
        ],
        out_specs=pl.BlockSpec((None, tm, tn), lambda b, i, j: (b, i, j)),
        out_shape=jax.ShapeDtypeStruct((bsz, t, n), out_dtype),
        scratch_shapes=[pltpu.VMEM((tm, d), BF16)],
        compiler_params=_cparams(("parallel", "parallel", "arbitrary")),
        name="norm_proj",
    )(x, g.reshape(1, d), sc, sh, w, col_scale.reshape(1, n))


def _da_kernel(c_ref, skip_ref, kidx_ref, q_ref, k_ref, v_ref, qf_ref, kf_ref, u_ref, lam_ref, g_ref, o_ref,
               qa_scr, m_scr, l_scr, acc_scr, s_scr, p_scr, *, lam_init):
    h = pl.program_id(1)
    qi = pl.program_id(2)
    kb = pl.program_id(3)
    nk = pl.num_programs(3)
    hd = DA_HEAD_DIM
    n_split, hr, tk = s_scr.shape[0] // 2, s_scr.shape[1], s_scr.shape[2]
    kt = lax.rem(qi + kb, nk)
    flat = ((pl.program_id(0) * pl.num_programs(1) + h) * pl.num_programs(2) + qi) * nk + kb

    @pl.when(kb == 0)
    def _():
        m_scr[...] = jnp.full(m_scr.shape, -jnp.inf, F32)
        l_scr[...] = jnp.zeros(l_scr.shape, F32)
        acc_scr[...] = jnp.zeros(acc_scr.shape, F32)
        q = q_ref[...]
        qf = qf_ref[...]
        for m in range(2):
            qm = q[:, m * hd:(m + 1) * hd]
            qa_scr[m] = jnp.concatenate([qm, qf], axis=1)

    def step(diag):
        sgn = jnp.where(kt > qi, -1.0, 1.0).astype(BF16)
        k = k_ref[...]
        kf = kf_ref[...] * sgn
        v = v_ref[...]
        c2 = 2.0 * c_ref[h]
        for m in range(2):
            ka = jnp.concatenate([k[:, m * hd:(m + 1) * hd], kf], axis=1)
            for sp in range(n_split):
                s_scr[m * n_split + sp] = _nt_dot(qa_scr[m, sp * hr:(sp + 1) * hr, :], ka)
        nlt = tk // LANE
        blocks = [slice(rb * DA_ROW_BLK, (rb + 1) * DA_ROW_BLK) for rb in range(hr // DA_ROW_BLK)]
        for m in range(2):
            for sp in range(n_split):
                ci = m * n_split + sp
                rs = slice(sp * hr, (sp + 1) * hr)
                pmax = []
                for loc in blocks:
                    s = s_scr[ci, loc, :]
                    if diag:
                        s = s - c2 * u_ref[sp * hr + loc.start:sp * hr + loc.stop, :]
                        s_scr[ci, loc, :] = s
                    pm = s[:, 0:LANE]
                    for lt in range(1, nlt):
                        pm = jnp.maximum(pm, s[:, lt * LANE:(lt + 1) * LANE])
                    pmax.append(pm)
                m_old = m_scr[m, rs, :]
                m_new = jnp.maximum(m_old, jnp.max(jnp.concatenate(pmax, axis=0), axis=1, keepdims=True))
                alpha = jnp.exp2(m_old - m_new)
                m_scr[m, rs, :] = m_new
                psum = []
                for loc in blocks:
                    s = s_scr[ci, loc, :]
                    mb = m_new[loc]
                    acc_l = None
                    for lt in range(nlt):
                        p = jnp.exp2(s[:, lt * LANE:(lt + 1) * LANE] - mb)
                        acc_l = p if acc_l is None else acc_l + p
                        p_scr[ci, loc, lt * LANE:(lt + 1) * LANE] = p.astype(BF16)
                    psum.append(acc_l)
                row_sum = jnp.sum(jnp.concatenate(psum, axis=0), axis=1, keepdims=True)
                l_scr[m, rs, :] = alpha * l_scr[m, rs, :] + row_sum
                pv = jnp.dot(p_scr[ci], v, preferred_element_type=F32)
                acc_scr[m, rs, :] = jnp.concatenate([alpha, alpha], axis=1) * acc_scr[m, rs, :] + pv

    @pl.when(kb == 0)
    def _():
        step(True)

    @pl.when(jnp.logical_and(kb > 0, skip_ref[flat] == 0))
    def _():
        step(False)

    @pl.when(kb == nk - 1)
    def _():
        lv = lam_ref[...]
        lam = (jnp.exp(jnp.sum(lv[0:1] * lv[1:2], axis=1, keepdims=True))
               - jnp.exp(jnp.sum(lv[2:3] * lv[3:4], axis=1, keepdims=True)) + lam_init)
        inv = [1.0 / l_scr[m] for m in range(2)]
        o = (acc_scr[0] * jnp.concatenate([inv[0], inv[0]], axis=1)
             - lam * (acc_scr[1] * jnp.concatenate([inv[1], inv[1]], axis=1)))
        o = _rms(o) * g_ref[...] * (1.0 - lam_init)
        o_ref[...] = o.astype(o_ref.dtype)


def _alibi_features(t, n_heads):
    slopes = 2.0 ** (-8.0 * np.arange(1, n_heads + 1) / n_heads)
    c = jnp.asarray(slopes * LOG2E, F32)
    cp = c[:, None] * jnp.arange(t, dtype=F32)[None, :]
    p3 = jnp.stack(_bf16_parts(cp, 3), axis=-1)
    ones = jnp.ones((n_heads, t, 3), BF16)
    pad = jnp.zeros((n_heads, t, LANE - 6), BF16)
    qf = jnp.concatenate([-p3, ones, pad], axis=-1)
    kf = jnp.concatenate([ones, p3, pad], axis=-1)
    return c, qf, kf


DA_SKIP_MARGIN = 176.0


def _norms_kernel(x_ref, o_ref):
    x = x_ref[...].astype(F32)
    cols = []
    for g in range(x.shape[1] // LANE):
        v = x[:, g * LANE:(g + 1) * LANE]
        cols.append(jnp.max(jnp.sum(v * v, axis=1, keepdims=True), axis=0, keepdims=True))
    o_ref[...] = jnp.concatenate(cols, axis=1)


def _da_tile_norms(qkv, tile):
    bsz, t, _ = qkv.shape
    width = 2 * DA_HEADS * DA_VAL_DIM
    return pl.pallas_call(
        _norms_kernel,
        grid=(bsz, t // tile),
        in_specs=[pl.BlockSpec((None, tile, width), lambda b, i: (b, i, 0))],
        out_specs=pl.BlockSpec((None, None, 1, width // LANE), lambda b, i: (b, i, 0, 0)),
        out_shape=jax.ShapeDtypeStruct((bsz, t // tile, 1, width // LANE), F32),
        compiler_params=_cparams(("parallel", "parallel")),
        name="da_tile_norms",
    )(qkv)


def _da_skip_plan(norms2, c, tile):
    bsz, nt = norms2.shape[0], norms2.shape[1]
    nh = DA_HEADS
    nrm = jnp.sqrt(norms2.reshape(bsz, nt, 2, nh, 2)) * 1.001
    nq, nkk = nrm[:, :, 0], nrm[:, :, 1]
    qi = jnp.arange(nt)[:, None]
    j = jnp.arange(nt)[None, :]
    kt = (qi + j) % nt
    bound = jnp.max(nq[:, :, None] * (nkk[:, kt] + nkk[:, :, None]), axis=-1)
    dist_min = jnp.maximum(jnp.abs(kt - qi) - 1, 0) * tile + 1
    far = c[None, None, None, :] * dist_min[None, :, :, None].astype(F32) >= DA_SKIP_MARGIN + bound
    skip = jnp.logical_and(far, (j > 0)[None, :, :, None]).transpose(0, 3, 1, 2)
    kidx = jnp.broadcast_to(kt[None, None], skip.shape)
    cols = [kidx[..., 0]]
    for jj in range(1, nt):
        cols.append(jnp.where(skip[..., jj], cols[-1], kidx[..., jj]))
    kidx = jnp.stack(cols, axis=-1)
    return skip.astype(jnp.int32).reshape(-1), kidx.astype(jnp.int32).reshape(-1)


def _diff_attention(qkv, lam_vecs, subln_g, lam_init, *, tile):
    bsz, t, _ = qkv.shape
    nh = DA_HEADS
    tq = tk = min(tile, t)
    nq = t // tq
    n_split = 2 if tq >= 4 * DA_ROW_BLK else 1
    c, qf, kf = _alibi_features(t, nh)
    skip, kidx = _da_skip_plan(_da_tile_norms(qkv, tq), c, tq)
    ii = jnp.arange(tq, dtype=F32)
    u = jnp.maximum(ii[None, :] - ii[:, None], 0.0)

    def key_tile(b, h, i, j, kidx_ref):
        return kidx_ref[((b * nh + h) * nq + i) * nq + j]

    grid_spec = pltpu.PrefetchScalarGridSpec(
        num_scalar_prefetch=3,
        grid=(bsz, nh, nq, nq),
        in_specs=[
            pl.BlockSpec((None, tq, DA_VAL_DIM), lambda b, h, i, j, c, sk, ki: (b, i, h)),
            pl.BlockSpec((None, tk, DA_VAL_DIM), lambda b, h, i, j, c, sk, ki: (b, key_tile(b, h, i, j, ki), nh + h)),
            pl.BlockSpec((None, tk, DA_VAL_DIM),
                         lambda b, h, i, j, c, sk, ki: (b, key_tile(b, h, i, j, ki), 2 * nh + h)),
            pl.BlockSpec((None, tq, LANE), lambda b, h, i, j, c, sk, ki: (h, i, 0)),
            pl.BlockSpec((None, tk, LANE), lambda b, h, i, j, c, sk, ki: (h, key_tile(b, h, i, j, ki), 0)),
            pl.BlockSpec((tq, tk), lambda b, h, i, j, c, sk, ki: (0, 0)),
            pl.BlockSpec((4, DA_HEAD_DIM), lambda b, h, i, j, c, sk, ki: (0, 0)),
            pl.BlockSpec((1, DA_VAL_DIM), lambda b, h, i, j, c, sk, ki: (0, 0)),
        ],
        out_specs=pl.BlockSpec((None, tq, DA_VAL_DIM), lambda b, h, i, j, c, sk, ki: (b, i, h)),
        scratch_shapes=[
            pltpu.VMEM((2, tq, 2 * DA_HEAD_DIM), BF16),
            pltpu.VMEM((2, tq, LANE), F32),
            pltpu.VMEM((2, tq, LANE), F32),
            pltpu.VMEM((2, tq, DA_VAL_DIM), F32),
            pltpu.VMEM((2 * n_split, tq // n_split, tk), F32),
            pltpu.VMEM((2 * n_split, tq // n_split, tk), BF16),
        ],
    )
    return pl.pallas_call(
        functools.partial(_da_kernel, lam_init=lam_init),
        grid_spec=grid_spec,
        out_shape=jax.ShapeDtypeStruct((bsz, t, nh * DA_VAL_DIM), BF16),
        compiler_params=_cparams(("parallel", "parallel", "parallel", "arbitrary")),
        name="diff_attention",
    )(c, skip, kidx, qkv, qkv, qkv, qf, kf, u, lam_vecs, subln_g.reshape(1, DA_VAL_DIM))


def _gla_kernel(q_ref, k_ref, v_ref, gl_ref, w2h_ref, w2l_ref, gb_ref, o_ref, st_scr, *, reverse, nchunk):
    cs = GLA_CHUNK
    blk = nchunk * cs

    @pl.when(pl.program_id(2) == 0)
    def _():
        st_scr[...] = jnp.zeros(st_scr.shape, F32)

    ri = lax.broadcasted_iota(jnp.int32, (blk, blk), 0)
    ci = lax.broadcasted_iota(jnp.int32, (blk, blk), 1)
    same_chunk = (ri // cs) == (ci // cs)
    keep = jnp.logical_and(same_chunk, (ci >= ri) if reverse else (ci <= ri))
    tri = jnp.where(keep, 1.0, 0.0).astype(BF16)
    w2h = w2h_ref[...]
    gh, glo = _split2(gl_ref[...])
    x = (jnp.dot(gh, w2h, preferred_element_type=F32) + jnp.dot(glo, w2h, preferred_element_type=F32)
         + jnp.dot(gh, w2l_ref[...], preferred_element_type=F32) + gb_ref[...])
    la = (jnp.minimum(x, 0.0) - jnp.log1p(jnp.exp(-jnp.abs(x)))) * (1.0 / GLA_TAU)
    a1, a2, a3 = _split3(la)
    b = (jnp.dot(tri, a1, preferred_element_type=F32) + jnp.dot(tri, a2, preferred_element_type=F32)
         + jnp.dot(tri, a3, preferred_element_type=F32))
    q = q_ref[...].astype(F32)
    k = k_ref[...].astype(F32)
    v = v_ref[...]
    q_in = (q * jnp.exp(b)).astype(BF16)
    k_in = (k * jnp.exp(-b)).astype(BF16)
    att = jnp.where(keep, _nt_dot(q_in, k_in), 0.0)
    o_intra = jnp.dot(att.astype(BF16), v, preferred_element_type=F32)
    chunks = [slice(c * cs, (c + 1) * cs) for c in range(nchunk)]
    tots = [b[r.start:r.start + 1] if reverse else b[r.stop - 1:r.stop] for r in chunks]
    kvs = [_tn_dot(v[r], (k[r] * jnp.exp(tot - b[r])).astype(BF16)) for r, tot in zip(chunks, tots)]
    st = st_scr[...]
    for c in (range(nchunk - 1, -1, -1) if reverse else range(nchunk)):
        r = chunks[c]
        o_ref[r, :] = o_intra[r] + _nt_dot(q_in[r], st.astype(BF16))
        st = st * jnp.exp(tots[c]) + kvs[c]
    st_scr[...] = st


def _gla_scan(qkvr, g_lr, w2h, w2l, gb, *, reverse, blk):
    bsz, t, _ = qkvr.shape
    nh = GLA_HEADS
    blk = min(blk, t)
    nblk = t // blk
    pos = (lambda i: nblk - 1 - i) if reverse else (lambda i: i)
    kq = (nh * GLA_DK) // GLA_DK
    kv = (2 * nh * GLA_DK) // GLA_DV
    return pl.pallas_call(
        functools.partial(_gla_kernel, reverse=reverse, nchunk=blk // GLA_CHUNK),
        grid=(bsz, nh, nblk),
        in_specs=[
            pl.BlockSpec((None, blk, GLA_DK), lambda b, h, i: (b, pos(i), h)),
            pl.BlockSpec((None, blk, GLA_DK), lambda b, h, i: (b, pos(i), kq + h)),
            pl.BlockSpec((None, blk, GLA_DV), lambda b, h, i: (b, pos(i), kv + h)),
            pl.BlockSpec((None, blk, LANE), lambda b, h, i: (b, pos(i), 0)),
            pl.BlockSpec((LANE, GLA_DK), lambda b, h, i: (0, h)),
            pl.BlockSpec((LANE, GLA_DK), lambda b, h, i: (0, h)),
            pl.BlockSpec((1, GLA_DK), lambda b, h, i: (0, h)),
        ],
        out_specs=pl.BlockSpec((None, blk, GLA_DV), lambda b, h, i: (b, pos(i), h)),
        out_shape=jax.ShapeDtypeStruct((bsz, t, nh * GLA_DV), F32),
        scratch_shapes=[pltpu.VMEM((GLA_DV, GLA_DK), F32)],
        compiler_params=_cparams(("parallel", "parallel", "arbitrary")),
        name="gla_scan_bwd" if reverse else "gla_scan_fwd",
    )(qkvr, qkvr, qkvr, g_lr, w2h, w2l, gb)


def _gla_fin_kernel(of_ref, ob_ref, r_ref, g_ref, o_ref):
    o = _rms(of_ref[...] + ob_ref[...]) * g_ref[...]
    r = r_ref[...].astype(F32)
    o_ref[...] = (o * (r * jax.nn.sigmoid(r))).astype(o_ref.dtype)


def _gla_finish(o_f, o_b, qkvr, norm_g, *, tm):
    bsz, t, _ = o_f.shape
    nh = GLA_HEADS
    tm = min(tm, t)
    kr = (2 * nh * GLA_DK + nh * GLA_DV) // GLA_DV
    return pl.pallas_call(
        _gla_fin_kernel,
        grid=(bsz, t // tm, nh),
        in_specs=[
            pl.BlockSpec((None, tm, GLA_DV), lambda b, i, h: (b, i, h)),
            pl.BlockSpec((None, tm, GLA_DV), lambda b, i, h: (b, i, h)),
            pl.BlockSpec((None, tm, GLA_DV), lambda b, i, h: (b, i, kr + h)),
            pl.BlockSpec((1, GLA_DV), lambda b, i, h: (0, 0)),
        ],
        out_specs=pl.BlockSpec((None, tm, GLA_DV), lambda b, i, h: (b, i, h)),
        out_shape=jax.ShapeDtypeStruct((bsz, t, nh * GLA_DV), BF16),
        compiler_params=_cparams(("parallel", "parallel", "parallel")),
        name="gla_finish",
    )(o_f, o_b, qkvr, norm_g.reshape(1, GLA_DV))


def _na_kernel(q_ref, kp, km, kn, vp, vm, vn, tbl_ref, o_ref, kcat, vcat, *, n_rows):
    i = pl.program_id(2)
    w = GRID_W
    halo = (NA_ROWS // 2) * w
    main = NA_QROWS * w
    for ref_p, ref_m, ref_n, cat in ((kp, km, kn, kcat), (vp, vm, vn, vcat)):
        cat[0:halo, :] = ref_p[...]
        cat[halo:halo + main, :] = ref_m[...]
        cat[halo + main:2 * halo + main, :] = ref_n[...]
    lane_head = lax.broadcasted_iota(jnp.int32, (w, LANE), 1) // NA_HEAD_DIM
    base = i * NA_QROWS
    for rr in range(NA_QROWS):
        r = base + rr
        r0 = jnp.clip(r - NA_ROWS // 2, 0, n_rows - NA_ROWS)
        start = pl.multiple_of((NA_ROWS // 2 + r0 - base) * w, w)
        delta = r - r0
        qr = q_ref[rr * w:(rr + 1) * w, :]
        qs = jnp.concatenate([jnp.where(lane_head == hh, qr, jnp.zeros_like(qr)) for hh in range(NA_GROUP)], axis=0)
        kb = kcat[pl.ds(start, NA_ROWS * w), :]
        vb = vcat[pl.ds(start, NA_ROWS * w), :]
        s = _nt_dot(qs, kb)
        bias = jnp.concatenate(
            [tbl_ref[2 * j - delta + NA_ROWS - 1].reshape(NA_GROUP * w, 2 * w) for j in range(NA_ROWS // 2)], axis=1)
        s = s + bias
        p = jnp.exp2(s - jnp.max(s, axis=1, keepdims=True))
        l = jnp.sum(p, axis=1, keepdims=True)
        o4 = jnp.dot(p.astype(BF16), vb, preferred_element_type=F32) / l
        o = jnp.zeros((w, LANE), F32)
        for hh in range(NA_GROUP):
            o = jnp.where(lane_head == hh, o4[hh * w:(hh + 1) * w, :], o)
        o_ref[rr * w:(rr + 1) * w, :] = o.astype(o_ref.dtype)


def _na_bias_table(rpb):
    col = np.arange(GRID_W)
    col_start = np.clip(col - NA_COLS // 2, 0, GRID_W - NA_COLS)
    in_window = (col[None, :] >= col_start[:, None]) & (col[None, :] < col_start[:, None] + NA_COLS)
    col_off = np.clip(col[None, :] - col[:, None] + NA_COLS - 1, 0, 2 * NA_COLS - 2)
    cb = jnp.where(in_window[None, None], rpb.astype(F32)[:, :, col_off] * LOG2E, NEG_INF)
    pair = jnp.concatenate([cb[:, :-1], cb[:, 1:]], axis=-1)
    n_pair = 2 * NA_ROWS - 2
    pair = pair.reshape(NA_HEADS // NA_GROUP, NA_GROUP, n_pair, GRID_W, 2 * GRID_W)
    return pair.transpose(0, 2, 1, 3, 4)


def _neighborhood_attention(qkv, rpb):
    bsz, t, _ = qkv.shape
    n_rows = t // GRID_W
    ng = NA_HEADS // NA_GROUP
    tq = NA_QROWS * GRID_W
    halo = (NA_ROWS // 2) * GRID_W
    per = tq // halo
    n_halo = t // halo
    tbl = _na_bias_table(rpb)

    def kv_specs(col0):
        return [
            pl.BlockSpec((None, halo, LANE), lambda g, b, i: (b, jnp.maximum(i * per - 1, 0), col0 + g)),
            pl.BlockSpec((None, tq, LANE), lambda g, b, i: (b, i, col0 + g)),
            pl.BlockSpec((None, halo, LANE), lambda g, b, i: (b, jnp.minimum((i + 1) * per, n_halo - 1), col0 + g)),
        ]

    cat_rows = tq + 2 * halo
    return pl.pallas_call(
        functools.partial(_na_kernel, n_rows=n_rows),
        grid=(ng, bsz, t // tq),
        in_specs=[pl.BlockSpec((None, tq, LANE), lambda g, b, i: (b, i, g))] + kv_specs(ng) + kv_specs(2 * ng)
        + [pl.BlockSpec((None, 2 * NA_ROWS - 2, NA_GROUP, GRID_W, 2 * GRID_W), lambda g, b, i: (g, 0, 0, 0, 0))],
        out_specs=pl.BlockSpec((None, tq, LANE), lambda g, b, i: (b, i, g)),
        out_shape=jax.ShapeDtypeStruct((bsz, t, NA_HEADS * NA_HEAD_DIM), BF16),
        scratch_shapes=[pltpu.VMEM((cat_rows, LANE), BF16), pltpu.VMEM((cat_rows, LANE), BF16)],
        compiler_params=_cparams(("parallel", "parallel", "parallel")),
        name="neighborhood_attention",
    )(qkv, *([qkv] * 6), tbl)


def _post_kernel(o_ref, w_ref, x_ref, g1_ref, n2_ref, sc_ref, sh_ref, wrc_ref, xn_ref, hf_ref, lg_ref):
    y = jnp.dot(o_ref[...], w_ref[...], preferred_element_type=F32)
    xn = x_ref[...] + g1_ref[...] * y
    xn_ref[...] = xn
    hf = _rms(xn) * n2_ref[...] * (1.0 + sc_ref[...]) + sh_ref[...]
    tm = hf.shape[0]
    for j in range(hf.shape[1] // LANE):
        hf_ref[pl.ds(j, tm, stride=MOE_SLAB), :] = hf[:, j * LANE:(j + 1) * LANE]
    hh, hl = _split2(hf)
    wrc = wrc_ref[...]
    both = jnp.dot(hh, wrc, preferred_element_type=F32)
    lg_ref[...] = both[:, :LANE] + both[:, LANE:] + jnp.dot(hl, wrc[:, :LANE], preferred_element_type=F32)


def _post_mixer(o, w_out, x, g1, n2g, sc2, sh2, wr_cat, *, tm):
    bsz, t, d = x.shape
    kdim = o.shape[-1]
    tm = min(tm, t)
    vec = pl.BlockSpec((None, 1, d), lambda b, i: (b, 0, 0))
    tile = pl.BlockSpec((None, tm, d), lambda b, i: (b, i, 0))
    return pl.pallas_call(
        _post_kernel,
        grid=(bsz, t // tm),
        in_specs=[
            pl.BlockSpec((None, tm, kdim), lambda b, i: (b, i, 0)),
            pl.BlockSpec((kdim, d), lambda b, i: (0, 0)),
            tile, vec,
            pl.BlockSpec((1, d), lambda b, i: (0, 0)),
            vec, vec,
            pl.BlockSpec((d, 2 * LANE), lambda b, i: (0, 0)),
        ],
        out_specs=[tile, pl.BlockSpec((tm * MOE_SLAB, LANE), lambda b, i: (b * (t // tm) + i, 0)),
                   pl.BlockSpec((None, tm, LANE), lambda b, i: (b, i, 0))],
        out_shape=[jax.ShapeDtypeStruct((bsz, t, d), F32), jax.ShapeDtypeStruct((bsz * t * MOE_SLAB, LANE), F32),
                   jax.ShapeDtypeStruct((bsz, t, LANE), F32)],
        compiler_params=_cparams(("parallel", "parallel")),
        name="post_mixer",
    )(o, w_out, x, g1, n2g.reshape(1, d), sc2, sh2, wr_cat)


ROUTER_ROWS = 8 + MOE_EXPERTS


def _router_kernel(lt_ref, b_ref, id_ref, w_ref):
    lt = lt_ref[...] + b_ref[...]
    tn = lt.shape[1]
    lg = lt[0:MOE_GROUPS]
    e = jnp.exp(lg - jnp.max(lg, axis=0, keepdims=True))
    gp = e / jnp.sum(e, axis=0, keepdims=True)
    g_p = jnp.max(gp, axis=0, keepdims=True)
    rg = lax.broadcasted_iota(jnp.int32, (MOE_GROUPS, tn), 0)
    g_idx = jnp.min(jnp.where(gp == g_p, rg, MOE_GROUPS), axis=0, keepdims=True)
    el = jnp.zeros((MOE_EPG, tn), F32)
    for g in range(MOE_GROUPS):
        el = jnp.where(g_idx == g, lt[8 + g * MOE_EPG:8 + (g + 1) * MOE_EPG], el)
    ee = jnp.exp(el - jnp.max(el, axis=0, keepdims=True))
    ep = ee / jnp.sum(ee, axis=0, keepdims=True)
    re = lax.broadcasted_iota(jnp.int32, (MOE_EPG, tn), 0)
    p1 = jnp.max(ep, axis=0, keepdims=True)
    i1 = jnp.min(jnp.where(ep == p1, re, MOE_EPG), axis=0, keepdims=True)
    ep2 = jnp.where(re == i1, -1.0, ep)
    p2 = jnp.max(ep2, axis=0, keepdims=True)
    i2 = jnp.min(jnp.where(ep2 == p2, re, MOE_EPG), axis=0, keepdims=True)
    den = p1 + p2
    id_ref[0:1, :] = g_idx * MOE_EPG + i1
    id_ref[1:2, :] = g_idx * MOE_EPG + i2
    w_ref[0:1, :] = g_p * (p1 / den)
    w_ref[1:2, :] = g_p * (p2 / den)


def _router(logits_t, bias_col, *, tn):
    n = logits_t.shape[1]
    tn = min(tn, n)
    return pl.pallas_call(
        _router_kernel,
        grid=(n // tn,),
        in_specs=[pl.BlockSpec((ROUTER_ROWS, tn), lambda i: (0, i)),
                  pl.BlockSpec((ROUTER_ROWS, 1), lambda i: (0, 0))],
        out_specs=[pl.BlockSpec((MOE_TOP_K, tn), lambda i: (0, i)), pl.BlockSpec((MOE_TOP_K, tn), lambda i: (0, i))],
        out_shape=[jax.ShapeDtypeStruct((MOE_TOP_K, n), jnp.int32), jax.ShapeDtypeStruct((MOE_TOP_K, n), F32)],
        compiler_params=_cparams(("parallel",)),
        name="router_topk",
    )(logits_t, bias_col)


MOE_DMA_UNROLL = 8


def _moe_kernel(pb_ref, pe_ref, r0_ref, r1_ref, fl_ref, np_ref, src_cur, src_nxt, dst_prv, dst_lst, hf_hbm,
                wgu_ref, wd_ref, y_hbm, wgu_bf, wd_bf, xbuf, obuf, gsem, ssem, *, n_blocks):
    s = pl.program_id(0)
    blk = xbuf.shape[1] // MOE_PITCH
    ff = wd_ref.shape[0]
    b = pb_ref[s]
    slot = b % 2
    flags = fl_ref[s]
    active = s < np_ref[0]
    first = jnp.logical_and(active, (flags & 1) != 0)
    new_w = jnp.logical_and(active, (flags & 4) != 0)

    def gather_copy(src_ref, sl, r):
        return pltpu.make_async_copy(hf_hbm.at[pl.ds(pl.multiple_of(src_ref[0, r] * MOE_SLAB, MOE_SLAB), MOE_SLAB)],
                                     xbuf.at[sl, pl.ds(r * MOE_PITCH, MOE_SLAB)], gsem.at[sl])

    def scatter_copy(dst_ref, sl, r):
        return pltpu.make_async_copy(obuf.at[sl, pl.ds(r * MOE_PITCH, MOE_SLAB)],
                                     y_hbm.at[pl.ds(pl.multiple_of(dst_ref[0, r] * MOE_SLAB, MOE_SLAB), MOE_SLAB)],
                                     ssem.at[sl])

    def issue_loop(make):
        def body(g, carry):
            for j in range(MOE_DMA_UNROLL):
                make(g * MOE_DMA_UNROLL + j).start(priority=j % 2)
            return carry
        lax.fori_loop(0, blk // MOE_DMA_UNROLL, body, 0)

    def wait_gather(sl):
        pltpu.make_async_copy(hf_hbm.at[pl.ds(0, blk * MOE_SLAB)], xbuf.at[sl, pl.ds(0, blk * MOE_SLAB)],
                              gsem.at[sl]).wait()

    def wait_scatter(sl):
        pltpu.make_async_copy(obuf.at[sl, pl.ds(0, blk * MOE_SLAB)], y_hbm.at[pl.ds(0, blk * MOE_SLAB)],
                              ssem.at[sl]).wait()

    @pl.when(s == 0)
    def _():
        obuf[...] = jnp.zeros(obuf.shape, F32)
        issue_loop(lambda r: gather_copy(src_cur, 0, r))

    @pl.when(first)
    def _():
        wait_gather(slot)

        @pl.when(b >= 2)
        def _():
            wait_scatter(slot)

    @pl.when(new_w)
    def _():
        wgu_bf[...] = wgu_ref[...].astype(BF16)
        wd_bf[...] = wd_ref[...].astype(BF16)

    def compute():
        x = jnp.concatenate([xbuf[slot, pl.ds(j, blk, stride=MOE_PITCH), :] for j in range(MOE_SLAB)],
                            axis=1).astype(BF16)
        hgu = jnp.dot(x, wgu_bf[...], preferred_element_type=F32)
        g = hgu[:, :ff]
        u = hgu[:, ff:]
        act = (g * jax.nn.sigmoid(g) * u).astype(BF16)
        y = jnp.dot(act, wd_bf[...], preferred_element_type=F32)
        ri = lax.broadcasted_iota(jnp.int32, (blk, 1), 0)
        mine = jnp.logical_and(ri >= r0_ref[s], ri < r1_ref[s])
        for j in range(MOE_SLAB):
            rows = pl.ds(j, blk, stride=MOE_PITCH)
            obuf[slot, rows, :] = jnp.where(mine, y[:, j * LANE:(j + 1) * LANE], obuf[slot, rows, :])

    @pl.when(first)
    def _():
        do_gather = b + 1 < n_blocks
        do_scatter = b >= 1
        for r in range(blk):
            @pl.when(do_gather)
            def _():
                gather_copy(src_nxt, 1 - slot, r).start(priority=r % 2)

            @pl.when(do_scatter)
            def _():
                scatter_copy(dst_prv, 1 - slot, r).start(priority=(r + 1) % 2)

        compute()

    @pl.when(jnp.logical_and(active, jnp.logical_not(first)))
    def _():
        compute()

    @pl.when(s == pl.num_programs(0) - 1)
    def _():
        last_slot = (n_blocks - 1) % 2
        issue_loop(lambda r: scatter_copy(dst_lst, last_slot, r))
        wait_scatter(0)
        wait_scatter(1)


def _moe_plan(expert_id, blk):
    flat_e = expert_id.reshape(-1)
    n_asg = flat_e.shape[0]
    n_blocks = n_asg // blk
    n_steps = n_blocks + MOE_EXPERTS - 1
    _, asg = lax.sort_key_val(flat_e, jnp.arange(n_asg, dtype=jnp.int32))
    counts = jnp.sum((flat_e[:, None] == jnp.arange(MOE_EXPERTS, dtype=jnp.int32)[None, :]).astype(jnp.int32), axis=0)
    ends = jnp.cumsum(counts)
    starts = ends - counts
    src_tok = asg // MOE_TOP_K
    dst_row = (asg % MOE_TOP_K) * (n_asg // MOE_TOP_K) + asg // MOE_TOP_K
    bidx = jnp.arange(n_blocks, dtype=jnp.int32)
    e_lo = jnp.searchsorted(ends, bidx * blk, side='right').astype(jnp.int32)
    e_hi = jnp.searchsorted(ends, (bidx + 1) * blk - 1, side='right').astype(jnp.int32)
    n_pair_b = e_hi - e_lo + 1
    pair_end = jnp.cumsum(n_pair_b)
    pair_start = pair_end - n_pair_b
    n_pairs = pair_end[-1]
    sidx = jnp.arange(n_steps, dtype=jnp.int32)
    pb = jnp.minimum(jnp.searchsorted(pair_end, sidx, side='right').astype(jnp.int32), n_blocks - 1)
    pe = jnp.clip(e_lo[pb] + sidx - pair_start[pb], 0, MOE_EXPERTS - 1)
    pe = jnp.where(sidx < n_pairs, pe, pe[jnp.maximum(n_pairs - 1, 0)])
    r0 = jnp.clip(starts[pe] - pb * blk, 0, blk)
    r1 = jnp.clip(ends[pe] - pb * blk, 0, blk)
    prev_e = jnp.concatenate([jnp.full((1,), -1, jnp.int32), pe[:-1]])
    flags = ((sidx == pair_start[pb]).astype(jnp.int32) + 2 * (sidx == pair_end[pb] - 1).astype(jnp.int32)
             + 4 * (pe != prev_e).astype(jnp.int32))
    flags = jnp.where(sidx < n_pairs, flags, 0)
    i32 = lambda a: a.astype(jnp.int32)
    return (i32(pb), i32(pe), i32(r0), i32(r1), i32(flags), i32(n_pairs).reshape(1),
            src_tok.reshape(n_blocks, 1, blk), dst_row.reshape(n_blocks, 1, blk))


def _moe_experts(hf, expert_id, w_gu, w_d, layer, *, blk):
    n_tok, d = hf.shape[0] // MOE_SLAB, hf.shape[1] * MOE_SLAB
    n_asg = n_tok * MOE_TOP_K
    blk = min(blk, n_asg // 2)
    n_blocks = n_asg // blk
    pb, pe, r0, r1, flags, n_pairs, src_tok, dst_row = _moe_plan(expert_id, blk)
    ff = w_d.shape[2]
    smem = functools.partial(pl.BlockSpec, memory_space=pltpu.SMEM)
    grid_spec = pltpu.PrefetchScalarGridSpec(
        num_scalar_prefetch=6,
        grid=(pb.shape[0],),
        in_specs=[
            smem((None, 1, blk), lambda s, pb, *_: (pb[s], 0, 0)),
            smem((None, 1, blk), lambda s, pb, *_: (jnp.minimum(pb[s] + 1, n_blocks - 1), 0, 0)),
            smem((None, 1, blk), lambda s, pb, *_: (jnp.maximum(pb[s] - 1, 0), 0, 0)),
            smem((None, 1, blk), lambda s, pb, *_: (n_blocks - 1, 0, 0)),
            pl.BlockSpec(memory_space=pl.ANY),
            pl.BlockSpec((None, None, d, 2 * ff), lambda s, pb, pe, *_: (layer, pe[s], 0, 0)),
            pl.BlockSpec((None, None, ff, d), lambda s, pb, pe, *_: (layer, pe[s], 0, 0)),
        ],
        out_specs=pl.BlockSpec(memory_space=pl.ANY),
        scratch_shapes=[
            pltpu.VMEM((d, 2 * ff), BF16),
            pltpu.VMEM((ff, d), BF16),
            pltpu.VMEM((2, blk * MOE_PITCH, LANE), F32),
            pltpu.VMEM((2, blk * MOE_PITCH, LANE), F32),
            pltpu.SemaphoreType.DMA((2,)),
            pltpu.SemaphoreType.DMA((2,)),
        ],
    )
    return pl.pallas_call(
        functools.partial(_moe_kernel, n_blocks=n_blocks),
        grid_spec=grid_spec,
        out_shape=jax.ShapeDtypeStruct((n_asg * MOE_SLAB, LANE), F32),
        compiler_params=_cparams(("arbitrary",)),
        name="moe_experts",
    )(pb, pe, r0, r1, flags, n_pairs, src_tok, src_tok, dst_row, dst_row, hf, w_gu, w_d)


def _combine_kernel(x_ref, y0_ref, y1_ref, w_ref, g2_ref, fg_ref, o_ref, *, final):
    w = w_ref[...]
    tm = w.shape[0]

    def rows_to_lanes(y_ref):
        return jnp.concatenate([y_ref[pl.ds(j, tm, stride=MOE_SLAB), :] for j in range(MOE_SLAB)], axis=1)

    moe = w[:, 0:1] * rows_to_lanes(y0_ref) + w[:, 1:2] * rows_to_lanes(y1_ref)
    xn = x_ref[...] + g2_ref[...] * moe
    if final:
        xn = _rms(xn) * fg_ref[...]
    o_ref[...] = xn


def _combine(x, ybuf, wts, g2, final_g, *, final, tm):
    bsz, t, d = x.shape
    tm = min(tm, t)
    nt = t // tm
    nb = bsz * nt
    return pl.pallas_call(
        functools.partial(_combine_kernel, final=final),
        grid=(bsz, nt),
        in_specs=[
            pl.BlockSpec((None, tm, d), lambda b, i: (b, i, 0)),
            pl.BlockSpec((tm * MOE_SLAB, LANE), lambda b, i: (b * nt + i, 0)),
            pl.BlockSpec((tm * MOE_SLAB, LANE), lambda b, i: (nb + b * nt + i, 0)),
            pl.BlockSpec((tm, MOE_TOP_K), lambda b, i: (b * nt + i, 0)),
            pl.BlockSpec((None, 1, d), lambda b, i: (b, 0, 0)),
            pl.BlockSpec((1, d), lambda b, i: (0, 0)),
        ],
        out_specs=pl.BlockSpec((None, tm, d), lambda b, i: (b, i, 0)),
        out_shape=jax.ShapeDtypeStruct((bsz, t, d), F32),
        compiler_params=_cparams(("parallel", "parallel")),
        name="moe_combine_final" if final else "moe_combine",
    )(x, ybuf, ybuf, wts, g2, final_g.reshape(1, d))


def _router_weights(rg_w, rg_b, re_w, re_b):
    d = rg_w.shape[0]
    wr = jnp.zeros((d, LANE), F32).at[:, 0:MOE_GROUPS].set(rg_w).at[:, 8:8 + MOE_EXPERTS].set(re_w)
    bias = jnp.zeros((ROUTER_ROWS, 1), F32).at[0:MOE_GROUPS, 0].set(rg_b).at[8:, 0].set(re_b)
    return jnp.concatenate(_bf16_parts(wr, 2), axis=1), bias


def _col_scale(n, n_scaled, scale):
    return jnp.concatenate([jnp.full((n_scaled,), scale, F32), jnp.ones((n - n_scaled,), F32)])


def kernel(x, c, ada_w, ada_b, norm1_g, norm2_g, router_g_w, router_g_b, router_e_w, router_e_b, moe_w_gu, moe_w_d, da_w_in, da_w_out, da_lam_q1, da_lam_k1, da_lam_q2, da_lam_k2, da_subln_g, gla_w_in, gla_gate_w2_f, gla_gate_b_f, gla_gate_w2_b, gla_gate_b_b, gla_norm_g, gla_w_out, na_w_in, na_rpb, na_w_out, final_g):
    bsz, t, d = x.shape
    n_tok = bsz * t
    mod = _ada_mod(c, ada_w, ada_b)
    wts = ybuf = g2 = None
    for i in range(DEPTH):
        sh1, sc1, g1, sh2, sc2, g2_i = [mod[i, :, None, m * d:(m + 1) * d] for m in range(6)]
        if i > 0:
            x = _combine(x, ybuf, wts, g2, final_g, final=False, tm=512)
        kind, j = i % N_MIXERS, i // N_MIXERS
        if kind == 0:
            w_in = da_w_in[j].astype(BF16)
            cs = _col_scale(w_in.shape[1], DA_HEADS * DA_VAL_DIM, DA_HEAD_DIM ** -0.5 * LOG2E)
            qkv = _norm_proj(x, norm1_g[i], sc1, sh1, w_in, cs, tm=1024, tn=2048, out_dtype=BF16)
            lam_vecs = jnp.stack([da_lam_q1[j], da_lam_k1[j], da_lam_q2[j], da_lam_k2[j]]).astype(F32)
            lam_init = 0.8 - 0.6 * math.exp(-0.3 * i)
            o = _diff_attention(qkv, lam_vecs, da_subln_g[j], lam_init, tile=1024)
            w_out = da_w_out[j]
        elif kind == 1:
            n_main = 2 * GLA_HEADS * GLA_DK + 2 * GLA_HEADS * GLA_DV
            w_main = gla_w_in[j][:, :n_main].astype(BF16)
            w_gate = jnp.zeros((d, LANE), F32).at[:, :2 * GLA_GATE_RANK].set(gla_w_in[j][:, n_main:]).astype(BF16)
            cs = _col_scale(n_main, GLA_HEADS * GLA_DK, GLA_DK ** -0.5)
            qkvr = _norm_proj(x, norm1_g[i], sc1, sh1, w_main, cs, tm=1024, tn=2048, out_dtype=BF16)
            g_lr = _norm_proj(x, norm1_g[i], sc1, sh1, w_gate, jnp.ones((LANE,), F32), tm=1024, tn=LANE,
                              out_dtype=F32)
            outs = []
            for rev, w2, gb in ((False, gla_gate_w2_f[j], gla_gate_b_f[j]), (True, gla_gate_w2_b[j], gla_gate_b_b[j])):
                r0 = GLA_GATE_RANK if rev else 0
                w2p = jnp.zeros((LANE, w2.shape[1]), F32).at[r0:r0 + GLA_GATE_RANK].set(w2)
                w2h, w2l = _bf16_parts(w2p, 2)
                outs.append(_gla_scan(qkvr, g_lr, w2h, w2l, gb.reshape(1, -1).astype(F32), reverse=rev, blk=512))
            o = _gla_finish(outs[0], outs[1], qkvr, gla_norm_g[j], tm=512)
            w_out = gla_w_out[j]
        else:
            w_in = na_w_in[j].astype(BF16)
            cs = _col_scale(w_in.shape[1], NA_HEADS * NA_HEAD_DIM, NA_HEAD_DIM ** -0.5 * LOG2E)
            qkv = _norm_proj(x, norm1_g[i], sc1, sh1, w_in, cs, tm=1024, tn=2048, out_dtype=BF16)
            o = _neighborhood_attention(qkv, na_rpb[j])
            w_out = na_w_out[j]
        wr_cat, r_bias = _router_weights(router_g_w[i], router_g_b[i], router_e_w[i], router_e_b[i])
        x, hf, logits = _post_mixer(o, w_out.astype(BF16), x, g1, norm2_g[i], sc2, sh2, wr_cat, tm=256)
        logits_t = logits.reshape(n_tok, LANE).T[:ROUTER_ROWS]
        ids, wt = _router(logits_t, r_bias, tn=2048)
        ybuf = _moe_experts(hf, ids.T, moe_w_gu, moe_w_d, i, blk=MOE_BLK)
        wts = wt.T
        g2 = g2_i
    return _combine(x, ybuf, wts, g2, final_g, final=True, tm=512)
```

```python
import functools
import math

import numpy as np
import jax
import jax.numpy as jnp
from jax import lax
from jax.experimental import pallas as pl
from jax.experimental.pallas import tpu as pltpu

F32 = jnp.float32
BF16 = jnp.bfloat16

D_MODEL = 2048
DEPTH = 4
N_MIXERS = 3
NORM_EPS = 1e-6
NEG_INF = -1e30
LOG2E = 1.4426950408889634

DA_HEADS = 8
DA_HEAD_DIM = 128
DA_VAL_DIM = 2 * DA_HEAD_DIM
DA_ROW_BLK = 16

GLA_HEADS = 4
GLA_DK = 256
GLA_DV = 512
GLA_GATE_RANK = 16
GLA_TAU = 16.0
GLA_CHUNK = 64

GRID_W = 64
NA_HEADS = 64
NA_HEAD_DIM = 32
NA_ROWS = 8
NA_COLS = 16
NA_GROUP = 4
NA_QROWS = 64

MOE_GROUPS = 4
MOE_EPG = 8
MOE_EXPERTS = 32
MOE_TOP_K = 2
MOE_D_FF = 512
MOE_BLK = 256
MOE_SLAB = D_MODEL // 128
MOE_PITCH = 20

LANE = 128
VMEM_LIMIT = 56 * 1024 * 1024


def _cparams(sem):
    return pltpu.CompilerParams(dimension_semantics=sem, vmem_limit_bytes=VMEM_LIMIT)


def _split2(v):
    hi = v.astype(BF16)
    lo = (v - hi.astype(F32)).astype(BF16)
    return hi, lo


def _split3(v):
    hi = v.astype(BF16)
    r = v - hi.astype(F32)
    mid = r.astype(BF16)
    lo = (r - mid.astype(F32)).astype(BF16)
    return hi, mid, lo


def _bf16_parts(v, n):
    parts = []
    r = v.astype(F32)
    for _ in range(n):
        top = lax.bitcast_convert_type(
            lax.bitcast_convert_type(r, jnp.uint32) & jnp.uint32(0xFFFF0000), F32)
        parts.append(top.astype(BF16))
        r = r - top
    return parts


def _nt_dot(a, b):
    return lax.dot_general(a, b, (((1,), (1,)), ((), ())), preferred_element_type=F32)


def _tn_dot(a, b):
    return lax.dot_general(a, b, (((0,), (0,)), ((), ())), preferred_element_type=F32)


def _rms(x):
    return x * lax.rsqrt(jnp.mean(x * x, axis=-1, keepdims=True) + NORM_EPS)


def _ada_kernel(c_ref, w_ref, b_ref, o_ref):
    c = c_ref[...]
    cond = (c * jax.nn.sigmoid(c)).astype(BF16)
    o_ref[...] = jnp.dot(cond, w_ref[...].astype(BF16), preferred_element_type=F32) + b_ref[...]


def _ada_mod(c, ada_w, ada_b):
    depth, d, n = ada_w.shape
    bsz = c.shape[0]
    rows = 8
    c8 = jnp.zeros((rows, d), F32).at[:bsz].set(c)
    tn = 1024
    out = pl.pallas_call(
        _ada_kernel,
        grid=(depth, n // tn),
        in_specs=[
            pl.BlockSpec((rows, d), lambda l, j: (0, 0)),
            pl.BlockSpec((None, d, tn), lambda l, j: (l, 0, j)),
            pl.BlockSpec((None, 1, tn), lambda l, j: (l, 0, j)),
        ],
        out_specs=pl.BlockSpec((None, rows, tn), lambda l, j: (l, 0, j)),
        out_shape=jax.ShapeDtypeStruct((depth, rows, n), F32),
        compiler_params=_cparams(("parallel", "parallel")),
        name="ada_mod",
    )(c8, ada_w, ada_b.reshape(depth, 1, n))
    return out[:, :bsz]


def _proj_kernel(x_ref, g_ref, sc_ref, sh_ref, w_ref, cs_ref, o_ref, h_scr):
    @pl.when(pl.program_id(2) == 0)
    def _():
        h = _rms(x_ref[...]) * g_ref[...] * (1.0 + sc_ref[...]) + sh_ref[...]
        h_scr[...] = h.astype(BF16)

    acc = jnp.dot(h_scr[...], w_ref[...], preferred_element_type=F32)
    o_ref[...] = (acc * cs_ref[...]).astype(o_ref.dtype)


def _norm_proj(x, g, sc, sh, w, col_scale, *, tm, tn, out_dtype):
    bsz, t, d = x.shape
    n = w.shape[1]
    tm = min(tm, t)
    return pl.pallas_call(
        _proj_kernel,
        grid=(bsz, t // tm, n // tn),
        in_specs=[
            pl.BlockSpec((None, tm, d), lambda b, i, j: (b, i, 0)),
            pl.BlockSpec((1, d), lambda b, i, j: (0, 0)),
            pl.BlockSpec((None, 1, d), lambda b, i, j: (b, 0, 0)),
            pl.BlockSpec((None, 1, d), lambda b, i, j: (b, 0, 0)),
            pl.BlockSpec((d, tn), lambda b, i, j: (0, j)),
            pl.BlockSpec((1, tn), lambda b, i, j: (0, j)),
        ],
        out_specs=pl.BlockSpec((None, tm, tn), lambda b, i, j: (b, i, j)),
        out_shape=jax.ShapeDtypeStruct((bsz, t, n), out_dtype),
        scratch_shapes=[pltpu.VMEM((tm, d), BF16)],
        compiler_params=_cparams(("parallel", "parallel", "arbitrary")),
        name="norm_proj",
    )(x, g.reshape(1, d), sc, sh, w, col_scale.reshape(1, n))


def _da_kernel(c_ref, skip_ref, kidx_ref, q_ref, k_ref, v_ref, qf_ref, kf_ref, u_ref, lam_ref, g_ref, o_ref,
               qa_scr, m_scr, l_scr, acc_scr, s_scr, p_scr, *, lam_init):
    h = pl.program_id(1)
    qi = pl.program_id(2)
    kb = pl.program_id(3)
    nk = pl.num_programs(3)
    hd = DA_HEAD_DIM
    n_split, hr, tk = s_scr.shape[0] // 2, s_scr.shape[1], s_scr.shape[2]
    kt = lax.rem(qi + kb, nk)
    flat = ((pl.program_id(0) * pl.num_programs(1) + h) * pl.num_programs(2) + qi) * nk + kb

    @pl.when(kb == 0)
    def _():
        m_scr[...] = jnp.full(m_scr.shape, -jnp.inf, F32)
        l_scr[...] = jnp.zeros(l_scr.shape, F32)
        acc_scr[...] = jnp.zeros(acc_scr.shape, F32)
        q = q_ref[...]
        qf = qf_ref[...]
        for m in range(2):
            qm = q[:, m * hd:(m + 1) * hd]
            qa_scr[m] = jnp.concatenate([qm, qf], axis=1)

    def step(diag):
        sgn = jnp.where(kt > qi, -1.0, 1.0).astype(BF16)
        k = k_ref[...]
        kf = kf_ref[...] * sgn
        v = v_ref[...]
        c2 = 2.0 * c_ref[h]
        for m in range(2):
            ka = jnp.concatenate([k[:, m * hd:(m + 1) * hd], kf], axis=1)
            for sp in range(n_split):
                s_scr[m * n_split + sp] = _nt_dot(qa_scr[m, sp * hr:(sp + 1) * hr, :], ka)
        nlt = tk // LANE
        blocks = [slice(rb * DA_ROW_BLK, (rb + 1) * DA_ROW_BLK) for rb in range(hr // DA_ROW_BLK)]
        for m in range(2):
            for sp in range(n_split):
                ci = m * n_split + sp
                rs = slice(sp * hr, (sp + 1) * hr)
                pmax = []
                for loc in blocks:
                    s = s_scr[ci, loc, :]
                    if diag:
                        s = s - c2 * u_ref[sp * hr + loc.start:sp * hr + loc.stop, :]
                        s_scr[ci, loc, :] = s
                    pm = s[:, 0:LANE]
                    for lt in range(1, nlt):
                        pm = jnp.maximum(pm, s[:, lt * LANE:(lt + 1) * LANE])
                    pmax.append(pm)
                m_old = m_scr[m, rs, :]
                m_new = jnp.maximum(m_old, jnp.max(jnp.concatenate(pmax, axis=0), axis=1, keepdims=True))
                alpha = jnp.exp2(m_old - m_new)
                m_scr[m, rs, :] = m_new
                psum = []
                for loc in blocks:
                    s = s_scr[ci, loc, :]
                    mb = m_new[loc]
                    acc_l = None
                    for lt in range(nlt):
                        p = jnp.exp2(s[:, lt * LANE:(lt + 1) * LANE] - mb)
                        acc_l = p if acc_l is None else acc_l + p
                        p_scr[ci, loc, lt * LANE:(lt + 1) * LANE] = p.astype(BF16)
                    psum.append(acc_l)
                row_sum = jnp.sum(jnp.concatenate(psum, axis=0), axis=1, keepdims=True)
                l_scr[m, rs, :] = alpha * l_scr[m, rs, :] + row_sum
                pv = jnp.dot(p_scr[ci], v, preferred_element_type=F32)
                acc_scr[m, rs, :] = jnp.concatenate([alpha, alpha], axis=1) * acc_scr[m, rs, :] + pv

    @pl.when(kb == 0)
    def _():
        step(True)

    @pl.when(jnp.logical_and(kb > 0, skip_ref[flat] == 0))
    def _():
        step(False)

    @pl.when(kb == nk - 1)
    def _():
        lv = lam_ref[...]
        lam = (jnp.exp(jnp.sum(lv[0:1] * lv[1:2], axis=1, keepdims=True))
               - jnp.exp(jnp.sum(lv[2:3] * lv[3:4], axis=1, keepdims=True)) + lam_init)
        inv = [1.0 / l_scr[m] for m in range(2)]
        o = (acc_scr[0] * jnp.concatenate([inv[0], inv[0]], axis=1)
             - lam * (acc_scr[1] * jnp.concatenate([inv[1], inv[1]], axis=1)))
        o = _rms(o) * g_ref[...] * (1.0 - lam_init)
        o_ref[...] = o.astype(o_ref.dtype)


def _alibi_features(t, n_heads):
    slopes = 2.0 ** (-8.0 * np.arange(1, n_heads + 1) / n_heads)
    c = jnp.asarray(slopes * LOG2E, F32)
    cp = c[:, None] * jnp.arange(t, dtype=F32)[None, :]
    p3 = jnp.stack(_bf16_parts(cp, 3), axis=-1)
    ones = jnp.ones((n_heads, t, 3), BF16)
    pad = jnp.zeros((n_heads, t, LANE - 6), BF16)
    qf = jnp.concatenate([-p3, ones, pad], axis=-1)
    kf = jnp.concatenate([ones, p3, pad], axis=-1)
    return c, qf, kf


DA_SKIP_MARGIN = 176.0


def _norms_kernel(x_ref, o_ref):
    x = x_ref[...].astype(F32)
    cols = []
    for g in range(x.shape[1] // LANE):
        v = x[:, g * LANE:(g + 1) * LANE]
        cols.append(jnp.max(jnp.sum(v * v, axis=1, keepdims=True), axis=0, keepdims=True))
    o_ref[...] = jnp.concatenate(cols, axis=1)


def _da_tile_norms(qkv, tile):
    bsz, t, _ = qkv.shape
    width = 2 * DA_HEADS * DA_VAL_DIM
    return pl.pallas_call(
        _norms_kernel,
        grid=(bsz, t // tile),
        in_specs=[pl.BlockSpec((None, tile, width), lambda b, i: (b, i, 0))],
        out_specs=pl.BlockSpec((None, None, 1, width // LANE), lambda b, i: (b, i, 0, 0)),
        out_shape=jax.ShapeDtypeStruct((bsz, t // tile, 1, width // LANE), F32),
        compiler_params=_cparams(("parallel", "parallel")),
        name="da_tile_norms",
    )(qkv)


def _da_skip_plan(norms2, c, tile):
    bsz, nt = norms2.shape[0], norms2.shape[1]
    nh = DA_HEADS
    nrm = jnp.sqrt(norms2.reshape(bsz, nt, 2, nh, 2)) * 1.001
    nq, nkk = nrm[:, :, 0], nrm[:, :, 1]
    qi = jnp.arange(nt)[:, None]
    j = jnp.arange(nt)[None, :]
    kt = (qi + j) % nt
    bound = jnp.max(nq[:, :, None] * (nkk[:, kt] + nkk[:, :, None]), axis=-1)
    dist_min = jnp.maximum(jnp.abs(kt - qi) - 1, 0) * tile + 1
    far = c[None, None, None, :] * dist_min[None, :, :, None].astype(F32) >= DA_SKIP_MARGIN + bound
    skip = jnp.logical_and(far, (j > 0)[None, :, :, None]).transpose(0, 3, 1, 2)
    kidx = jnp.broadcast_to(kt[None, None], skip.shape)
    cols = [kidx[..., 0]]
    for jj in range(1, nt):
        cols.append(jnp.where(skip[..., jj], cols[-1], kidx[..., jj]))
    kidx = jnp.stack(cols, axis=-1)
    return skip.astype(jnp.int32).reshape(-1), kidx.astype(jnp.int32).reshape(-1)


def _diff_attention(qkv, lam_vecs, subln_g, lam_init, *, tile):
    bsz, t, _ = qkv.shape
    nh = DA_HEADS
    tq = tk = min(tile, t)
    nq = t // tq
    n_split = 2 if tq >= 4 * DA_ROW_BLK else 1
    c, qf, kf = _alibi_features(t, nh)
    skip, kidx = _da_skip_plan(_da_tile_norms(qkv, tq), c, tq)
    ii = jnp.arange(tq, dtype=F32)
    u = jnp.maximum(ii[None, :] - ii[:, None], 0.0)

    def key_tile(b, h, i, j, kidx_ref):
        return kidx_ref[((b * nh + h) * nq + i) * nq + j]

    grid_spec = pltpu.PrefetchScalarGridSpec(
        num_scalar_prefetch=3,
        grid=(bsz, nh, nq, nq),
        in_specs=[
            pl.BlockSpec((None, tq, DA_VAL_DIM), lambda b, h, i, j, c, sk, ki: (b, i, h)),
            pl.BlockSpec((None, tk, DA_VAL_DIM), lambda b, h, i, j, c, sk, ki: (b, key_tile(b, h, i, j, ki), nh + h)),
            pl.BlockSpec((None, tk, DA_VAL_DIM),
                         lambda b, h, i, j, c, sk, ki: (b, key_tile(b, h, i, j, ki), 2 * nh + h)),
            pl.BlockSpec((None, tq, LANE), lambda b, h, i, j, c, sk, ki: (h, i, 0)),
            pl.BlockSpec((None, tk, LANE), lambda b, h, i, j, c, sk, ki: (h, key_tile(b, h, i, j, ki), 0)),
            pl.BlockSpec((tq, tk), lambda b, h, i, j, c, sk, ki: (0, 0)),
            pl.BlockSpec((4, DA_HEAD_DIM), lambda b, h, i, j, c, sk, ki: (0, 0)),
            pl.BlockSpec((1, DA_VAL_DIM), lambda b, h, i, j, c, sk, ki: (0, 0)),
        ],
        out_specs=pl.BlockSpec((None, tq, DA_VAL_DIM), lambda b, h, i, j, c, sk, ki: (b, i, h)),
        scratch_shapes=[
            pltpu.VMEM((2, tq, 2 * DA_HEAD_DIM), BF16),
            pltpu.VMEM((2, tq, LANE), F32),
            pltpu.VMEM((2, tq, LANE), F32),
            pltpu.VMEM((2, tq, DA_VAL_DIM), F32),
            pltpu.VMEM((2 * n_split, tq // n_split, tk), F32),
            pltpu.VMEM((2 * n_split, tq // n_split, tk), BF16),
        ],
    )
    return pl.pallas_call(
        functools.partial(_da_kernel, lam_init=lam_init),
        grid_spec=grid_spec,
        out_shape=jax.ShapeDtypeStruct((bsz, t, nh * DA_VAL_DIM), BF16),
        compiler_params=_cparams(("parallel", "parallel", "parallel", "arbitrary")),
        name="diff_attention",
    )(c, skip, kidx, qkv, qkv, qkv, qf, kf, u, lam_vecs, subln_g.reshape(1, DA_VAL_DIM))


def _gla_kernel(qf_ref, kf_ref, vf_ref, glf_ref, qb_ref, kb_ref, vb_ref, glb_ref, w2h_ref, w2l_ref, gb_ref,
                of_ref, ob_ref, st_scr, *, nchunk):
    cs = GLA_CHUNK
    blk = nchunk * cs

    @pl.when(pl.program_id(2) == 0)
    def _():
        st_scr[...] = jnp.zeros(st_scr.shape, F32)

    ri = lax.broadcasted_iota(jnp.int32, (blk, blk), 0)
    ci = lax.broadcasted_iota(jnp.int32, (blk, blk), 1)
    same_chunk = (ri // cs) == (ci // cs)
    chunks = [slice(c * cs, (c + 1) * cs) for c in range(nchunk)]

    def prepare(d, q_ref, k_ref, v_ref, gl_ref):
        reverse = d == 1
        keep = jnp.logical_and(same_chunk, (ci >= ri) if reverse else (ci <= ri))
        tri = jnp.where(keep, 1.0, 0.0).astype(BF16)
        w2h = w2h_ref[d]
        gh, glo = _split2(gl_ref[...])
        x = (jnp.dot(gh, w2h, preferred_element_type=F32) + jnp.dot(glo, w2h, preferred_element_type=F32)
             + jnp.dot(gh, w2l_ref[d], preferred_element_type=F32) + gb_ref[d])
        la = (jnp.minimum(x, 0.0) - jnp.log1p(jnp.exp(-jnp.abs(x)))) * (1.0 / GLA_TAU)
        a1, a2, a3 = _split3(la)
        b = (jnp.dot(tri, a1, preferred_element_type=F32) + jnp.dot(tri, a2, preferred_element_type=F32)
             + jnp.dot(tri, a3, preferred_element_type=F32))
        q = q_ref[...].astype(F32)
        k = k_ref[...].astype(F32)
        v = v_ref[...]
        q_in = (q * jnp.exp(b)).astype(BF16)
        k_in = (k * jnp.exp(-b)).astype(BF16)
        att = jnp.where(keep, _nt_dot(q_in, k_in), 0.0)
        o_intra = jnp.dot(att.astype(BF16), v, preferred_element_type=F32)
        tots = [b[r.start:r.start + 1] if reverse else b[r.stop - 1:r.stop] for r in chunks]
        kvs = [_tn_dot(v[r], (k[r] * jnp.exp(tot - b[r])).astype(BF16)) for r, tot in zip(chunks, tots)]
        return q_in, o_intra, tots, kvs

    pf = prepare(0, qf_ref, kf_ref, vf_ref, glf_ref)
    pb = prepare(1, qb_ref, kb_ref, vb_ref, glb_ref)
    st = [st_scr[0], st_scr[1]]
    for c in range(nchunk):
        for d, (q_in, o_intra, tots, kvs), o_ref in ((0, pf, of_ref), (1, pb, ob_ref)):
            cc = nchunk - 1 - c if d == 1 else c
            r = chunks[cc]
            o_ref[r, :] = o_intra[r] + _nt_dot(q_in[r], st[d].astype(BF16))
            st[d] = st[d] * jnp.exp(tots[cc]) + kvs[cc]
    st_scr[0] = st[0]
    st_scr[1] = st[1]


def _gla_scan(qkvr, g_lr, w2h, w2l, gb, *, blk):
    bsz, t, _ = qkvr.shape
    nh = GLA_HEADS
    blk = min(blk, t)
    nblk = t // blk
    kq = (nh * GLA_DK) // GLA_DK
    kv = (2 * nh * GLA_DK) // GLA_DV

    def data_specs(pos):
        return [
            pl.BlockSpec((None, blk, GLA_DK), lambda b, h, i: (b, pos(i), h)),
            pl.BlockSpec((None, blk, GLA_DK), lambda b, h, i: (b, pos(i), kq + h)),
            pl.BlockSpec((None, blk, GLA_DV), lambda b, h, i: (b, pos(i), kv + h)),
            pl.BlockSpec((None, blk, LANE), lambda b, h, i: (b, pos(i), 0)),
        ]

    fwd = lambda i: i
    bwd = lambda i: nblk - 1 - i
    out_sds = jax.ShapeDtypeStruct((bsz, t, nh * GLA_DV), F32)
    return pl.pallas_call(
        functools.partial(_gla_kernel, nchunk=blk // GLA_CHUNK),
        grid=(bsz, nh, nblk),
        in_specs=data_specs(fwd) + data_specs(bwd) + [
            pl.BlockSpec((2, LANE, GLA_DK), lambda b, h, i: (0, 0, h)),
            pl.BlockSpec((2, LANE, GLA_DK), lambda b, h, i: (0, 0, h)),
            pl.BlockSpec((2, 1, GLA_DK), lambda b, h, i: (0, 0, h)),
        ],
        out_specs=[pl.BlockSpec((None, blk, GLA_DV), lambda b, h, i: (b, fwd(i), h)),
                   pl.BlockSpec((None, blk, GLA_DV), lambda b, h, i: (b, bwd(i), h))],
        out_shape=[out_sds, out_sds],
        scratch_shapes=[pltpu.VMEM((2, GLA_DV, GLA_DK), F32)],
        compiler_params=_cparams(("parallel", "parallel", "arbitrary")),
        name="gla_scan",
    )(*([qkvr] * 3), g_lr, *([qkvr] * 3), g_lr, w2h, w2l, gb)


def _gla_fin_kernel(of_ref, ob_ref, r_ref, g_ref, o_ref):
    o = _rms(of_ref[...] + ob_ref[...]) * g_ref[...]
    r = r_ref[...].astype(F32)
    o_ref[...] = (o * (r * jax.nn.sigmoid(r))).astype(o_ref.dtype)


def _gla_finish(o_f, o_b, qkvr, norm_g, *, tm):
    bsz, t, _ = o_f.shape
    nh = GLA_HEADS
    tm = min(tm, t)
    kr = (2 * nh * GLA_DK + nh * GLA_DV) // GLA_DV
    return pl.pallas_call(
        _gla_fin_kernel,
        grid=(bsz, t // tm, nh),
        in_specs=[
            pl.BlockSpec((None, tm, GLA_DV), lambda b, i, h: (b, i, h)),
            pl.BlockSpec((None, tm, GLA_DV), lambda b, i, h: (b, i, h)),
            pl.BlockSpec((None, tm, GLA_DV), lambda b, i, h: (b, i, kr + h)),
            pl.BlockSpec((1, GLA_DV), lambda b, i, h: (0, 0)),
        ],
        out_specs=pl.BlockSpec((None, tm, GLA_DV), lambda b, i, h: (b, i, h)),
        out_shape=jax.ShapeDtypeStruct((bsz, t, nh * GLA_DV), BF16),
        compiler_params=_cparams(("parallel", "parallel", "parallel")),
        name="gla_finish",
    )(o_f, o_b, qkvr, norm_g.reshape(1, GLA_DV))


def _na_kernel(q_ref, kp, km, kn, vp, vm, vn, tbl_ref, o_ref, kcat, vcat, *, n_rows):
    i = pl.program_id(2)
    w = GRID_W
    halo = (NA_ROWS // 2) * w
    main = NA_QROWS * w
    for ref_p, ref_m, ref_n, cat in ((kp, km, kn, kcat), (vp, vm, vn, vcat)):
        cat[0:halo, :] = ref_p[...]
        cat[halo:halo + main, :] = ref_m[...]
        cat[halo + main:2 * halo + main, :] = ref_n[...]
    lane_head = lax.broadcasted_iota(jnp.int32, (w, LANE), 1) // NA_HEAD_DIM
    base = i * NA_QROWS
    for rr in range(NA_QROWS):
        r = base + rr
        r0 = jnp.clip(r - NA_ROWS // 2, 0, n_rows - NA_ROWS)
        start = pl.multiple_of((NA_ROWS // 2 + r0 - base) * w, w)
        delta = r - r0
        qr = q_ref[rr * w:(rr + 1) * w, :]
        qs = jnp.concatenate([jnp.where(lane_head == hh, qr, jnp.zeros_like(qr)) for hh in range(NA_GROUP)], axis=0)
        kb = kcat[pl.ds(start, NA_ROWS * w), :]
        vb = vcat[pl.ds(start, NA_ROWS * w), :]
        s = _nt_dot(qs, kb)
        bias = jnp.concatenate(
            [tbl_ref[2 * j - delta + NA_ROWS - 1].reshape(NA_GROUP * w, 2 * w) for j in range(NA_ROWS // 2)], axis=1)
        s = s + bias
        p = jnp.exp2(s - jnp.max(s, axis=1, keepdims=True))
        l = jnp.sum(p, axis=1, keepdims=True)
        o4 = jnp.dot(p.astype(BF16), vb, preferred_element_type=F32) / l
        o = jnp.zeros((w, LANE), F32)
        for hh in range(NA_GROUP):
            o = jnp.where(lane_head == hh, o4[hh * w:(hh + 1) * w, :], o)
        o_ref[rr * w:(rr + 1) * w, :] = o.astype(o_ref.dtype)


def _na_bias_table(rpb):
    col = np.arange(GRID_W)
    col_start = np.clip(col - NA_COLS // 2, 0, GRID_W - NA_COLS)
    in_window = (col[None, :] >= col_start[:, None]) & (col[None, :] < col_start[:, None] + NA_COLS)
    col_off = np.clip(col[None, :] - col[:, None] + NA_COLS - 1, 0, 2 * NA_COLS - 2)
    cb = jnp.where(in_window[None, None], rpb.astype(F32)[:, :, col_off] * LOG2E, NEG_INF)
    pair = jnp.concatenate([cb[:, :-1], cb[:, 1:]], axis=-1)
    n_pair = 2 * NA_ROWS - 2
    pair = pair.reshape(NA_HEADS // NA_GROUP, NA_GROUP, n_pair, GRID_W, 2 * GRID_W)
    return pair.transpose(0, 2, 1, 3, 4)


def _neighborhood_attention(qkv, rpb):
    bsz, t, _ = qkv.shape
    n_rows = t // GRID_W
    ng = NA_HEADS // NA_GROUP
    tq = NA_QROWS * GRID_W
    halo = (NA_ROWS // 2) * GRID_W
    per = tq // halo
    n_halo = t // halo
    tbl = _na_bias_table(rpb)

    def kv_specs(col0):
        return [
            pl.BlockSpec((None, halo, LANE), lambda g, b, i: (b, jnp.maximum(i * per - 1, 0), col0 + g)),
            pl.BlockSpec((None, tq, LANE), lambda g, b, i: (b, i, col0 + g)),
            pl.BlockSpec((None, halo, LANE), lambda g, b, i: (b, jnp.minimum((i + 1) * per, n_halo - 1), col0 + g)),
        ]

    cat_rows = tq + 2 * halo
    return pl.pallas_call(
        functools.partial(_na_kernel, n_rows=n_rows),
        grid=(ng, bsz, t // tq),
        in_specs=[pl.BlockSpec((None, tq, LANE), lambda g, b, i: (b, i, g))] + kv_specs(ng) + kv_specs(2 * ng)
        + [pl.BlockSpec((None, 2 * NA_ROWS - 2, NA_GROUP, GRID_W, 2 * GRID_W), lambda g, b, i: (g, 0, 0, 0, 0))],
        out_specs=pl.BlockSpec((None, tq, LANE), lambda g, b, i: (b, i, g)),
        out_shape=jax.ShapeDtypeStruct((bsz, t, NA_HEADS * NA_HEAD_DIM), BF16),
        scratch_shapes=[pltpu.VMEM((cat_rows, LANE), BF16), pltpu.VMEM((cat_rows, LANE), BF16)],
        compiler_params=_cparams(("parallel", "parallel", "parallel")),
        name="neighborhood_attention",
    )(qkv, *([qkv] * 6), tbl)


def _post_kernel(o_ref, w_ref, x_ref, g1_ref, n2_ref, sc_ref, sh_ref, wrc_ref, xn_ref, hf_ref, lg_ref):
    y = jnp.dot(o_ref[...], w_ref[...], preferred_element_type=F32)
    xn = x_ref[...] + g1_ref[...] * y
    xn_ref[...] = xn
    hf = _rms(xn) * n2_ref[...] * (1.0 + sc_ref[...]) + sh_ref[...]
    tm = hf.shape[0]
    for j in range(hf.shape[1] // LANE):
        hf_ref[pl.ds(j, tm, stride=MOE_SLAB), :] = hf[:, j * LANE:(j + 1) * LANE]
    hh, hl = _split2(hf)
    wrc = wrc_ref[...]
    both = jnp.dot(hh, wrc, preferred_element_type=F32)
    lg_ref[...] = both[:, :LANE] + both[:, LANE:] + jnp.dot(hl, wrc[:, :LANE], preferred_element_type=F32)


def _post_mixer(o, w_out, x, g1, n2g, sc2, sh2, wr_cat, *, tm):
    bsz, t, d = x.shape
    kdim = o.shape[-1]
    tm = min(tm, t)
    vec = pl.BlockSpec((None, 1, d), lambda b, i: (b, 0, 0))
    tile = pl.BlockSpec((None, tm, d), lambda b, i: (b, i, 0))
    return pl.pallas_call(
        _post_kernel,
        grid=(bsz, t // tm),
        in_specs=[
            pl.BlockSpec((None, tm, kdim), lambda b, i: (b, i, 0)),
            pl.BlockSpec((kdim, d), lambda b, i: (0, 0)),
            tile, vec,
            pl.BlockSpec((1, d), lambda b, i: (0, 0)),
            vec, vec,
            pl.BlockSpec((d, 2 * LANE), lambda b, i: (0, 0)),
        ],
        out_specs=[tile, pl.BlockSpec((tm * MOE_SLAB, LANE), lambda b, i: (b * (t // tm) + i, 0)),
                   pl.BlockSpec((None, tm, LANE), lambda b, i: (b, i, 0))],
        out_shape=[jax.ShapeDtypeStruct((bsz, t, d), F32), jax.ShapeDtypeStruct((bsz * t * MOE_SLAB, LANE), F32),
                   jax.ShapeDtypeStruct((bsz, t, LANE), F32)],
        compiler_params=_cparams(("parallel", "parallel")),
        name="post_mixer",
    )(o, w_out, x, g1, n2g.reshape(1, d), sc2, sh2, wr_cat)


ROUTER_ROWS = 8 + MOE_EXPERTS


def _router_kernel(lt_ref, b_ref, id_ref, w_ref):
    lt = lt_ref[...] + b_ref[...]
    tn = lt.shape[1]
    lg = lt[0:MOE_GROUPS]
    e = jnp.exp(lg - jnp.max(lg, axis=0, keepdims=True))
    gp = e / jnp.sum(e, axis=0, keepdims=True)
    g_p = jnp.max(gp, axis=0, keepdims=True)
    rg = lax.broadcasted_iota(jnp.int32, (MOE_GROUPS, tn), 0)
    g_idx = jnp.min(jnp.where(gp == g_p, rg, MOE_GROUPS), axis=0, keepdims=True)
    el = jnp.zeros((MOE_EPG, tn), F32)
    for g in range(MOE_GROUPS):
        el = jnp.where(g_idx == g, lt[8 + g * MOE_EPG:8 + (g + 1) * MOE_EPG], el)
    ee = jnp.exp(el - jnp.max(el, axis=0, keepdims=True))
    ep = ee / jnp.sum(ee, axis=0, keepdims=True)
    re = lax.broadcasted_iota(jnp.int32, (MOE_EPG, tn), 0)
    p1 = jnp.max(ep, axis=0, keepdims=True)
    i1 = jnp.min(jnp.where(ep == p1, re, MOE_EPG), axis=0, keepdims=True)
    ep2 = jnp.where(re == i1, -1.0, ep)
    p2 = jnp.max(ep2, axis=0, keepdims=True)
    i2 = jnp.min(jnp.where(ep2 == p2, re, MOE_EPG), axis=0, keepdims=True)
    den = p1 + p2
    id_ref[0:1, :] = g_idx * MOE_EPG + i1
    id_ref[1:2, :] = g_idx * MOE_EPG + i2
    w_ref[0:1, :] = g_p * (p1 / den)
    w_ref[1:2, :] = g_p * (p2 / den)


def _router(logits_t, bias_col, *, tn):
    n = logits_t.shape[1]
    tn = min(tn, n)
    return pl.pallas_call(
        _router_kernel,
        grid=(n // tn,),
        in_specs=[pl.BlockSpec((ROUTER_ROWS, tn), lambda i: (0, i)),
                  pl.BlockSpec((ROUTER_ROWS, 1), lambda i: (0, 0))],
        out_specs=[pl.BlockSpec((MOE_TOP_K, tn), lambda i: (0, i)), pl.BlockSpec((MOE_TOP_K, tn), lambda i: (0, i))],
        out_shape=[jax.ShapeDtypeStruct((MOE_TOP_K, n), jnp.int32), jax.ShapeDtypeStruct((MOE_TOP_K, n), F32)],
        compiler_params=_cparams(("parallel",)),
        name="router_topk",
    )(logits_t, bias_col)


MOE_DMA_UNROLL = 8


def _moe_kernel(pb_ref, pe_ref, r0_ref, r1_ref, fl_ref, np_ref, src_cur, src_nxt, dst_prv, dst_lst, hf_hbm,
                wgu_ref, wd_ref, y_hbm, wgu_bf, wd_bf, xbuf, obuf, gsem, ssem, *, n_blocks):
    s = pl.program_id(0)
    blk = xbuf.shape[1] // MOE_PITCH
    ff = wd_ref.shape[0]
    b = pb_ref[s]
    slot = b % 2
    flags = fl_ref[s]
    active = s < np_ref[0]
    first = jnp.logical_and(active, (flags & 1) != 0)
    new_w = jnp.logical_and(active, (flags & 4) != 0)

    def gather_copy(src_ref, sl, r):
        return pltpu.make_async_copy(hf_hbm.at[pl.ds(pl.multiple_of(src_ref[0, r] * MOE_SLAB, MOE_SLAB), MOE_SLAB)],
                                     xbuf.at[sl, pl.ds(r * MOE_PITCH, MOE_SLAB)], gsem.at[sl])

    def scatter_copy(dst_ref, sl, r):
        return pltpu.make_async_copy(obuf.at[sl, pl.ds(r * MOE_PITCH, MOE_SLAB)],
                                     y_hbm.at[pl.ds(pl.multiple_of(dst_ref[0, r] * MOE_SLAB, MOE_SLAB), MOE_SLAB)],
                                     ssem.at[sl])

    def issue_loop(make):
        def body(g, carry):
            for j in range(MOE_DMA_UNROLL):
                make(g * MOE_DMA_UNROLL + j).start(priority=j % 2)
            return carry
        lax.fori_loop(0, blk // MOE_DMA_UNROLL, body, 0)

    def wait_gather(sl):
        pltpu.make_async_copy(hf_hbm.at[pl.ds(0, blk * MOE_SLAB)], xbuf.at[sl, pl.ds(0, blk * MOE_SLAB)],
                              gsem.at[sl]).wait()

    def wait_scatter(sl):
        pltpu.make_async_copy(obuf.at[sl, pl.ds(0, blk * MOE_SLAB)], y_hbm.at[pl.ds(0, blk * MOE_SLAB)],
                              ssem.at[sl]).wait()

    @pl.when(s == 0)
    def _():
        obuf[...] = jnp.zeros(obuf.shape, F32)
        issue_loop(lambda r: gather_copy(src_cur, 0, r))

    @pl.when(first)
    def _():
        wait_gather(slot)

        @pl.when(b >= 2)
        def _():
            wait_scatter(slot)

    @pl.when(new_w)
    def _():
        wgu_bf[...] = wgu_ref[...].astype(BF16)
        wd_bf[...] = wd_ref[...].astype(BF16)

    def compute():
        x = jnp.concatenate([xbuf[slot, pl.ds(j, blk, stride=MOE_PITCH), :] for j in range(MOE_SLAB)],
                            axis=1).astype(BF16)
        hgu = jnp.dot(x, wgu_bf[...], preferred_element_type=F32)
        g = hgu[:, :ff]
        u = hgu[:, ff:]
        act = (g * jax.nn.sigmoid(g) * u).astype(BF16)
        y = jnp.dot(act, wd_bf[...], preferred_element_type=F32)
        ri = lax.broadcasted_iota(jnp.int32, (blk, 1), 0)
        mine = jnp.logical_and(ri >= r0_ref[s], ri < r1_ref[s])
        for j in range(MOE_SLAB):
            rows = pl.ds(j, blk, stride=MOE_PITCH)
            obuf[slot, rows, :] = jnp.where(mine, y[:, j * LANE:(j + 1) * LANE], obuf[slot, rows, :])

    @pl.when(first)
    def _():
        do_gather = b + 1 < n_blocks
        do_scatter = b >= 1
        for r in range(blk):
            @pl.when(do_gather)
            def _():
                gather_copy(src_nxt, 1 - slot, r).start(priority=r % 2)

            @pl.when(do_scatter)
            def _():
                scatter_copy(dst_prv, 1 - slot, r).start(priority=(r + 1) % 2)

        compute()

    @pl.when(jnp.logical_and(active, jnp.logical_not(first)))
    def _():
        compute()

    @pl.when(s == pl.num_programs(0) - 1)
    def _():
        last_slot = (n_blocks - 1) % 2
        issue_loop(lambda r: scatter_copy(dst_lst, last_slot, r))
        wait_scatter(0)
        wait_scatter(1)


def _moe_plan(expert_id, blk):
    flat_e = expert_id.reshape(-1)
    n_asg = flat_e.shape[0]
    n_blocks = n_asg // blk
    n_steps = n_blocks + MOE_EXPERTS - 1
    _, asg = lax.sort_key_val(flat_e, jnp.arange(n_asg, dtype=jnp.int32))
    counts = jnp.sum((flat_e[:, None] == jnp.arange(MOE_EXPERTS, dtype=jnp.int32)[None, :]).astype(jnp.int32), axis=0)
    ends = jnp.cumsum(counts)
    starts = ends - counts
    src_tok = asg // MOE_TOP_K
    dst_row = (asg % MOE_TOP_K) * (n_asg // MOE_TOP_K) + asg // MOE_TOP_K
    bidx = jnp.arange(n_blocks, dtype=jnp.int32)
    e_lo = jnp.searchsorted(ends, bidx * blk, side='right').astype(jnp.int32)
    e_hi = jnp.searchsorted(ends, (bidx + 1) * blk - 1, side='right').astype(jnp.int32)
    n_pair_b = e_hi - e_lo + 1
    pair_end = jnp.cumsum(n_pair_b)
    pair_start = pair_end - n_pair_b
    n_pairs = pair_end[-1]
    sidx = jnp.arange(n_steps, dtype=jnp.int32)
    pb = jnp.minimum(jnp.searchsorted(pair_end, sidx, side='right').astype(jnp.int32), n_blocks - 1)
    pe = jnp.clip(e_lo[pb] + sidx - pair_start[pb], 0, MOE_EXPERTS - 1)
    pe = jnp.where(sidx < n_pairs, pe, pe[jnp.maximum(n_pairs - 1, 0)])
    r0 = jnp.clip(starts[pe] - pb * blk, 0, blk)
    r1 = jnp.clip(ends[pe] - pb * blk, 0, blk)
    prev_e = jnp.concatenate([jnp.full((1,), -1, jnp.int32), pe[:-1]])
    flags = ((sidx == pair_start[pb]).astype(jnp.int32) + 2 * (sidx == pair_end[pb] - 1).astype(jnp.int32)
             + 4 * (pe != prev_e).astype(jnp.int32))
    flags = jnp.where(sidx < n_pairs, flags, 0)
    i32 = lambda a: a.astype(jnp.int32)
    return (i32(pb), i32(pe), i32(r0), i32(r1), i32(flags), i32(n_pairs).reshape(1),
            src_tok.reshape(n_blocks, 1, blk), dst_row.reshape(n_blocks, 1, blk))


def _moe_experts(hf, expert_id, w_gu, w_d, layer, *, blk):
    n_tok, d = hf.shape[0] // MOE_SLAB, hf.shape[1] * MOE_SLAB
    n_asg = n_tok * MOE_TOP_K
    blk = min(blk, n_asg // 2)
    n_blocks = n_asg // blk
    pb, pe, r0, r1, flags, n_pairs, src_tok, dst_row = _moe_plan(expert_id, blk)
    ff = w_d.shape[2]
    smem = functools.partial(pl.BlockSpec, memory_space=pltpu.SMEM)
    grid_spec = pltpu.PrefetchScalarGridSpec(
        num_scalar_prefetch=6,
        grid=(pb.shape[0],),
        in_specs=[
            smem((None, 1, blk), lambda s, pb, *_: (pb[s], 0, 0)),
            smem((None, 1, blk), lambda s, pb, *_: (jnp.minimum(pb[s] + 1, n_blocks - 1), 0, 0)),
            smem((None, 1, blk), lambda s, pb, *_: (jnp.maximum(pb[s] - 1, 0), 0, 0)),
            smem((None, 1, blk), lambda s, pb, *_: (n_blocks - 1, 0, 0)),
            pl.BlockSpec(memory_space=pl.ANY),
            pl.BlockSpec((None, None, d, 2 * ff), lambda s, pb, pe, *_: (layer, pe[s], 0, 0)),
            pl.BlockSpec((None, None, ff, d), lambda s, pb, pe, *_: (layer, pe[s], 0, 0)),
        ],
        out_specs=pl.BlockSpec(memory_space=pl.ANY),
        scratch_shapes=[
            pltpu.VMEM((d, 2 * ff), BF16),
            pltpu.VMEM((ff, d), BF16),
            pltpu.VMEM((2, blk * MOE_PITCH, LANE), F32),
            pltpu.VMEM((2, blk * MOE_PITCH, LANE), F32),
            pltpu.SemaphoreType.DMA((2,)),
            pltpu.SemaphoreType.DMA((2,)),
        ],
    )
    return pl.pallas_call(
        functools.partial(_moe_kernel, n_blocks=n_blocks),
        grid_spec=grid_spec,
        out_shape=jax.ShapeDtypeStruct((n_asg * MOE_SLAB, LANE), F32),
        compiler_params=_cparams(("arbitrary",)),
        name="moe_experts",
    )(pb, pe, r0, r1, flags, n_pairs, src_tok, src_tok, dst_row, dst_row, hf, w_gu, w_d)


def _combine_kernel(x_ref, y0_ref, y1_ref, w_ref, g2_ref, fg_ref, o_ref, *, final):
    w = w_ref[...]
    tm = w.shape[0]

    def rows_to_lanes(y_ref):
        return jnp.concatenate([y_ref[pl.ds(j, tm, stride=MOE_SLAB), :] for j in range(MOE_SLAB)], axis=1)

    moe = w[:, 0:1] * rows_to_lanes(y0_ref) + w[:, 1:2] * rows_to_lanes(y1_ref)
    xn = x_ref[...] + g2_ref[...] * moe
    if final:
        xn = _rms(xn) * fg_ref[...]
    o_ref[...] = xn


def _combine(x, ybuf, wts, g2, final_g, *, final, tm):
    bsz, t, d = x.shape
    tm = min(tm, t)
    nt = t // tm
    nb = bsz * nt
    return pl.pallas_call(
        functools.partial(_combine_kernel, final=final),
        grid=(bsz, nt),
        in_specs=[
            pl.BlockSpec((None, tm, d), lambda b, i: (b, i, 0)),
            pl.BlockSpec((tm * MOE_SLAB, LANE), lambda b, i: (b * nt + i, 0)),
            pl.BlockSpec((tm * MOE_SLAB, LANE), lambda b, i: (nb + b * nt + i, 0)),
            pl.BlockSpec((tm, MOE_TOP_K), lambda b, i: (b * nt + i, 0)),
            pl.BlockSpec((None, 1, d), lambda b, i: (b, 0, 0)),
            pl.BlockSpec((1, d), lambda b, i: (0, 0)),
        ],
        out_specs=pl.BlockSpec((None, tm, d), lambda b, i: (b, i, 0)),
        out_shape=jax.ShapeDtypeStruct((bsz, t, d), F32),
        compiler_params=_cparams(("parallel", "parallel")),
        name="moe_combine_final" if final else "moe_combine",
    )(x, ybuf, ybuf, wts, g2, final_g.reshape(1, d))


def _router_weights(rg_w, rg_b, re_w, re_b):
    d = rg_w.shape[0]
    wr = jnp.zeros((d, LANE), F32).at[:, 0:MOE_GROUPS].set(rg_w).at[:, 8:8 + MOE_EXPERTS].set(re_w)
    bias = jnp.zeros((ROUTER_ROWS, 1), F32).at[0:MOE_GROUPS, 0].set(rg_b).at[8:, 0].set(re_b)
    return jnp.concatenate(_bf16_parts(wr, 2), axis=1), bias


def _col_scale(n, n_scaled, scale):
    return jnp.concatenate([jnp.full((n_scaled,), scale, F32), jnp.ones((n - n_scaled,), F32)])


def kernel(x, c, ada_w, ada_b, norm1_g, norm2_g, router_g_w, router_g_b, router_e_w, router_e_b, moe_w_gu, moe_w_d, da_w_in, da_w_out, da_lam_q1, da_lam_k1, da_lam_q2, da_lam_k2, da_subln_g, gla_w_in, gla_gate_w2_f, gla_gate_b_f, gla_gate_w2_b, gla_gate_b_b, gla_norm_g, gla_w_out, na_w_in, na_rpb, na_w_out, final_g):
    bsz, t, d = x.shape
    n_tok = bsz * t
    mod = _ada_mod(c, ada_w, ada_b)
    wts = ybuf = g2 = None
    for i in range(DEPTH):
        sh1, sc1, g1, sh2, sc2, g2_i = [mod[i, :, None, m * d:(m + 1) * d] for m in range(6)]
        if i > 0:
            x = _combine(x, ybuf, wts, g2, final_g, final=False, tm=512)
        kind, j = i % N_MIXERS, i // N_MIXERS
        if kind == 0:
            w_in = da_w_in[j].astype(BF16)
            cs = _col_scale(w_in.shape[1], DA_HEADS * DA_VAL_DIM, DA_HEAD_DIM ** -0.5 * LOG2E)
            qkv = _norm_proj(x, norm1_g[i], sc1, sh1, w_in, cs, tm=1024, tn=2048, out_dtype=BF16)
            lam_vecs = jnp.stack([da_lam_q1[j], da_lam_k1[j], da_lam_q2[j], da_lam_k2[j]]).astype(F32)
            lam_init = 0.8 - 0.6 * math.exp(-0.3 * i)
            o = _diff_attention(qkv, lam_vecs, da_subln_g[j], lam_init, tile=1024)
            w_out = da_w_out[j]
        elif kind == 1:
            n_main = 2 * GLA_HEADS * GLA_DK + 2 * GLA_HEADS * GLA_DV
            w_main = gla_w_in[j][:, :n_main].astype(BF16)
            w_gate = jnp.zeros((d, LANE), F32).at[:, :2 * GLA_GATE_RANK].set(gla_w_in[j][:, n_main:]).astype(BF16)
            cs = _col_scale(n_main, GLA_HEADS * GLA_DK, GLA_DK ** -0.5)
            qkvr = _norm_proj(x, norm1_g[i], sc1, sh1, w_main, cs, tm=1024, tn=2048, out_dtype=BF16)
            g_lr = _norm_proj(x, norm1_g[i], sc1, sh1, w_gate, jnp.ones((LANE,), F32), tm=1024, tn=LANE,
                              out_dtype=F32)
            w2s = []
            for r0, w2 in ((0, gla_gate_w2_f[j]), (GLA_GATE_RANK, gla_gate_w2_b[j])):
                w2s.append(_bf16_parts(jnp.zeros((LANE, w2.shape[1]), F32).at[r0:r0 + GLA_GATE_RANK].set(w2), 2))
            gbs = jnp.stack([gla_gate_b_f[j], gla_gate_b_b[j]]).astype(F32)[:, None, :]
            outs = _gla_scan(qkvr, g_lr, jnp.stack([w2s[0][0], w2s[1][0]]), jnp.stack([w2s[0][1], w2s[1][1]]), gbs,
                             blk=512)
            o = _gla_finish(outs[0], outs[1], qkvr, gla_norm_g[j], tm=512)
            w_out = gla_w_out[j]
        else:
            w_in = na_w_in[j].astype(BF16)
            cs = _col_scale(w_in.shape[1], NA_HEADS * NA_HEAD_DIM, NA_HEAD_DIM ** -0.5 * LOG2E)
            qkv = _norm_proj(x, norm1_g[i], sc1, sh1, w_in, cs, tm=1024, tn=2048, out_dtype=BF16)
            o = _neighborhood_attention(qkv, na_rpb[j])
            w_out = na_w_out[j]
        wr_cat, r_bias = _router_weights(router_g_w[i], router_g_b[i], router_e_w[i], router_e_b[i])
        x, hf, logits = _post_mixer(o, w_out.astype(BF16), x, g1, norm2_g[i], sc2, sh2, wr_cat, tm=256)
        logits_t = logits.reshape(n_tok, LANE).T[:ROUTER_ROWS]
        ids, wt = _router(logits_t, r_bias, tn=2048)
        ybuf = _moe_experts(hf, ids.T, moe_w_gu, moe_w_d, i, blk=MOE_BLK)
        wts = wt.T
        g2 = g2_i
    return _combine(x, ybuf, wts, g2, final_g, final=True, tm=512)
```
